```python
import math
import jax, jax.numpy as jnp
from jax import lax
import numpy as np

D_MODEL = 1024
BATCH = 2
SEQ = 8192
DEPTH = 2
DEC_BATCH = 8
DEC_SEQ = 8192
PAST_LEN = 128

GRID_W = 64
HEAD_DIM = 64
GROUP_W = D_MODEL // 4
N_HEADS_A = GROUP_W // HEAD_DIM
N_GROUPS_B = GROUP_W // HEAD_DIM
HY_CH = GROUP_W
N_HEADS_D = GROUP_W // HEAD_DIM
KH_MAX = 8
KW = 16
HY_ORDER = 2
HY_EMB = 33
HY_BANDS = (HY_EMB - 1) // 2
HY_FILT = 64
ML_CHUNK = 128
PLE_DIM = 256
D_FF = -(-8 * D_MODEL // (3 * 256)) * 256
EPS = 1e-6
A_COLS = 3 * GROUP_W
B_COLS = GROUP_W
C_COLS = 3 * HY_CH
D_COLS = 4 * GROUP_W + 4 * N_HEADS_D
N_IN = A_COLS + B_COLS + C_COLS + D_COLS

kernel_name = 'hybrid_bidir_na_fnet_hyena_mlstm'


def rmsnorm(x, g):
    xf = x.astype(jnp.float32)
    xf = xf * lax.rsqrt(jnp.mean(xf * xf, axis=-1, keepdims=True) + EPS)
    return (xf * g.astype(jnp.float32)).astype(x.dtype)


def head_rmsnorm(y, g):
    yf = y.astype(jnp.float32)
    sh = yf.shape
    yh = yf.reshape(sh[:-1] + (sh[-1] // HEAD_DIM, HEAD_DIM))
    yh = yh * lax.rsqrt(jnp.mean(yh * yh, axis=-1, keepdims=True) + EPS)
    return (yh.reshape(sh) * g.astype(jnp.float32)).astype(y.dtype)


def neighbourhood_attention(q, k, v, rpb):
    bsz, L, _ = q.shape
    rows = L // GRID_W
    kh = min(KH_MAX, rows)
    n_keys = kh * KW
    r = np.arange(rows)
    c = np.arange(GRID_W)
    key_r = np.clip(r - kh // 2, 0, rows - kh)[:, None] + np.arange(kh)[None, :]
    key_c = np.clip(c - KW // 2, 0, GRID_W - KW)[:, None] + np.arange(KW)[None, :]
    idx = (key_r[:, None, :, None] * GRID_W + key_c[None, :, None, :]).reshape(rows, GRID_W, n_keys)
    dr = (key_r - r[:, None] + KH_MAX - 1)[:, None, :, None]
    dc = (key_c - c[:, None] + KW - 1)[None, :, None, :]
    bias = rpb.astype(jnp.float32)[:, dr, dc].reshape(N_HEADS_A, rows, GRID_W, n_keys).transpose(1, 0, 2, 3)
    qr = q.reshape(bsz, rows, GRID_W, N_HEADS_A, HEAD_DIM).transpose(1, 0, 3, 2, 4)
    kk = k.reshape(bsz, L, N_HEADS_A, HEAD_DIM).transpose(0, 2, 1, 3)
    vv = v.reshape(bsz, L, N_HEADS_A, HEAD_DIM).transpose(0, 2, 1, 3)
    scale = HEAD_DIM ** -0.5

    def one_row(args):
        q_r, idx_r, bias_r = args
        kg = jnp.take(kk, idx_r, axis=2)
        vg = jnp.take(vv, idx_r, axis=2)
        s = jnp.einsum('bhwd,bhwnd->bhwn', q_r, kg).astype(jnp.float32) * scale + bias_r[None]
        pr = jax.nn.softmax(s, axis=-1).astype(vv.dtype)
        return jnp.einsum('bhwn,bhwnd->bhwd', pr, vg)

    out = lax.map(one_row, (qr, jnp.asarray(idx, dtype=jnp.int32), bias))
    return out.transpose(1, 0, 3, 2, 4).reshape(bsz, L, N_HEADS_A * HEAD_DIM)


def fourier_mix(u, w):
    bsz, L, _ = u.shape
    uf = u.astype(jnp.float32).reshape(bsz, L, N_GROUPS_B, HEAD_DIM)
    f = jnp.fft.fft2(uf, axes=(1, 3), norm='ortho').real
    y = jnp.einsum('blgc,gcd->blgd', f, w.astype(jnp.float32))
    return y.reshape(bsz, L, N_GROUPS_B * HEAD_DIM).astype(u.dtype)


def hyena_filters(L, w1, b1, freq, w2, b2, w3, decay):
    f32 = jnp.float32
    s = jnp.arange(L, dtype=f32)
    t = s / max(L - 1, 1)
    ang = (2.0 * math.pi / L) * s
    bands = jnp.linspace(1e-4, HY_BANDS - 1, HY_BANDS, dtype=f32)
    fb = ang[:, None] * bands[None, :]
    feats = jnp.concatenate([t[:, None], jnp.cos(fb), -jnp.sin(fb)], axis=-1)
    freq = freq.astype(f32)
    h = jnp.sin(freq[0] * (feats @ w1.astype(f32) + b1.astype(f32)))
    h = jnp.sin(freq[1] * (h @ w2.astype(f32) + b2.astype(f32)))
    h = (h @ w3.astype(f32)).reshape(L, HY_ORDER, 2, HY_CH)
    h = h * jnp.exp(-t[:, None, None, None] * decay.astype(f32))
    fwd = h[:, :, 0]
    bwd = h[1:, :, 1][::-1]
    full = jnp.concatenate([fwd, jnp.zeros((1, HY_ORDER, HY_CH), f32), bwd], axis=0)
    return jnp.fft.rfft(full, axis=0)


def fftconv(u, kf, skip):
    L = u.shape[1]
    U = jnp.fft.rfft(u, n=2 * L, axis=1)
    y = jnp.fft.irfft(U * kf[None], n=2 * L, axis=1)[:, :L]
    return y + u * skip


def short_conv(z, w, b):
    zp = jnp.pad(z, ((0, 0), (1, 1), (0, 0)))
    return zp[:, :-2] * w[0] + zp[:, 1:-1] * w[1] + zp[:, 2:] * w[2] + b


def hyena(u, conv_w, conv_b, w1, b1, freq, w2, b2, w3, decay, skip):
    f32 = jnp.float32
    L = u.shape[1]
    z = short_conv(u.astype(f32), conv_w.astype(f32), conv_b.astype(f32))
    v, x1, x2 = jnp.split(z, 3, axis=-1)
    kf = hyena_filters(L, w1, b1, freq, w2, b2, w3, decay)
    skip = skip.astype(f32)
    y = x1 * fftconv(v, kf[:, 0], skip[0])
    y = x2 * fftconv(y, kf[:, 1], skip[1])
    return y.astype(u.dtype)


def mlstm_chunkwise(q, k, v, ipre, fpre):
    bsz, H, L, dh = q.shape
    nc = L // ML_CHUNK
    q = q.reshape(bsz, H, nc, ML_CHUNK, dh)
    k = k.reshape(bsz, H, nc, ML_CHUNK, dh)
    v = v.reshape(bsz, H, nc, ML_CHUNK, dh)
    lf = jax.nn.log_sigmoid(fpre).reshape(bsz, H, nc, ML_CHUNK)
    li = ipre.reshape(bsz, H, nc, ML_CHUNK)
    b = jnp.cumsum(lf, axis=-1)
    g = b[..., -1]
    a = g[..., None] - b + li
    m_loc = jnp.max(a, axis=-1)
    w = jnp.exp(a - m_loc[..., None])
    S_loc = jnp.einsum('bhnc,bhncd,bhnce->bhnde', w, k, v)
    n_loc = jnp.einsum('bhnc,bhncd->bhnd', w, k)

    def step(carry, xs):
        S, n, m = carry
        g_c, m_c, S_c, n_c = xs
        m_new = jnp.maximum(g_c + m, m_c)
        d_old = jnp.exp(g_c + m - m_new)
        d_new = jnp.exp(m_c - m_new)
        S_new = d_old[..., None, None] * S + d_new[..., None, None] * S_c
        n_new = d_old[..., None] * n + d_new[..., None] * n_c
        return (S_new, n_new, m_new), (S, n, m)

    init = (jnp.zeros((bsz, H, dh, dh), jnp.float32), jnp.zeros((bsz, H, dh), jnp.float32),
            jnp.zeros((bsz, H), jnp.float32))
    xs = (jnp.moveaxis(g, 2, 0), jnp.moveaxis(m_loc, 2, 0), jnp.moveaxis(S_loc, 2, 0), jnp.moveaxis(n_loc, 2, 0))
    _, (S0, n0, m0) = lax.scan(step, init, xs)
    S0 = jnp.moveaxis(S0, 0, 2)
    n0 = jnp.moveaxis(n0, 0, 2)
    m0 = jnp.moveaxis(m0, 0, 2)
    inter = b + m0[..., None]
    lower = np.tril(np.ones((ML_CHUNK, ML_CHUNK), dtype=bool))
    Dm = jnp.where(lower, b[..., :, None] - b[..., None, :] + li[..., None, :], -jnp.inf)
    m_t = jnp.maximum(inter, jnp.max(Dm, axis=-1))
    P = jnp.exp(Dm - m_t[..., None]) * jnp.einsum('bhncd,bhnsd->bhncs', q, k)
    wi = jnp.exp(inter - m_t)
    num = wi[..., None] * jnp.einsum('bhncd,bhnde->bhnce', q, S0) + jnp.einsum('bhncs,bhnse->bhnce', P, v)
    den = wi * jnp.einsum('bhncd,bhnd->bhnc', q, n0) + jnp.sum(P, axis=-1)
    h = num / jnp.maximum(jnp.abs(den), jnp.exp(-m_t))[..., None]
    return h.reshape(bsz, H, L, dh)


def mlstm_bidir(q, k, v, o, gates, gate_b):
    f32 = jnp.float32
    bsz, L, _ = q.shape
    def heads(t):
        return t.astype(f32).reshape(bsz, L, N_HEADS_D, HEAD_DIM).transpose(0, 2, 1, 3)
    qh = heads(q)
    kh = heads(k) * (HEAD_DIM ** -0.5)
    vh = heads(v)
    gt = (gates.astype(f32).reshape(bsz, L, 4, N_HEADS_D) + gate_b.astype(f32)).transpose(2, 0, 3, 1)
    h_f = mlstm_chunkwise(qh, kh, vh, gt[0], gt[1])
    fl = lambda t: jnp.flip(t, axis=2)
    h_b = fl(mlstm_chunkwise(fl(qh), fl(kh), fl(vh), fl(gt[2]), fl(gt[3])))
    h = (h_f + h_b).transpose(0, 2, 1, 3).reshape(bsz, L, N_HEADS_D * HEAD_DIM)
    return (jax.nn.sigmoid(o.astype(f32)) * h).astype(q.dtype)


def _layer(x, p_i, norm_mix, w_in, attn_rpb, fnet_w, hy_conv_w, hy_conv_b, hy_w1, hy_b1, hy_freq, hy_w2,
           hy_b2, hy_w3, hy_decay, hy_skip, ml_gate_b, out_norm, w_out, norm_ffn, w_gate, w_up, w_down,
           ple_norm, w_ple_gate, w_ple_proj):
    xn = rmsnorm(x, norm_mix)
    z = xn @ w_in
    splits = [GROUP_W, GROUP_W, GROUP_W, B_COLS, C_COLS, GROUP_W, GROUP_W, GROUP_W, GROUP_W, 4 * N_HEADS_D]
    qa, ka, va, ub, uc, qd, kd, vd, od, gd = jnp.split(z, list(np.cumsum(splits)[:-1]), axis=-1)
    ya = neighbourhood_attention(qa, ka, va, attn_rpb)
    yb = fourier_mix(ub, fnet_w)
    yc = hyena(uc, hy_conv_w, hy_conv_b, hy_w1, hy_b1, hy_freq, hy_w2, hy_b2, hy_w3, hy_decay, hy_skip)
    yd = mlstm_bidir(qd, kd, vd, od, gd, ml_gate_b)
    y = head_rmsnorm(jnp.concatenate([ya, yb, yc, yd], axis=-1), out_norm)
    x = x + y @ w_out
    hn = rmsnorm(x, norm_ffn)
    x = x + (jax.nn.silu(hn @ w_gate) * (hn @ w_up)) @ w_down
    gate = jax.nn.sigmoid(rmsnorm(x, ple_norm) @ w_ple_gate)
    return x + gate * (p_i @ w_ple_proj)


def _trunk(x, p, layer_weights, final_norm):
    for i in range(DEPTH):
        x = _layer(x, p[i], *[w[i] for w in layer_weights])
    return rmsnorm(x, final_norm)


def setup_inputs(seed: int = 0) -> dict:
    key = jax.random.key(seed)
    ks = list(jax.random.split(key, 40))
    f32 = jnp.float32
    def nk():
        return ks.pop()
    def nrm(shape, scale):
        return jax.random.normal(nk(), shape, f32) * scale
    def gain(shape):
        return 1.0 + 0.05 * jax.random.normal(nk(), shape, f32)
    def unif(shape, lo, hi):
        return jax.random.uniform(nk(), shape, f32, lo, hi)
    inp = {}
    inp['x_prompt'] = nrm((BATCH, SEQ, D_MODEL), 1.0)
    inp['x_sample'] = nrm((DEC_BATCH, DEC_SEQ, D_MODEL), 1.0)
    inp['p_prompt'] = nrm((DEPTH, BATCH, SEQ, PLE_DIM), 1.0)
    inp['p_sample'] = nrm((DEPTH, DEC_BATCH, DEC_SEQ, PLE_DIM), 1.0)
    inp['norm_mix'] = gain((DEPTH, D_MODEL))
    inp['w_in'] = nrm((DEPTH, D_MODEL, N_IN), D_MODEL ** -0.5)
    inp['attn_rpb'] = nrm((DEPTH, N_HEADS_A, 2 * KH_MAX - 1, 2 * KW - 1), 0.1)
    inp['fnet_w'] = nrm((DEPTH, N_GROUPS_B, HEAD_DIM, HEAD_DIM), HEAD_DIM ** -0.5)
    inp['hy_conv_w'] = nrm((DEPTH, 3, C_COLS), 3 ** -0.5)
    inp['hy_conv_b'] = nrm((DEPTH, C_COLS), 0.02)
    inp['hy_w1'] = nrm((DEPTH, HY_EMB, HY_FILT), HY_EMB ** -0.5)
    inp['hy_b1'] = nrm((DEPTH, HY_FILT), 0.1)
    inp['hy_freq'] = gain((DEPTH, 2, HY_FILT))
    inp['hy_w2'] = nrm((DEPTH, HY_FILT, HY_FILT), HY_FILT ** -0.5)
    inp['hy_b2'] = nrm((DEPTH, HY_FILT), 0.1)
    inp['hy_w3'] = nrm((DEPTH, HY_FILT, HY_ORDER * 2 * HY_CH), HY_FILT ** -0.5)
    inp['hy_decay'] = unif((DEPTH, HY_ORDER, 2, HY_CH), 3.0, 15.0)
    inp['hy_skip'] = nrm((DEPTH, HY_ORDER, HY_CH), 1.0)
    ib = nrm((DEPTH, 2, N_HEADS_D), 0.1)
    fb = unif((DEPTH, 2, N_HEADS_D), 3.0, 6.0)
    inp['ml_gate_b'] = jnp.stack([ib[:, 0], fb[:, 0], ib[:, 1], fb[:, 1]], axis=1)
    inp['out_norm'] = gain((DEPTH, D_MODEL))
    inp['w_out'] = nrm((DEPTH, D_MODEL, D_MODEL), D_MODEL ** -0.5)
    inp['norm_ffn'] = gain((DEPTH, D_MODEL))
    inp['w_gate'] = nrm((DEPTH, D_MODEL, D_FF), D_MODEL ** -0.5)
    inp['w_up'] = nrm((DEPTH, D_MODEL, D_FF), D_MODEL ** -0.5)
    inp['w_down'] = nrm((DEPTH, D_FF, D_MODEL), D_FF ** -0.5)
    inp['ple_norm'] = gain((DEPTH, D_MODEL))
    inp['w_ple_gate'] = nrm((DEPTH, D_MODEL, D_MODEL), D_MODEL ** -0.5)
    inp['w_ple_proj'] = nrm((DEPTH, PLE_DIM, D_MODEL), PLE_DIM ** -0.5)
    inp['final_norm'] = gain((D_MODEL,))
    return inp


def reference(x_prompt, x_sample, p_prompt, p_sample, norm_mix, w_in, attn_rpb, fnet_w, hy_conv_w, hy_conv_b,
              hy_w1, hy_b1, hy_freq, hy_w2, hy_b2, hy_w3, hy_decay, hy_skip, ml_gate_b, out_norm, w_out,
              norm_ffn, w_gate, w_up, w_down, ple_norm, w_ple_gate, w_ple_proj, final_norm):
    layer_weights = (norm_mix, w_in, attn_rpb, fnet_w, hy_conv_w, hy_conv_b, hy_w1, hy_b1, hy_freq, hy_w2,
                     hy_b2, hy_w3, hy_decay, hy_skip, ml_gate_b, out_norm, w_out, norm_ffn, w_gate, w_up,
                     w_down, ple_norm, w_ple_gate, w_ple_proj)
    y_prompt = _trunk(x_prompt, p_prompt, layer_weights, final_norm)
    y_sample = _trunk(x_sample, p_sample, layer_weights, final_norm)
    return (y_prompt, y_sample)
```

```python
import functools
import math

import numpy as np
import jax
import jax.numpy as jnp
from jax import lax
from jax.experimental import pallas as pl
from jax.experimental.pallas import tpu as pltpu

F32, BF16 = jnp.float32, jnp.bfloat16

D_MODEL = 1024
DEPTH = 2
GRID_W = 64
HEAD_DIM = 64
GROUP_W = 256
N_HEADS = 4
KH = 8
KW = 16
HY_ORDER = 2
HY_EMB = 33
HY_BANDS = 16
ML_CHUNK = 128
PLE_DIM = 256
D_FF = 2816
EPS = 1e-6
QK_SCALE = HEAD_DIM ** -0.5
NEG = -1e30
N_GATES = 4 * N_HEADS

LANES = 128
VMEM_LIMIT = 56 * 1024 * 1024
TOKEN_TILE = 512
FF_CHUNK = 256
NA_ROWS_PER_STEP = 8
HY_CH_BLOCK = 16
FN_CH_BLOCK = 32
FILT_TILE = 1024


def _cparams(*sem):
    return pltpu.CompilerParams(dimension_semantics=sem, vmem_limit_bytes=VMEM_LIMIT)


def _dot(a, b):
    return jnp.dot(a, b, preferred_element_type=F32)


def _dot_nt(a, b):
    return lax.dot_general(a, b, (((1,), (1,)), ((), ())), preferred_element_type=F32)


def _split3(x):
    hi = x.astype(BF16)
    r = x - hi.astype(F32)
    mid = r.astype(BF16)
    lo = (r - mid.astype(F32)).astype(BF16)
    return hi, mid, lo


def _split2(x):
    hi = x.astype(BF16)
    return hi, (x - hi.astype(F32)).astype(BF16)


def _rms(x, g):
    return x * lax.rsqrt(jnp.mean(x * x, axis=-1, keepdims=True) + EPS) * g


def _sigmoid(x):
    return 1.0 / (1.0 + jnp.exp(-x))


def _full_spec(a):
    nd = a.ndim
    return pl.BlockSpec(a.shape, lambda *_: (0,) * nd, pipeline_mode=pl.Buffered(1))


def _inproj_kernel(x_ref, g_ref, wa_ref, wb_ref, wc_ref, wd_ref, wg_ref, wgt_ref, wkt_ref, fc_ref, fs_ref,
                   gbc_ref, gbr_ref,
                   qa_ref, ka_ref, va_ref, zr_ref, zi_ref, uc_ref, qd_ref, vd_ref, od_ref, gcol_ref, grow_ref,
                   kdt_ref):
    xn = _rms(x_ref[0], g_ref[...]).astype(BF16)
    a = _dot(xn, wa_ref[...])
    qa_ref[0] = (a[:, :GROUP_W] * QK_SCALE).astype(BF16)
    ka_ref[0] = a[:, GROUP_W:2 * GROUP_W].astype(BF16)
    va_ref[0] = a[:, 2 * GROUP_W:].astype(BF16)
    ub = _dot(xn, wb_ref[...]).astype(BF16)
    zr_ref[0] = _dot(ub, fc_ref[...])
    zi_ref[0] = _dot(ub, fs_ref[...])
    uc_ref[0] = _dot(xn, wc_ref[...])
    d = _dot(xn, wd_ref[...])
    qd_ref[0] = d[:, :GROUP_W].astype(BF16)
    vd_ref[0] = d[:, 2 * GROUP_W:3 * GROUP_W].astype(BF16)
    od_ref[0] = d[:, 3 * GROUP_W:]
    gcol_ref[0] = _dot(xn, wg_ref[...]) + gbc_ref[...]
    grow_ref[0] = _dot_nt(wgt_ref[...], xn) + gbr_ref[...]
    kdt_ref[0] = (_dot_nt(wkt_ref[...], xn) * QK_SCALE).astype(BF16)


def _inproj(x, g, w_in, gate_b, fcs):
    bsz, seq, _ = x.shape
    tm = min(TOKEN_TILE, seq)
    wb16 = w_in.astype(BF16)
    o = 0
    wa = wb16[:, o:o + 3 * GROUP_W]; o += 3 * GROUP_W
    wb = wb16[:, o:o + GROUP_W]; o += GROUP_W
    wc = wb16[:, o:o + 3 * GROUP_W]; o += 3 * GROUP_W
    wd = wb16[:, o:o + 4 * GROUP_W]; o += 4 * GROUP_W
    wgs = wb16[:, o:o + N_GATES]
    wg = jnp.pad(wgs, ((0, 0), (0, LANES - N_GATES)))
    wgt = wgs.T
    wkt = wd[:, GROUP_W:2 * GROUP_W].T
    gb = gate_b.astype(F32).reshape(N_GATES)
    gbc = jnp.pad(gb, (0, LANES - N_GATES)).reshape(1, LANES)
    gbr = gb.reshape(N_GATES, 1)
    fc, fs = fcs
    weights = (g.astype(F32).reshape(1, D_MODEL), wa, wb, wc, wd, wg, wgt, wkt, fc, fs, gbc, gbr)

    def tok(width, dtype):
        return jax.ShapeDtypeStruct((bsz, seq, width), dtype), pl.BlockSpec((1, tm, width), lambda b, t: (b, t, 0))

    def chan(height, dtype):
        return jax.ShapeDtypeStruct((bsz, height, seq), dtype), pl.BlockSpec((1, height, tm), lambda b, t: (b, 0, t))

    outs = [tok(GROUP_W, BF16), tok(GROUP_W, BF16), tok(GROUP_W, BF16),
            tok(GROUP_W, F32), tok(GROUP_W, F32), tok(3 * GROUP_W, F32),
            tok(GROUP_W, BF16), tok(GROUP_W, BF16), tok(GROUP_W, F32),
            tok(LANES, F32), chan(N_GATES, F32), chan(GROUP_W, BF16)]
    return pl.pallas_call(
        _inproj_kernel,
        grid=(bsz, seq // tm),
        in_specs=[pl.BlockSpec((1, tm, D_MODEL), lambda b, t: (b, t, 0))] + [_full_spec(w) for w in weights],
        out_specs=[s for _, s in outs],
        out_shape=[s for s, _ in outs],
        compiler_params=_cparams("parallel", "parallel"),
        name="inproj",
    )(x, *weights)


def _na_kernel(q_ref, k_ref, v_ref, bias_ref, o_ref, *, rows, rows_per_step):
    i = pl.program_id(1)
    lane_head = lax.broadcasted_iota(jnp.int32, (GRID_W, GROUP_W), 1) // HEAD_DIM
    masks = [lane_head == h for h in range(N_HEADS)]
    for j in range(rows_per_step):
        r = i * rows_per_step + j
        kr0 = jnp.clip(r - KH // 2, 0, rows - KH)
        case = r - kr0
        ks = pl.multiple_of(kr0 * GRID_W, GRID_W)
        q = q_ref[0, j * GRID_W:(j + 1) * GRID_W, :]
        qs = jnp.concatenate([jnp.where(m, q, jnp.zeros_like(q)) for m in masks], axis=0)
        kw = k_ref[0, pl.ds(ks, KH * GRID_W), :]
        vw = v_ref[0, pl.ds(ks, KH * GRID_W), :]
        s = _dot_nt(qs, kw) + bias_ref[case]
        p = jnp.exp(s - jnp.max(s, axis=-1, keepdims=True))
        den = jnp.sum(p, axis=-1, keepdims=True)
        o = _dot(p.astype(BF16), vw) / den
        out = jnp.zeros((GRID_W, GROUP_W), F32)
        for h in range(N_HEADS):
            out = out + jnp.where(masks[h], o[h * GRID_W:(h + 1) * GRID_W, :], 0.0)
        o_ref[0, j * GRID_W:(j + 1) * GRID_W, :] = out


def _na_bias_table(rpb):
    c = np.arange(GRID_W)
    kc = np.arange(GRID_W)
    kc0 = np.clip(c - KW // 2, 0, GRID_W - KW)
    valid = (kc[None, :] >= kc0[:, None]) & (kc[None, :] < kc0[:, None] + KW)
    dc = np.clip(kc[None, :] - c[:, None] + KW - 1, 0, 2 * KW - 2)
    case = np.arange(KH)
    j = np.arange(KH)
    dr = j[None, :] - case[:, None] + KH - 1
    rp = rpb.astype(F32)
    tab = rp[:, dr[:, None, :, None], dc[None, :, None, :]]
    tab = jnp.where(jnp.asarray(valid)[None, None, :, None, :], tab, NEG)
    tab = tab.transpose(1, 0, 2, 3, 4).reshape(KH, N_HEADS * GRID_W, KH * GRID_W)
    return tab


def _na(qa, ka, va, bias):
    bsz, seq, _ = qa.shape
    rows = seq // GRID_W
    assert rows >= KH and rows % NA_ROWS_PER_STEP == 0
    rb = NA_ROWS_PER_STEP
    return pl.pallas_call(
        functools.partial(_na_kernel, rows=rows, rows_per_step=rb),
        grid=(bsz, rows // rb),
        in_specs=[pl.BlockSpec((1, rb * GRID_W, GROUP_W), lambda b, i: (b, i, 0)),
                  pl.BlockSpec((1, seq, GROUP_W), lambda b, i: (b, 0, 0)),
                  pl.BlockSpec((1, seq, GROUP_W), lambda b, i: (b, 0, 0)),
                  _full_spec(bias)],
        out_specs=pl.BlockSpec((1, rb * GRID_W, GROUP_W), lambda b, i: (b, i, 0)),
        out_shape=jax.ShapeDtypeStruct((bsz, seq, GROUP_W), F32),
        compiler_params=_cparams("parallel", "arbitrary"),
        name="nbr_attn",
    )(qa, ka, va, bias)


def _cs(num, den):
    ang = 2.0 * np.pi * (np.asarray(num, np.float64) % den) / den
    return np.cos(ang), np.sin(ang)


def _hilo(m):
    m32 = jnp.asarray(m, F32)
    hi = m32.astype(BF16)
    return hi, (m32 - hi.astype(F32)).astype(BF16)


def _fnet_tables(seq):
    n1f = seq // LANES
    c, s = _cs(np.outer(np.arange(HEAD_DIM), np.arange(HEAD_DIM)), HEAD_DIM)
    norm = 1.0 / math.sqrt(HEAD_DIM * seq)
    eye = np.eye(GROUP_W // HEAD_DIM)
    fc = np.kron(eye, c) * norm
    fs = np.kron(eye, -s) * norm
    c1, s1 = _cs(np.outer(np.arange(n1f), np.arange(n1f)), n1f)
    m1 = np.block([[c1, -s1], [s1, c1]])
    ct, st = _cs(np.outer(np.arange(LANES), np.arange(n1f)), seq)
    t1 = np.concatenate([ct, ct], axis=1)
    t2 = np.concatenate([st, -st], axis=1)
    c2, s2 = _cs(np.outer(np.arange(LANES), np.arange(LANES)), LANES)
    g2 = np.concatenate([c2, s2], axis=0)
    return dict(fc=jnp.asarray(fc, F32).astype(BF16), fs=jnp.asarray(fs, F32).astype(BF16),
                m1=jnp.asarray(m1, F32).astype(BF16), t1=jnp.asarray(t1, F32), t2=jnp.asarray(t2, F32),
                g2=jnp.asarray(g2, F32).astype(BF16))


def _hyena_tables(seq):
    n = 2 * seq
    n1 = n // LANES
    c1, s1 = _cs(np.outer(np.arange(n1), np.arange(n1)), n1)
    f1 = np.concatenate([c1, -s1], axis=1)
    ct, st = _cs(np.outer(np.arange(LANES), np.arange(n1)), n)
    c2, s2 = _cs(np.outer(np.arange(LANES), np.arange(LANES)), LANES)
    g2 = np.block([[c2, -s2], [s2, c2]])
    gi2 = np.block([[c2, s2], [-s2, c2]])
    gi1 = np.concatenate([c1, -s1], axis=0) / n
    gi1[:, n1 // 2:] = 0.0
    return dict(f1=_hilo(f1), tc=jnp.asarray(ct, F32), ts=jnp.asarray(st, F32),
                tct=jnp.asarray(ct.T, F32), tst=jnp.asarray(st.T, F32),
                g2=_hilo(g2), gi2=_hilo(gi2), gi1=_hilo(gi1))


def _mm(a, tab, precise):
    hi, lo = tab
    if not precise:
        return _dot(a.astype(BF16), hi)
    a_hi, a_lo = _split2(a)
    return _dot(a_hi, hi) + (_dot(a_lo, hi) + _dot(a_hi, lo))


def _fnet_kernel(z_ref, m1_ref, t1_ref, t2_ref, g2_ref, o_ref):
    cb, n2, w = z_ref.shape[1:]
    n1f = w // 2
    a = _dot(z_ref[0].reshape(cb * n2, w).astype(BF16), m1_ref[...])
    a = a.reshape(cb, n2, w)
    sw = pltpu.roll(a.reshape(cb * n2, w), n1f, 1).reshape(cb, n2, w)
    a = a * t1_ref[...] + sw * t2_ref[...]
    at = jnp.swapaxes(a, 1, 2)
    op = jnp.concatenate([at[:, :n1f, :], at[:, n1f:, :]], axis=-1)
    x = _dot(op.reshape(cb * n1f, 2 * n2).astype(BF16), g2_ref[...])
    o_ref[0] = x.reshape(cb, n1f, n2)


def _fnet(zr, zi, tabs):
    bsz, seq, ch = zr.shape
    n1f = seq // LANES
    z = jnp.concatenate([zr.reshape(bsz, n1f, LANES, ch), zi.reshape(bsz, n1f, LANES, ch)], axis=1)
    z = z.transpose(0, 3, 2, 1)
    cb = FN_CH_BLOCK
    consts = (tabs["m1"], tabs["t1"], tabs["t2"], tabs["g2"])
    f = pl.pallas_call(
        _fnet_kernel,
        grid=(bsz, ch // cb),
        in_specs=[pl.BlockSpec((1, cb, LANES, 2 * n1f), lambda b, c: (b, c, 0, 0))] + [_full_spec(t) for t in consts],
        out_specs=pl.BlockSpec((1, cb, n1f, LANES), lambda b, c: (b, c, 0, 0)),
        out_shape=jax.ShapeDtypeStruct((bsz, ch, n1f, LANES), F32),
        compiler_params=_cparams("parallel", "parallel"),
        name="fourier_mix",
    )(z, *consts)
    return f.transpose(0, 3, 2, 1).reshape(bsz, seq, ch)


def _filter_mlp_kernel(feat_ref, t_ref, w1_ref, b1_ref, f0_ref, w2_ref, b2_ref, f1_ref, w3_ref, dec_ref, o_ref):
    hp = lax.Precision.HIGHEST
    h = jnp.sin(f0_ref[...] * (jnp.dot(feat_ref[...], w1_ref[...], precision=hp, preferred_element_type=F32)
                               + b1_ref[...]))
    h = jnp.sin(f1_ref[...] * (jnp.dot(h, w2_ref[...], precision=hp, preferred_element_type=F32) + b2_ref[...]))
    h = jnp.dot(h, w3_ref[...], precision=hp, preferred_element_type=F32)
    o_ref[...] = h * jnp.exp(-t_ref[...] * dec_ref[...])


def _hyena_filter_taps(seq, w1, b1, freq, w2, b2, w3, decay):
    s = jnp.arange(seq, dtype=F32)
    t = s / max(seq - 1, 1)
    ang = (2.0 * math.pi / seq) * s
    bands = jnp.linspace(1e-4, HY_BANDS - 1, HY_BANDS, dtype=F32)
    fb = ang[:, None] * bands[None, :]
    feats = jnp.concatenate([t[:, None], jnp.cos(fb), -jnp.sin(fb)], axis=-1)
    filt = w1.shape[1]
    pad_c = LANES - filt
    feats = jnp.pad(feats, ((0, 0), (0, LANES - HY_EMB)))
    w1p = jnp.pad(w1.astype(F32), ((0, LANES - HY_EMB), (0, pad_c)))
    w2p = jnp.pad(w2.astype(F32), ((0, pad_c), (0, pad_c)))
    w3p = jnp.pad(w3.astype(F32), ((0, pad_c), (0, 0)))
    row = lambda v: jnp.pad(v.astype(F32), (0, pad_c)).reshape(1, LANES)
    n_out = w3.shape[1]
    tl = min(FILT_TILE, seq)
    args = (feats, t.reshape(seq, 1), w1p, row(b1), row(freq[0]), w2p, row(b2), row(freq[1]), w3p,
            decay.astype(F32).reshape(1, n_out))
    specs = [pl.BlockSpec((tl, LANES), lambda i: (i, 0)), pl.BlockSpec((tl, 1), lambda i: (i, 0))]
    specs += [_full_spec(a) for a in args[2:]]
    return pl.pallas_call(
        _filter_mlp_kernel,
        grid=(seq // tl,),
        in_specs=specs,
        out_specs=pl.BlockSpec((tl, n_out), lambda i: (i, 0)),
        out_shape=jax.ShapeDtypeStruct((seq, n_out), F32),
        compiler_params=_cparams("parallel"),
        name="hyena_filter_mlp",
    )(*args)


def _fft_fwd(u, f1, tc, ts, g2, precise):
    cb, n2, n1 = u.shape
    a = _mm(u.reshape(cb * n2, n1), f1, precise)
    ar = a[:, :n1].reshape(cb, n2, n1)
    ai = a[:, n1:].reshape(cb, n2, n1)
    br = ar * tc + ai * ts
    bi = ai * tc - ar * ts
    op = jnp.concatenate([jnp.swapaxes(br, 1, 2), jnp.swapaxes(bi, 1, 2)], axis=-1)
    return _mm(op.reshape(cb * n1, 2 * n2), g2, precise)


def _fft_inv(y, gi2, tct, tst, gi1, cb, n1, precise):
    n2 = y.shape[1] // 2
    d = _mm(y, gi2, precise)
    dr = d[:, :n2].reshape(cb, n1, n2)
    di = d[:, n2:].reshape(cb, n1, n2)
    er = dr * tct - di * tst
    ei = dr * tst + di * tct
    op = jnp.concatenate([jnp.swapaxes(er, 1, 2), jnp.swapaxes(ei, 1, 2)], axis=-1)
    return _mm(op.reshape(cb * n2, 2 * n1), gi1, precise).reshape(cb, n2, n1)


def _spectrum_kernel(h_ref, f1h, f1l, tc_ref, ts_ref, g2h, g2l, o_ref):
    x = _fft_fwd(h_ref[...], (f1h[...], f1l[...]), tc_ref[...], ts_ref[...], (g2h[...], g2l[...]), True)
    o_ref[...] = x.reshape(o_ref.shape)


def _hyena_spectra(taps, seq, tabs):
    ch = taps.shape[-1]
    n1 = 2 * seq // LANES
    fwd = taps[:, :, 0]
    bwd = taps[1:, :, 1][::-1]
    full = jnp.concatenate([fwd, jnp.zeros((1, HY_ORDER, ch), F32), bwd], axis=0)
    hcm = full.reshape(n1, LANES, HY_ORDER * ch).transpose(2, 1, 0)
    cb = HY_CH_BLOCK
    consts = (*tabs["f1"], tabs["tc"], tabs["ts"], *tabs["g2"])
    return pl.pallas_call(
        _spectrum_kernel,
        grid=(HY_ORDER * ch // cb,),
        in_specs=[pl.BlockSpec((cb, LANES, n1), lambda c: (c, 0, 0))] + [_full_spec(t) for t in consts],
        out_specs=pl.BlockSpec((cb, n1, 2 * LANES), lambda c: (c, 0, 0)),
        out_shape=jax.ShapeDtypeStruct((HY_ORDER * ch, n1, 2 * LANES), F32),
        compiler_params=_cparams("parallel"),
        name="hyena_filter_spectrum",
    )(hcm, *consts)


def _hyena_kernel(uv_ref, u1_ref, u2_ref, cv_ref, c1_ref, c2_ref, skip_ref, k0_ref, k1_ref,
                  f1_ref, tc_ref, ts_ref, g2_ref, gi2_ref, tct_ref, tst_ref, gi1_ref, o_ref):
    cb, n2, n1 = uv_ref.shape[1:]
    rows = cb * n2
    row_n2 = lax.broadcasted_iota(jnp.int32, (rows, n1), 0) % n2
    first = row_n2 == 0
    last = row_n2 == n2 - 1
    valid = lax.broadcasted_iota(jnp.int32, (cb, n2, n1), 2) < n1 // 2

    def short_conv(u_ref, c_ref):
        u = u_ref[0]
        u2 = u.reshape(rows, n1)
        prev = jnp.where(first, pltpu.roll(pltpu.roll(u2, rows - (n2 - 1), 0), 1, 1), pltpu.roll(u2, 1, 0))
        nxt = jnp.where(last, pltpu.roll(pltpu.roll(u2, n2 - 1, 0), n1 - 1, 1), pltpu.roll(u2, rows - 1, 0))
        c = c_ref[...]
        return (prev.reshape(cb, n2, n1) * c[:, 0:1, :] + u * c[:, 1:2, :] + nxt.reshape(cb, n2, n1) * c[:, 2:3, :]
                + c[:, 3:4, :])

    tabs_f = ((f1_ref[...], None), tc_ref[...], ts_ref[...], (g2_ref[...], None))
    tabs_i = ((gi2_ref[...], None), tct_ref[...], tst_ref[...], (gi1_ref[...], None))

    def fftconv(u, k_ref):
        x = _fft_fwd(u, *tabs_f, False)
        kf = k_ref[...].reshape(cb * n1, 2 * n2)
        xr, xi = x[:, :n2], x[:, n2:]
        kr, ki = kf[:, :n2], kf[:, n2:]
        y = jnp.concatenate([xr * kr - xi * ki, xr * ki + xi * kr], axis=-1)
        return _fft_inv(y, *tabs_i, cb, n1, False)

    skip = skip_ref[...]
    v = jnp.where(valid, short_conv(uv_ref, cv_ref), 0.0)
    y1 = short_conv(u1_ref, c1_ref) * (fftconv(v, k0_ref) + v * skip[:, 0:1, :])
    y1 = jnp.where(valid, y1, 0.0)
    o_ref[0] = short_conv(u2_ref, c2_ref) * (fftconv(y1, k1_ref) + y1 * skip[:, 1:2, :])


def _hyena(uc, conv_w, conv_b, skip, spectra, tabs):
    bsz, seq, _ = uc.shape
    ch = GROUP_W
    n1 = 2 * seq // LANES
    u = uc.reshape(bsz, n1 // 2, LANES, 3 * ch).transpose(0, 3, 2, 1)
    u = jnp.pad(u, ((0, 0), (0, 0), (0, 0), (0, n1 // 2)))
    cw = jnp.concatenate([conv_w.astype(F32), conv_b.astype(F32)[None]], axis=0).T
    cw = jnp.broadcast_to(cw[:, :, None], (3 * ch, 4, n1))
    sk = jnp.broadcast_to(skip.astype(F32).T[:, :, None], (ch, HY_ORDER, n1))
    cb = HY_CH_BLOCK
    nblk = ch // cb
    consts = (tabs["f1"][0], tabs["tc"], tabs["ts"], tabs["g2"][0], tabs["gi2"][0], tabs["tct"], tabs["tst"],
              tabs["gi1"][0])
    u_spec = lambda g: pl.BlockSpec((1, cb, LANES, n1), lambda c, b, g=g: (b, c + g * nblk, 0, 0))
    c_spec = lambda g: pl.BlockSpec((cb, 4, n1), lambda c, b, g=g: (c + g * nblk, 0, 0))
    k_spec = lambda o: pl.BlockSpec((cb, n1, 2 * LANES), lambda c, b, o=o: (c + o * nblk, 0, 0))
    y = pl.pallas_call(
        _hyena_kernel,
        grid=(nblk, bsz),
        in_specs=[u_spec(0), u_spec(1), u_spec(2), c_spec(0), c_spec(1), c_spec(2),
                  pl.BlockSpec((cb, HY_ORDER, n1), lambda c, b: (c, 0, 0)), k_spec(0), k_spec(1)]
                 + [_full_spec(t) for t in consts],
        out_specs=pl.BlockSpec((1, cb, LANES, n1), lambda c, b: (b, c, 0, 0)),
        out_shape=jax.ShapeDtypeStruct((bsz, ch, LANES, n1), F32),
        compiler_params=_cparams("parallel", "parallel"),
        name="hyena",
    )(u, u, u, cw, cw, cw, sk, spectra, spectra, *consts)
    return y[..., :n1 // 2].transpose(0, 3, 2, 1).reshape(bsz, seq, ch)


def _mlstm_direction(d, q_ref, v_ref, kt_ref, gc_ref, gr_ref, tl_ref, tu_ref, out_ref, s_scr, m_scr):
    c = ML_CHUNK
    q = q_ref[0]
    v = v_ref[0]
    kt = kt_ref[0]
    gcol = gc_ref[0]
    grow = gr_ref[0]
    lf_col = jax.nn.log_sigmoid(gcol)
    lf_row = jax.nn.log_sigmoid(grow)
    tri_col = tl_ref[...] if d == 0 else tu_ref[...]
    tri_row = tu_ref[...] if d == 0 else tl_ref[...]
    b_col = sum(_dot(tri_col, part) for part in _split3(lf_col))
    b_row = sum(_dot(part, tri_row) for part in _split3(lf_row))
    jj = lax.broadcasted_iota(jnp.int32, (c, c), 0)
    ss = lax.broadcasted_iota(jnp.int32, (c, c), 1)
    causal = (ss <= jj) if d == 0 else (ss >= jj)
    lane_head = lax.broadcasted_iota(jnp.int32, (c, GROUP_W), 1) // HEAD_DIM
    i_idx = 2 * N_HEADS * d
    f_idx = i_idx + N_HEADS

    dmats, wis, einvs, w_rows, d_olds, m_news = [], [], [], [], [], []
    for h in range(N_HEADS):
        bc = b_col[:, f_idx + h:f_idx + h + 1]
        br = b_row[f_idx + h:f_idx + h + 1, :]
        li = grow[i_idx + h:i_idx + h + 1, :]
        g = jnp.sum(lf_row[f_idx + h:f_idx + h + 1, :], axis=-1, keepdims=True)
        m0 = m_scr[pl.ds(d * N_HEADS + h, 1), :][:, :1]
        dm = jnp.where(causal, bc - br + li, NEG)
        inter = bc + m0
        mt = jnp.maximum(inter, jnp.max(dm, axis=-1, keepdims=True))
        dmats.append(jnp.exp(dm - mt))
        wis.append(jnp.exp(inter - mt))
        einvs.append(jnp.exp(-mt))
        a = g - br + li
        m_loc = jnp.max(a, axis=-1, keepdims=True)
        m_new = jnp.maximum(g + m0, m_loc)
        w_rows.append(jnp.exp(a - m_loc) * jnp.exp(m_loc - m_new))
        d_olds.append(jnp.exp(g + m0 - m_new))
        m_news.append(m_new)

    q_stack = jnp.concatenate([jnp.where(lane_head == h, q, jnp.zeros_like(q)) for h in range(N_HEADS)], axis=0)
    p = jnp.concatenate(dmats, axis=0) * _dot(q_stack, kt)
    p_sum = jnp.sum(p, axis=-1, keepdims=True)
    pv = _dot(p.astype(BF16), v)
    s_old = s_scr[d]
    qs = _dot(q, s_old.astype(BF16))
    out = jnp.zeros((c, GROUP_W), F32)
    for h in range(N_HEADS):
        num = wis[h] * qs[:, :GROUP_W] + pv[h * c:(h + 1) * c, :]
        den = wis[h] * qs[:, GROUP_W + h:GROUP_W + h + 1] + p_sum[h * c:(h + 1) * c, :]
        hh = num / jnp.maximum(jnp.abs(den), einvs[h])
        out = out + jnp.where(lane_head == h, hh, 0.0)
    out_ref[0] = out

    w_full = jnp.concatenate([jnp.broadcast_to(w, (HEAD_DIM, c)) for w in w_rows], axis=0)
    d_full = jnp.concatenate([jnp.broadcast_to(x, (HEAD_DIM, 1)) for x in d_olds], axis=0)
    ktw = kt.astype(F32) * w_full
    s_loc = _dot(ktw.astype(BF16), v)
    n_loc = jnp.sum(ktw, axis=-1, keepdims=True)
    rh = lax.broadcasted_iota(jnp.int32, (GROUP_W, GROUP_W), 0) // HEAD_DIM
    ch = lax.broadcasted_iota(jnp.int32, (GROUP_W, GROUP_W), 1) // HEAD_DIM
    rh_n = lax.broadcasted_iota(jnp.int32, (GROUP_W, LANES), 0) // HEAD_DIM
    col_n = lax.broadcasted_iota(jnp.int32, (GROUP_W, LANES), 1)
    s_scr[d, :, :GROUP_W] = d_full * s_old[:, :GROUP_W] + jnp.where(rh == ch, s_loc, 0.0)
    s_scr[d, :, GROUP_W:] = d_full * s_old[:, GROUP_W:] + jnp.where(rh_n == col_n, n_loc, 0.0)
    for h in range(N_HEADS):
        m_scr[pl.ds(d * N_HEADS + h, 1), :] = jnp.broadcast_to(m_news[h], (1, LANES))


def _mlstm_kernel(qf, vf, ktf, gcf, grf, qb, vb, ktb, gcb, grb, tl_ref, tu_ref, hf_ref, hb_ref, s_scr, m_scr):
    @pl.when(pl.program_id(1) == 0)
    def _():
        s_scr[...] = jnp.zeros_like(s_scr)
        m_scr[...] = jnp.zeros_like(m_scr)

    _mlstm_direction(0, qf, vf, ktf, gcf, grf, tl_ref, tu_ref, hf_ref, s_scr, m_scr)
    _mlstm_direction(1, qb, vb, ktb, gcb, grb, tl_ref, tu_ref, hb_ref, s_scr, m_scr)


def _mlstm(qd, vd, kdt, gcol, grow):
    bsz, seq, _ = qd.shape
    c = ML_CHUNK
    nc = seq // c
    tl = jnp.asarray(np.tril(np.ones((c, c))), BF16)
    tu = jnp.asarray(np.triu(np.ones((c, c))), BF16)
    fwd = lambda b, i: (b, i, 0)
    bwd = lambda b, i: (b, nc - 1 - i, 0)
    fwd_t = lambda b, i: (b, 0, i)
    bwd_t = lambda b, i: (b, 0, nc - 1 - i)

    def specs(tok, chan):
        return [pl.BlockSpec((1, c, GROUP_W), tok), pl.BlockSpec((1, c, GROUP_W), tok),
                pl.BlockSpec((1, GROUP_W, c), chan), pl.BlockSpec((1, c, LANES), tok),
                pl.BlockSpec((1, N_GATES, c), chan)]

    args = (qd, vd, kdt, gcol, grow)
    return pl.pallas_call(
        _mlstm_kernel,
        grid=(bsz, nc),
        in_specs=specs(fwd, fwd_t) + specs(bwd, bwd_t) + [_full_spec(tl), _full_spec(tu)],
        out_specs=[pl.BlockSpec((1, c, GROUP_W), fwd), pl.BlockSpec((1, c, GROUP_W), bwd)],
        out_shape=[jax.ShapeDtypeStruct((bsz, seq, GROUP_W), F32)] * 2,
        scratch_shapes=[pltpu.VMEM((2, GROUP_W, GROUP_W + LANES), F32), pltpu.VMEM((2 * N_HEADS, LANES), F32)],
        compiler_params=_cparams("parallel", "arbitrary"),
        name="mlstm",
    )(*args, *args, tl, tu)


def _post_kernel(x_ref, ya_ref, f_ref, yc_ref, hf_ref, hb_ref, od_ref, p_ref,
                 wfn_ref, onorm_ref, gsum_ref, gbc_ref, wout_ref, nffn_ref, wgate_ref, wup_ref, wdown_ref,
                 pnorm_ref, wpg_ref, wpp_ref, fnorm_ref, o_ref, *, final):
    yb = _dot(f_ref[0].astype(BF16), wfn_ref[...])
    yd = _sigmoid(od_ref[0]) * (hf_ref[0] + hb_ref[0])
    y = jnp.concatenate([ya_ref[0], yb, yc_ref[0], yd], axis=-1)
    sq_hi, sq_lo = _split2(y * y)
    ss = _dot(sq_hi, gsum_ref[...]) + _dot(sq_lo, gsum_ref[...])
    r_hi, r_lo = _split2(lax.rsqrt(ss * (1.0 / HEAD_DIM) + EPS))
    rb = _dot(r_hi, gbc_ref[...]) + _dot(r_lo, gbc_ref[...])
    x = x_ref[0] + _dot((y * rb * onorm_ref[...]).astype(BF16), wout_ref[...])
    hn = _rms(x, nffn_ref[...]).astype(BF16)
    acc = jnp.zeros_like(x)
    for c in range(D_FF // FF_CHUNK):
        sl = slice(c * FF_CHUNK, (c + 1) * FF_CHUNK)
        g = _dot(hn, wgate_ref[:, sl])
        u = _dot(hn, wup_ref[:, sl])
        acc = acc + _dot((g * _sigmoid(g) * u).astype(BF16), wdown_ref[sl, :])
    x = x + acc
    gate = _sigmoid(_dot(_rms(x, pnorm_ref[...]).astype(BF16), wpg_ref[...]))
    x = x + gate * _dot(p_ref[0].astype(BF16), wpp_ref[...])
    if final:
        x = _rms(x, fnorm_ref[...])
    o_ref[0] = x


def _post(x, ya, f, yc, hf, hb, od, p, lw, final_norm, final):
    bsz, seq, _ = x.shape
    tm = min(TOKEN_TILE, seq)
    n_groups = D_MODEL // HEAD_DIM
    ind = np.zeros((D_MODEL, LANES), np.float32)
    ind[np.arange(D_MODEL), np.arange(D_MODEL) // HEAD_DIM] = 1.0
    gsum = jnp.asarray(ind, BF16)
    gbc = jnp.asarray(ind.T, BF16)
    del n_groups
    fw = lw["fnet_w"].astype(F32)
    wfn = jax.scipy.linalg.block_diag(*[fw[g] for g in range(fw.shape[0])]).astype(BF16)
    row = lambda v: v.astype(F32).reshape(1, D_MODEL)
    weights = (wfn, row(lw["out_norm"]), gsum, gbc, lw["w_out"].astype(BF16), row(lw["norm_ffn"]),
               lw["w_gate"].astype(BF16), lw["w_up"].astype(BF16), lw["w_down"].astype(BF16),
               row(lw["ple_norm"]), lw["w_ple_gate"].astype(BF16), lw["w_ple_proj"].astype(BF16), row(final_norm))
    tok = lambda width: pl.BlockSpec((1, tm, width), lambda b, t: (b, t, 0))
    return pl.pallas_call(
        functools.partial(_post_kernel, final=final),
        grid=(bsz, seq // tm),
        in_specs=[tok(D_MODEL)] + [tok(GROUP_W)] * 6 + [tok(PLE_DIM)] + [_full_spec(w) for w in weights],
        out_specs=tok(D_MODEL),
        out_shape=jax.ShapeDtypeStruct((bsz, seq, D_MODEL), F32),
        compiler_params=_cparams("parallel", "parallel"),
        name="post",
    )(x, ya, f, yc, hf, hb, od, p, *weights)


def _layer_consts(lw, seq, hy_tabs):
    taps = _hyena_filter_taps(seq, lw["hy_w1"], lw["hy_b1"], lw["hy_freq"], lw["hy_w2"], lw["hy_b2"], lw["hy_w3"],
                              lw["hy_decay"])
    taps = taps.reshape(seq, HY_ORDER, 2, GROUP_W)
    return dict(spectra=_hyena_spectra(taps, seq, hy_tabs), na_bias=_na_bias_table(lw["attn_rpb"]))


def _trunk(x, p, layers, consts, final_norm, fn_tabs, hy_tabs):
    for i, (lw, lc) in enumerate(zip(layers, consts)):
        (qa, ka, va, zr, zi, uc, qd, vd, od, gcol, grow, kdt) = _inproj(
            x, lw["norm_mix"], lw["w_in"], lw["ml_gate_b"], (fn_tabs["fc"], fn_tabs["fs"]))
        ya = _na(qa, ka, va, lc["na_bias"])
        f = _fnet(zr, zi, fn_tabs)
        yc = _hyena(uc, lw["hy_conv_w"], lw["hy_conv_b"], lw["hy_skip"], lc["spectra"], hy_tabs)
        hf, hb = _mlstm(qd, vd, kdt, gcol, grow)
        x = _post(x, ya, f, yc, hf, hb, od, p[i], lw, final_norm, final=(i == len(layers) - 1))
    return x


_LAYER_KEYS = ("norm_mix", "w_in", "attn_rpb", "fnet_w", "hy_conv_w", "hy_conv_b", "hy_w1", "hy_b1", "hy_freq",
               "hy_w2", "hy_b2", "hy_w3", "hy_decay", "hy_skip", "ml_gate_b", "out_norm", "w_out", "norm_ffn",
               "w_gate", "w_up", "w_down", "ple_norm", "w_ple_gate", "w_ple_proj")


def kernel(x_prompt, x_sample, p_prompt, p_sample, norm_mix, w_in, attn_rpb, fnet_w, hy_conv_w, hy_conv_b, hy_w1,
           hy_b1, hy_freq, hy_w2, hy_b2, hy_w3, hy_decay, hy_skip, ml_gate_b, out_norm, w_out, norm_ffn, w_gate,
           w_up, w_down, ple_norm, w_ple_gate, w_ple_proj, final_norm):
    stacked = dict(zip(_LAYER_KEYS, (norm_mix, w_in, attn_rpb, fnet_w, hy_conv_w, hy_conv_b, hy_w1, hy_b1, hy_freq,
                                     hy_w2, hy_b2, hy_w3, hy_decay, hy_skip, ml_gate_b, out_norm, w_out, norm_ffn,
                                     w_gate, w_up, w_down, ple_norm, w_ple_gate, w_ple_proj)))
    depth = norm_mix.shape[0]
    layers = [{k: v[i] for k, v in stacked.items()} for i in range(depth)]
    outs = []
    cache = {}
    for x, p in ((x_prompt, p_prompt), (x_sample, p_sample)):
        seq = x.shape[1]
        if seq not in cache:
            fn_tabs = _fnet_tables(seq)
            hy_tabs = _hyena_tables(seq)
            cache[seq] = (fn_tabs, hy_tabs, [_layer_consts(lw, seq, hy_tabs) for lw in layers])
        fn_tabs, hy_tabs, consts = cache[seq]
        outs.append(_trunk(x, p, layers, consts, final_norm, fn_tabs, hy_tabs))
    return tuple(outs)
```

```python
import functools
import math

import numpy as np
import jax
import jax.numpy as jnp
from jax import lax
from jax.experimental import pallas as pl
from jax.experimental.pallas import tpu as pltpu

F32, BF16 = jnp.float32, jnp.bfloat16

D_MODEL = 1024
DEPTH = 2
GRID_W = 64
HEAD_DIM = 64
GROUP_W = 256
N_HEADS = 4
KH = 8
KW = 16
HY_ORDER = 2
HY_EMB = 33
HY_BANDS = 16
ML_CHUNK = 128
PLE_DIM = 256
D_FF = 2816
EPS = 1e-6
QK_SCALE = HEAD_DIM ** -0.5
NEG = -1e30
N_GATES = 4 * N_HEADS

LANES = 128
VMEM_LIMIT = 56 * 1024 * 1024
TOKEN_TILE = 512
FF_CHUNK = 256
NA_ROWS_PER_STEP = 8
HY_CH_BLOCK = 16
FN_CH_BLOCK = 32
FILT_TILE = 1024
ML_CHUNKS_PER_STEP = 4


def _cparams(*sem):
    return pltpu.CompilerParams(dimension_semantics=sem, vmem_limit_bytes=VMEM_LIMIT)


def _dot(a, b):
    return jnp.dot(a, b, preferred_element_type=F32)


def _dot_nt(a, b):
    return lax.dot_general(a, b, (((1,), (1,)), ((), ())), preferred_element_type=F32)


def _split2(x):
    hi = x.astype(BF16)
    return hi, (x - hi.astype(F32)).astype(BF16)


def _rms(x, g):
    return x * lax.rsqrt(jnp.mean(x * x, axis=-1, keepdims=True) + EPS) * g


def _sigmoid(x):
    return 1.0 / (1.0 + jnp.exp(-x))


def _full_spec(a):
    nd = a.ndim
    return pl.BlockSpec(a.shape, lambda *_: (0,) * nd, pipeline_mode=pl.Buffered(1))


def _inproj_kernel(x_ref, g_ref, wa_ref, wb_ref, wc_ref, wd_ref, wgt_ref, wkt_ref, fc_ref, fs_ref, gbr_ref,
                   qa_ref, ka_ref, va_ref, zr_ref, zi_ref, uc_ref, qd_ref, vd_ref, od_ref, grow_ref, kdt_ref):
    xn = _rms(x_ref[0], g_ref[...]).astype(BF16)
    a = _dot(xn, wa_ref[...])
    qa_ref[0] = (a[:, :GROUP_W] * QK_SCALE).astype(BF16)
    ka_ref[0] = a[:, GROUP_W:2 * GROUP_W].astype(BF16)
    va_ref[0] = a[:, 2 * GROUP_W:].astype(BF16)
    ub = _dot(xn, wb_ref[...]).astype(BF16)
    zr_ref[0] = _dot(ub, fc_ref[...])
    zi_ref[0] = _dot(ub, fs_ref[...])
    uc_ref[0] = _dot(xn, wc_ref[...])
    d = _dot(xn, wd_ref[...])
    qd_ref[0] = d[:, :GROUP_W].astype(BF16)
    vd_ref[0] = d[:, 2 * GROUP_W:3 * GROUP_W].astype(BF16)
    od_ref[0] = d[:, 3 * GROUP_W:]
    grow_ref[0] = _dot_nt(wgt_ref[...], xn) + gbr_ref[...]
    kdt_ref[0] = (_dot_nt(wkt_ref[...], xn) * QK_SCALE).astype(BF16)


def _inproj(x, g, w_in, gate_b, fcs):
    bsz, seq, _ = x.shape
    tm = min(TOKEN_TILE, seq)
    wb16 = w_in.astype(BF16)
    o = 0
    wa = wb16[:, o:o + 3 * GROUP_W]; o += 3 * GROUP_W
    wb = wb16[:, o:o + GROUP_W]; o += GROUP_W
    wc = wb16[:, o:o + 3 * GROUP_W]; o += 3 * GROUP_W
    wd = wb16[:, o:o + 4 * GROUP_W]; o += 4 * GROUP_W
    wgt = wb16[:, o:o + N_GATES].T
    wkt = wd[:, GROUP_W:2 * GROUP_W].T
    gbr = gate_b.astype(F32).reshape(N_GATES, 1)
    fc, fs = fcs
    weights = (g.astype(F32).reshape(1, D_MODEL), wa, wb, wc, wd, wgt, wkt, fc, fs, gbr)

    def tok(width, dtype):
        return jax.ShapeDtypeStruct((bsz, seq, width), dtype), pl.BlockSpec((1, tm, width), lambda b, t: (b, t, 0))

    def chan(height, dtype):
        return jax.ShapeDtypeStruct((bsz, height, seq), dtype), pl.BlockSpec((1, height, tm), lambda b, t: (b, 0, t))

    outs = [tok(GROUP_W, BF16), tok(GROUP_W, BF16), tok(GROUP_W, BF16),
            tok(GROUP_W, F32), tok(GROUP_W, F32), tok(3 * GROUP_W, F32),
            tok(GROUP_W, BF16), tok(GROUP_W, BF16), tok(GROUP_W, F32),
            chan(N_GATES, F32), chan(GROUP_W, BF16)]
    return pl.pallas_call(
        _inproj_kernel,
        grid=(bsz, seq // tm),
        in_specs=[pl.BlockSpec((1, tm, D_MODEL), lambda b, t: (b, t, 0))] + [_full_spec(w) for w in weights],
        out_specs=[s for _, s in outs],
        out_shape=[s for s, _ in outs],
        compiler_params=_cparams("parallel", "parallel"),
        name="inproj",
    )(x, *weights)


def _na_kernel(q_ref, k_ref, v_ref, bias_ref, o_ref, *, rows, rows_per_step):
    i = pl.program_id(1)
    lane_head = lax.broadcasted_iota(jnp.int32, (GRID_W, GROUP_W), 1) // HEAD_DIM
    masks = [lane_head == h for h in range(N_HEADS)]
    for j in range(rows_per_step):
        r = i * rows_per_step + j
        kr0 = jnp.clip(r - KH // 2, 0, rows - KH)
        case = r - kr0
        ks = pl.multiple_of(kr0 * GRID_W, GRID_W)
        q = q_ref[0, j * GRID_W:(j + 1) * GRID_W, :]
        qs = jnp.concatenate([jnp.where(m, q, jnp.zeros_like(q)) for m in masks], axis=0)
        kw = k_ref[0, pl.ds(ks, KH * GRID_W), :]
        vw = v_ref[0, pl.ds(ks, KH * GRID_W), :]
        s = _dot_nt(qs, kw) + bias_ref[case]
        p = jnp.exp(s - jnp.max(s, axis=-1, keepdims=True))
        den = jnp.sum(p, axis=-1, keepdims=True)
        o = _dot(p.astype(BF16), vw) / den
        out = jnp.zeros((GRID_W, GROUP_W), F32)
        for h in range(N_HEADS):
            out = out + jnp.where(masks[h], o[h * GRID_W:(h + 1) * GRID_W, :], 0.0)
        o_ref[0, j * GRID_W:(j + 1) * GRID_W, :] = out


def _na_bias_table(rpb):
    c = np.arange(GRID_W)
    kc = np.arange(GRID_W)
    kc0 = np.clip(c - KW // 2, 0, GRID_W - KW)
    valid = (kc[None, :] >= kc0[:, None]) & (kc[None, :] < kc0[:, None] + KW)
    dc = np.clip(kc[None, :] - c[:, None] + KW - 1, 0, 2 * KW - 2)
    case = np.arange(KH)
    j = np.arange(KH)
    dr = j[None, :] - case[:, None] + KH - 1
    row_sel = np.zeros((KH * KH, 2 * KH - 1), np.float32)
    row_sel[np.arange(KH * KH), dr.reshape(-1)] = 1.0
    col_sel = np.zeros((2 * KW - 1, GRID_W * GRID_W), np.float32)
    col_sel[dc.reshape(-1), np.arange(GRID_W * GRID_W)] = 1.0
    hp = lax.Precision.HIGHEST
    tab = jnp.einsum("rd,hde->hre", jnp.asarray(row_sel), rpb.astype(F32), precision=hp)
    tab = jnp.einsum("hre,ex->hrx", tab, jnp.asarray(col_sel), precision=hp)
    tab = tab.reshape(N_HEADS, KH, KH, GRID_W, GRID_W).transpose(1, 0, 3, 2, 4)
    tab = jnp.where(jnp.asarray(valid)[None, None, :, None, :], tab, NEG)
    return tab.reshape(KH, N_HEADS * GRID_W, KH * GRID_W)


def _na(qa, ka, va, bias):
    bsz, seq, _ = qa.shape
    rows = seq // GRID_W
    assert rows >= KH and rows % NA_ROWS_PER_STEP == 0
    rb = NA_ROWS_PER_STEP
    return pl.pallas_call(
        functools.partial(_na_kernel, rows=rows, rows_per_step=rb),
        grid=(bsz, rows // rb),
        in_specs=[pl.BlockSpec((1, rb * GRID_W, GROUP_W), lambda b, i: (b, i, 0)),
                  pl.BlockSpec((1, seq, GROUP_W), lambda b, i: (b, 0, 0)),
                  pl.BlockSpec((1, seq, GROUP_W), lambda b, i: (b, 0, 0)),
                  _full_spec(bias)],
        out_specs=pl.BlockSpec((1, rb * GRID_W, GROUP_W), lambda b, i: (b, i, 0)),
        out_shape=jax.ShapeDtypeStruct((bsz, seq, GROUP_W), F32),
        compiler_params=_cparams("parallel", "arbitrary"),
        name="nbr_attn",
    )(qa, ka, va, bias)


def _cs(num, den):
    ang = 2.0 * np.pi * (np.asarray(num, np.float64) % den) / den
    return np.cos(ang), np.sin(ang)


def _hilo(m):
    m32 = jnp.asarray(m, F32)
    hi = m32.astype(BF16)
    return hi, (m32 - hi.astype(F32)).astype(BF16)


def _fnet_tables(seq):
    n1f = seq // LANES
    c, s = _cs(np.outer(np.arange(HEAD_DIM), np.arange(HEAD_DIM)), HEAD_DIM)
    norm = 1.0 / math.sqrt(HEAD_DIM * seq)
    eye = np.eye(GROUP_W // HEAD_DIM)
    fc = np.kron(eye, c) * norm
    fs = np.kron(eye, -s) * norm
    c1, s1 = _cs(np.outer(np.arange(n1f), np.arange(n1f)), n1f)
    m1 = np.block([[c1, -s1], [s1, c1]])
    ct, st = _cs(np.outer(np.arange(LANES), np.arange(n1f)), seq)
    t1 = np.concatenate([ct, ct], axis=1)
    t2 = np.concatenate([st, -st], axis=1)
    c2, s2 = _cs(np.outer(np.arange(LANES), np.arange(LANES)), LANES)
    g2 = np.concatenate([c2, s2], axis=0)
    return dict(fc=jnp.asarray(fc, F32).astype(BF16), fs=jnp.asarray(fs, F32).astype(BF16),
                m1=jnp.asarray(m1, F32).astype(BF16), t1=jnp.asarray(t1, F32), t2=jnp.asarray(t2, F32),
                g2=jnp.asarray(g2, F32).astype(BF16))


def _hyena_tables(seq):
    n = 2 * seq
    n1 = n // LANES
    c1, s1 = _cs(np.outer(np.arange(n1), np.arange(n1)), n1)
    f1 = np.concatenate([c1, -s1], axis=1)
    ct, st = _cs(np.outer(np.arange(LANES), np.arange(n1)), n)
    c2, s2 = _cs(np.outer(np.arange(LANES), np.arange(LANES)), LANES)
    g2 = np.block([[c2, -s2], [s2, c2]])
    gi2 = np.block([[c2, s2], [-s2, c2]])
    gi1 = np.concatenate([c1, -s1], axis=0) / n
    gi1[:, n1 // 2:] = 0.0
    return dict(f1=_hilo(f1), tc=jnp.asarray(ct, F32), ts=jnp.asarray(st, F32),
                tct=jnp.asarray(ct.T, F32), tst=jnp.asarray(st.T, F32),
                g2=_hilo(g2), gi2=_hilo(gi2), gi1=_hilo(gi1))


def _mm(a, tab, precise):
    hi, lo = tab
    if not precise:
        return _dot(a.astype(BF16), hi)
    a_hi, a_lo = _split2(a)
    return _dot(a_hi, hi) + (_dot(a_lo, hi) + _dot(a_hi, lo))


def _fnet_kernel(z_ref, m1_ref, t1_ref, t2_ref, g2_ref, o_ref):
    cb, n2, w = z_ref.shape[1:]
    n1f = w // 2
    a = _dot(z_ref[0].reshape(cb * n2, w).astype(BF16), m1_ref[...])
    a = a.reshape(cb, n2, w)
    sw = pltpu.roll(a.reshape(cb * n2, w), n1f, 1).reshape(cb, n2, w)
    a = a * t1_ref[...] + sw * t2_ref[...]
    at = jnp.swapaxes(a, 1, 2)
    op = jnp.concatenate([at[:, :n1f, :], at[:, n1f:, :]], axis=-1)
    x = _dot(op.reshape(cb * n1f, 2 * n2).astype(BF16), g2_ref[...])
    o_ref[0] = x.reshape(cb, n1f, n2)


def _fnet(zr, zi, tabs):
    bsz, seq, ch = zr.shape
    n1f = seq // LANES
    z = jnp.concatenate([zr.reshape(bsz, n1f, LANES, ch), zi.reshape(bsz, n1f, LANES, ch)], axis=1)
    z = z.transpose(0, 3, 2, 1)
    cb = FN_CH_BLOCK
    consts = (tabs["m1"], tabs["t1"], tabs["t2"], tabs["g2"])
    f = pl.pallas_call(
        _fnet_kernel,
        grid=(bsz, ch // cb),
        in_specs=[pl.BlockSpec((1, cb, LANES, 2 * n1f), lambda b, c: (b, c, 0, 0))] + [_full_spec(t) for t in consts],
        out_specs=pl.BlockSpec((1, cb, n1f, LANES), lambda b, c: (b, c, 0, 0)),
        out_shape=jax.ShapeDtypeStruct((bsz, ch, n1f, LANES), F32),
        compiler_params=_cparams("parallel", "parallel"),
        name="fourier_mix",
    )(z, *consts)
    return f.transpose(0, 3, 2, 1).reshape(bsz, seq, ch)


def _filter_mlp_kernel(feat_ref, t_ref, w1_ref, b1_ref, f0_ref, w2_ref, b2_ref, f1_ref, w3_ref, dec_ref, o_ref):
    hp = lax.Precision.HIGHEST
    h = jnp.sin(f0_ref[...] * (jnp.dot(feat_ref[...], w1_ref[...], precision=hp, preferred_element_type=F32)
                               + b1_ref[...]))
    h = jnp.sin(f1_ref[...] * (jnp.dot(h, w2_ref[...], precision=hp, preferred_element_type=F32) + b2_ref[...]))
    h = jnp.dot(h, w3_ref[0], precision=hp, preferred_element_type=F32)
    o_ref[...] = h * (jnp.exp(-t_ref[:, 0:1] * dec_ref[0]) * t_ref[:, 1:2])


def _hyena_filter_taps(seq, w1, b1, freq, w2, b2, w3, decay):
    ch = decay.shape[-1]
    r = jnp.arange(2 * seq)
    s = jnp.where(r < seq, r, 2 * seq - r).astype(F32)
    t = s / max(seq - 1, 1)
    ang = (2.0 * math.pi / seq) * s
    bands = jnp.linspace(1e-4, HY_BANDS - 1, HY_BANDS, dtype=F32)
    fb = ang[:, None] * bands[None, :]
    feats = jnp.concatenate([t[:, None], jnp.cos(fb), -jnp.sin(fb)], axis=-1)
    t_keep = jnp.stack([t, (r != seq).astype(F32)], axis=1)
    filt = w1.shape[1]
    pad_c = LANES - filt
    feats = jnp.pad(feats, ((0, 0), (0, LANES - HY_EMB)))
    w1p = jnp.pad(w1.astype(F32), ((0, LANES - HY_EMB), (0, pad_c)))
    w2p = jnp.pad(w2.astype(F32), ((0, pad_c), (0, pad_c)))
    w3s = w3.astype(F32).reshape(filt, HY_ORDER, 2, ch).transpose(2, 0, 1, 3).reshape(2, filt, HY_ORDER * ch)
    w3s = jnp.pad(w3s, ((0, 0), (0, pad_c), (0, 0)))
    decs = decay.astype(F32).transpose(1, 0, 2).reshape(2, 1, HY_ORDER * ch)
    row = lambda v: jnp.pad(v.astype(F32), (0, pad_c)).reshape(1, LANES)
    n_out = HY_ORDER * ch
    tl = min(FILT_TILE, seq)
    half = seq // tl
    side = lambda i: (i // half, 0, 0)
    args = (feats, t_keep, w1p, row(b1), row(freq[0]), w2p, row(b2), row(freq[1]), w3s, decs)
    specs = [pl.BlockSpec((tl, LANES), lambda i: (i, 0)), pl.BlockSpec((tl, 2), lambda i: (i, 0))]
    specs += [_full_spec(a) for a in args[2:8]]
    specs += [pl.BlockSpec((1, LANES, n_out), side), pl.BlockSpec((1, 1, n_out), side)]
    return pl.pallas_call(
        _filter_mlp_kernel,
        grid=(2 * half,),
        in_specs=specs,
        out_specs=pl.BlockSpec((tl, n_out), lambda i: (i, 0)),
        out_shape=jax.ShapeDtypeStruct((2 * seq, n_out), F32),
        compiler_params=_cparams("parallel"),
        name="hyena_filter_mlp",
    )(*args)


def _fft_fwd(u, f1, tc, ts, g2, precise):
    cb, n2, n1_in = u.shape
    n1 = f1[0].shape[1] // 2
    a = _mm(u.reshape(cb * n2, n1_in), f1, precise)
    ar = a[:, :n1].reshape(cb, n2, n1)
    ai = a[:, n1:].reshape(cb, n2, n1)
    br = ar * tc + ai * ts
    bi = ai * tc - ar * ts
    op = jnp.concatenate([jnp.swapaxes(br, 1, 2), jnp.swapaxes(bi, 1, 2)], axis=-1)
    return _mm(op.reshape(cb * n1, 2 * n2), g2, precise)


def _fft_inv(y, gi2, tct, tst, gi1, cb, n1, precise):
    n2 = y.shape[1] // 2
    d = _mm(y, gi2, precise)
    dr = d[:, :n2].reshape(cb, n1, n2)
    di = d[:, n2:].reshape(cb, n1, n2)
    er = dr * tct - di * tst
    ei = dr * tst + di * tct
    op = jnp.concatenate([jnp.swapaxes(er, 1, 2), jnp.swapaxes(ei, 1, 2)], axis=-1)
    return _mm(op.reshape(cb * n2, 2 * n1), gi1, precise).reshape(cb, n2, gi1[0].shape[1])


def _spectrum_kernel(h_ref, f1h, f1l, tc_ref, ts_ref, g2h, g2l, o_ref):
    x = _fft_fwd(h_ref[...], (f1h[...], f1l[...]), tc_ref[...], ts_ref[...], (g2h[...], g2l[...]), True)
    o_ref[...] = x.reshape(o_ref.shape)


def _hyena_spectra(full, tabs):
    n, nch = full.shape
    n1 = n // LANES
    hcm = full.reshape(n1, LANES, nch).transpose(2, 1, 0)
    cb = HY_CH_BLOCK
    consts = (*tabs["f1"], tabs["tc"], tabs["ts"], *tabs["g2"])
    return pl.pallas_call(
        _spectrum_kernel,
        grid=(nch // cb,),
        in_specs=[pl.BlockSpec((cb, LANES, n1), lambda c: (c, 0, 0))] + [_full_spec(t) for t in consts],
        out_specs=pl.BlockSpec((cb, n1, 2 * LANES), lambda c: (c, 0, 0)),
        out_shape=jax.ShapeDtypeStruct((nch, n1, 2 * LANES), F32),
        compiler_params=_cparams("parallel"),
        name="hyena_filter_spectrum",
    )(hcm, *consts)


def _hyena_kernel(uv_ref, u1_ref, u2_ref, cv_ref, c1_ref, c2_ref, skip_ref, k0_ref, k1_ref,
                  f1_ref, tc_ref, ts_ref, g2_ref, gi2_ref, tct_ref, tst_ref, gi1_ref, o_ref):
    cb, h1, n2 = uv_ref.shape[1:]
    n1 = k0_ref.shape[1]
    rows = cb * h1
    lane = lax.broadcasted_iota(jnp.int32, (rows, n2), 1)
    row_n1 = lax.broadcasted_iota(jnp.int32, (rows, n2), 0) % h1
    lane_first, lane_last = lane == 0, lane == n2 - 1
    seq_first = lane_first & (row_n1 == 0)
    seq_last = lane_last & (row_n1 == h1 - 1)

    def short_conv(u_ref, c_ref):
        u = u_ref[0]
        u2 = u.reshape(rows, n2)
        r = pltpu.roll(u2, 1, 1)
        prev = jnp.where(seq_first, 0.0, jnp.where(lane_first, pltpu.roll(r, 1, 0), r))
        r = pltpu.roll(u2, n2 - 1, 1)
        nxt = jnp.where(seq_last, 0.0, jnp.where(lane_last, pltpu.roll(r, rows - 1, 0), r))
        c = c_ref[...]
        return (prev.reshape(cb, h1, n2) * c[:, 0:1, :] + u * c[:, 1:2, :] + nxt.reshape(cb, h1, n2) * c[:, 2:3, :]
                + c[:, 3:4, :])

    tabs_f = ((f1_ref[...], None), tc_ref[...], ts_ref[...], (g2_ref[...], None))
    tabs_i = ((gi2_ref[...], None), tct_ref[...], tst_ref[...], (gi1_ref[...], None))

    def fftconv(u, k_ref):
        x = _fft_fwd(jnp.swapaxes(u, 1, 2), *tabs_f, False)
        kf = k_ref[...].reshape(cb * n1, 2 * n2)
        xr, xi = x[:, :n2], x[:, n2:]
        kr, ki = kf[:, :n2], kf[:, n2:]
        y = jnp.concatenate([xr * kr - xi * ki, xr * ki + xi * kr], axis=-1)
        return jnp.swapaxes(_fft_inv(y, *tabs_i, cb, n1, False), 1, 2)

    skip = skip_ref[...]
    v = short_conv(uv_ref, cv_ref)
    y1 = short_conv(u1_ref, c1_ref) * (fftconv(v, k0_ref) + v * skip[:, 0:1, :])
    o_ref[0] = short_conv(u2_ref, c2_ref) * (fftconv(y1, k1_ref) + y1 * skip[:, 1:2, :])


def _hyena(uc, conv_w, conv_b, skip, spectra, tabs):
    bsz, seq, _ = uc.shape
    ch = GROUP_W
    n1 = 2 * seq // LANES
    h1 = n1 // 2
    u = uc.transpose(0, 2, 1).reshape(bsz, 3 * ch, h1, LANES)
    cw = jnp.concatenate([conv_w.astype(F32), conv_b.astype(F32)[None]], axis=0).T
    cw = jnp.broadcast_to(cw[:, :, None], (3 * ch, 4, LANES))
    sk = jnp.broadcast_to(skip.astype(F32).T[:, :, None], (ch, HY_ORDER, LANES))
    cb = HY_CH_BLOCK
    nblk = ch // cb
    consts = (tabs["f1"][0][:h1], tabs["tc"], tabs["ts"], tabs["g2"][0], tabs["gi2"][0], tabs["tct"], tabs["tst"],
              tabs["gi1"][0][:, :h1])
    u_spec = lambda g: pl.BlockSpec((1, cb, h1, LANES), lambda c, b, g=g: (b, c + g * nblk, 0, 0))
    c_spec = lambda g: pl.BlockSpec((cb, 4, LANES), lambda c, b, g=g: (c + g * nblk, 0, 0))
    k_spec = lambda o: pl.BlockSpec((cb, n1, 2 * LANES), lambda c, b, o=o: (c + o * nblk, 0, 0))
    y = pl.pallas_call(
        _hyena_kernel,
        grid=(nblk, bsz),
        in_specs=[u_spec(0), u_spec(1), u_spec(2), c_spec(0), c_spec(1), c_spec(2),
                  pl.BlockSpec((cb, HY_ORDER, LANES), lambda c, b: (c, 0, 0)), k_spec(0), k_spec(1)]
                 + [_full_spec(t) for t in consts],
        out_specs=pl.BlockSpec((1, cb, h1, LANES), lambda c, b: (b, c, 0, 0)),
        out_shape=jax.ShapeDtypeStruct((bsz, ch, h1, LANES), F32),
        compiler_params=_cparams("parallel", "parallel"),
        name="hyena",
    )(u, u, u, cw, cw, cw, sk, spectra, spectra, *consts)
    return y.reshape(bsz, ch, seq).transpose(0, 2, 1)


def _lane_scan(x, c, reverse, op):
    n = x.shape[-1]
    ax = x.ndim - 1
    pos = lax.broadcasted_iota(jnp.int32, x.shape, ax) % c
    k = 1
    while k < c:
        if reverse:
            x = jnp.where(pos < c - k, op(x, pltpu.roll(x, n - k, ax)), x)
        else:
            x = jnp.where(pos >= k, op(x, pltpu.roll(x, k, ax)), x)
        k *= 2
    return x


ML_STAT_LANES = 16


def _mlstm_direction(d, q, v, kt, gr, s, m0, chunks):
    c = ML_CHUNK
    n = chunks * c
    nh = N_HEADS
    reverse = d == 1
    i_idx = 2 * nh * d
    li = gr[i_idx:i_idx + nh, :]
    b = _lane_scan(jax.nn.log_sigmoid(gr[i_idx + nh:i_idx + 2 * nh, :]), c, reverse, jnp.add)
    row = li - b
    cmax = _lane_scan(row, c, reverse, jnp.maximum)
    order = list(range(chunks - 1, -1, -1) if reverse else range(chunks))

    m_in, m_top, d_old = {}, {}, {}
    m = m0
    for k in order:
        edge = k * c if reverse else (k + 1) * c - 1
        m_in[k] = m
        m_top[k] = jnp.maximum(m, cmax[:, edge:edge + 1])
        d_old[k] = jnp.exp(m - m_top[k])
        m = b[:, edge:edge + 1] + m_top[k]
    m_in_row = jnp.concatenate([jnp.broadcast_to(m_in[k], (nh, c)) for k in range(chunks)], axis=1)
    m_top_row = jnp.concatenate([jnp.broadcast_to(m_top[k], (nh, c)) for k in range(chunks)], axis=1)
    mx = jnp.maximum(m_in_row, cmax)
    wi = jnp.exp(m_in_row - mx)
    einv = jnp.exp(-b - mx)
    w = jnp.exp(row - m_top_row)
    zeros4 = jnp.zeros_like(mx)
    stat_a = jnp.transpose(jnp.concatenate([mx, wi, zeros4, zeros4], axis=0))
    stat_b = jnp.transpose(jnp.concatenate([zeros4, einv, zeros4, zeros4], axis=0))

    jj = lax.broadcasted_iota(jnp.int32, (c, c), 0)
    ss = lax.broadcasted_iota(jnp.int32, (c, c), 1)
    causal = (ss >= jj) if reverse else (ss <= jj)
    lane_head = lax.broadcasted_iota(jnp.int32, (c, GROUP_W), 1) // HEAD_DIM
    row_head = lax.broadcasted_iota(jnp.int32, (GROUP_W, c), 0) // HEAD_DIM
    stat_lane = lax.broadcasted_iota(jnp.int32, (c, ML_STAT_LANES), 1)
    head_lanes = (stat_lane >= nh) & (stat_lane < 2 * nh)
    ones_bd = (lax.broadcasted_iota(jnp.int32, (nh * c, LANES), 0) // c + nh
               == lax.broadcasted_iota(jnp.int32, (nh * c, LANES), 1)).astype(BF16)
    expand = (lax.broadcasted_iota(jnp.int32, (ML_STAT_LANES, GROUP_W), 0) - nh
              == lax.broadcasted_iota(jnp.int32, (ML_STAT_LANES, GROUP_W), 1) // HEAD_DIM).astype(BF16)
    s_rh = lax.broadcasted_iota(jnp.int32, (GROUP_W, GROUP_W + LANES), 0) // HEAD_DIM
    s_col = lax.broadcasted_iota(jnp.int32, (GROUP_W, GROUP_W + LANES), 1)
    s_mask = jnp.where(s_col < GROUP_W, s_col // HEAD_DIM, s_col - GROUP_W - nh) == s_rh
    ones_cols = jnp.ones((c, LANES), BF16)

    outs = {}
    for k in order:
        tok = slice(k * c, (k + 1) * c)
        qc, vc, ktc = q[tok, :], v[tok, :], kt[:, tok]
        k_bd = jnp.concatenate([jnp.where(row_head == h, ktc, jnp.zeros_like(ktc)) for h in range(nh)], axis=1)
        v_bd = jnp.concatenate([jnp.where(lane_head == h, vc, jnp.zeros_like(vc)) for h in range(nh)], axis=0)
        qk = _dot(qc, k_bd)
        sa = stat_a[tok, :]
        p = jnp.concatenate(
            [jnp.exp(jnp.where(causal, row[h:h + 1, tok] - sa[:, h:h + 1], NEG)) * qk[:, h * c:(h + 1) * c]
             for h in range(nh)], axis=1).astype(BF16)
        pv = _dot(p, v_bd)
        p_sum = _dot(p, ones_bd)[:, :ML_STAT_LANES]
        qs = _dot(qc, s.astype(BF16))
        den = sa * qs[:, GROUP_W:GROUP_W + ML_STAT_LANES] + p_sum
        rden = jnp.where(head_lanes, 1.0 / jnp.maximum(jnp.abs(den), stat_b[tok, :]), 0.0)
        wi_c = jnp.where(head_lanes, sa, 0.0)
        e_hi, e_lo = _split2(jnp.concatenate([wi_c, rden], axis=0))
        ex = _dot(e_hi, expand) + _dot(e_lo, expand)
        outs[k] = (ex[:c] * qs[:, :GROUP_W] + pv) * ex[c:]

        w_full = jnp.concatenate([jnp.broadcast_to(w[h:h + 1, tok], (HEAD_DIM, c)) for h in range(nh)], axis=0)
        d_full = jnp.concatenate([jnp.broadcast_to(d_old[k][h:h + 1, :], (HEAD_DIM, 1)) for h in range(nh)], axis=0)
        ktw = (ktc.astype(F32) * w_full).astype(BF16)
        s_loc = _dot(ktw, jnp.concatenate([vc, ones_cols], axis=1))
        s = d_full * s + jnp.where(s_mask, s_loc, 0.0)
    return outs, s, m


def _mlstm_kernel(qf, vf, ktf, grf, qb, vb, ktb, grb, hf_ref, hb_ref, s_scr, m_scr, *, chunks):
    @pl.when(pl.program_id(1) == 0)
    def _():
        s_scr[...] = jnp.zeros_like(s_scr)
        m_scr[...] = jnp.zeros_like(m_scr)

    c = ML_CHUNK
    results = []
    for d, (q_ref, v_ref, kt_ref, gr_ref) in enumerate(((qf, vf, ktf, grf), (qb, vb, ktb, grb))):
        m0 = m_scr[d * 8:d * 8 + N_HEADS, :][:, :1]
        results.append(_mlstm_direction(d, q_ref[0], v_ref[0], kt_ref[0], gr_ref[0], s_scr[d], m0, chunks))
    for d, out_ref in enumerate((hf_ref, hb_ref)):
        outs, s, m = results[d]
        for k in range(chunks):
            out_ref[0, k * c:(k + 1) * c, :] = outs[k]
        s_scr[d] = s
        m_scr[d * 8:d * 8 + N_HEADS, :] = jnp.broadcast_to(m, (N_HEADS, LANES))


def _mlstm(qd, vd, kdt, grow):
    bsz, seq, _ = qd.shape
    g = min(ML_CHUNKS_PER_STEP, seq // ML_CHUNK)
    blk = g * ML_CHUNK
    nb = seq // blk
    fwd = lambda b, i: (b, i, 0)
    bwd = lambda b, i: (b, nb - 1 - i, 0)
    fwd_t = lambda b, i: (b, 0, i)
    bwd_t = lambda b, i: (b, 0, nb - 1 - i)

    def specs(tok, chan):
        return [pl.BlockSpec((1, blk, GROUP_W), tok), pl.BlockSpec((1, blk, GROUP_W), tok),
                pl.BlockSpec((1, GROUP_W, blk), chan), pl.BlockSpec((1, N_GATES, blk), chan)]

    args = (qd, vd, kdt, grow)
    return pl.pallas_call(
        functools.partial(_mlstm_kernel, chunks=g),
        grid=(bsz, nb),
        in_specs=specs(fwd, fwd_t) + specs(bwd, bwd_t),
        out_specs=[pl.BlockSpec((1, blk, GROUP_W), fwd), pl.BlockSpec((1, blk, GROUP_W), bwd)],
        out_shape=[jax.ShapeDtypeStruct((bsz, seq, GROUP_W), F32)] * 2,
        scratch_shapes=[pltpu.VMEM((2, GROUP_W, GROUP_W + LANES), F32), pltpu.VMEM((16, LANES), F32)],
        compiler_params=_cparams("parallel", "arbitrary"),
        name="mlstm",
    )(*args, *args)


def _post_kernel(x_ref, ya_ref, f_ref, yc_ref, hf_ref, hb_ref, od_ref, p_ref,
                 wfn_ref, onorm_ref, gsum_ref, gbc_ref, wout_ref, nffn_ref, wgate_ref, wup_ref, wdown_ref,
                 pnorm_ref, wpg_ref, wpp_ref, fnorm_ref, o_ref, *, final):
    yb = _dot(f_ref[0].astype(BF16), wfn_ref[...])
    yd = _sigmoid(od_ref[0]) * (hf_ref[0] + hb_ref[0])
    y = jnp.concatenate([ya_ref[0], yb, yc_ref[0], yd], axis=-1)
    sq_hi, sq_lo = _split2(y * y)
    ss = _dot(sq_hi, gsum_ref[...]) + _dot(sq_lo, gsum_ref[...])
    r_hi, r_lo = _split2(lax.rsqrt(ss * (1.0 / HEAD_DIM) + EPS))
    rb = _dot(r_hi, gbc_ref[...]) + _dot(r_lo, gbc_ref[...])
    x = x_ref[0] + _dot((y * rb * onorm_ref[...]).astype(BF16), wout_ref[...])
    hn = _rms(x, nffn_ref[...]).astype(BF16)
    acc = jnp.zeros_like(x)
    for c in range(D_FF // FF_CHUNK):
        sl = slice(c * FF_CHUNK, (c + 1) * FF_CHUNK)
        g = _dot(hn, wgate_ref[:, sl])
        u = _dot(hn, wup_ref[:, sl])
        acc = acc + _dot((g * _sigmoid(g) * u).astype(BF16), wdown_ref[sl, :])
    x = x + acc
    gate = _sigmoid(_dot(_rms(x, pnorm_ref[...]).astype(BF16), wpg_ref[...]))
    x = x + gate * _dot(p_ref[0].astype(BF16), wpp_ref[...])
    if final:
        x = _rms(x, fnorm_ref[...])
    o_ref[0] = x


def _post(x, ya, f, yc, hf, hb, od, p, lw, final_norm, final):
    bsz, seq, _ = x.shape
    tm = min(TOKEN_TILE, seq)
    n_groups = D_MODEL // HEAD_DIM
    ind = np.zeros((D_MODEL, LANES), np.float32)
    ind[np.arange(D_MODEL), np.arange(D_MODEL) // HEAD_DIM] = 1.0
    gsum = jnp.asarray(ind, BF16)
    gbc = jnp.asarray(ind.T, BF16)
    del n_groups
    fw = lw["fnet_w"].astype(F32)
    wfn = jax.scipy.linalg.block_diag(*[fw[g] for g in range(fw.shape[0])]).astype(BF16)
    row = lambda v: v.astype(F32).reshape(1, D_MODEL)
    weights = (wfn, row(lw["out_norm"]), gsum, gbc, lw["w_out"].astype(BF16), row(lw["norm_ffn"]),
               lw["w_gate"].astype(BF16), lw["w_up"].astype(BF16), lw["w_down"].astype(BF16),
               row(lw["ple_norm"]), lw["w_ple_gate"].astype(BF16), lw["w_ple_proj"].astype(BF16), row(final_norm))
    tok = lambda width: pl.BlockSpec((1, tm, width), lambda b, t: (b, t, 0))
    return pl.pallas_call(
        functools.partial(_post_kernel, final=final),
        grid=(bsz, seq // tm),
        in_specs=[tok(D_MODEL)] + [tok(GROUP_W)] * 6 + [tok(PLE_DIM)] + [_full_spec(w) for w in weights],
        out_specs=tok(D_MODEL),
        out_shape=jax.ShapeDtypeStruct((bsz, seq, D_MODEL), F32),
        compiler_params=_cparams("parallel", "parallel"),
        name="post",
    )(x, ya, f, yc, hf, hb, od, p, *weights)


def _layer_consts(lw, seq, hy_tabs):
    full = _hyena_filter_taps(seq, lw["hy_w1"], lw["hy_b1"], lw["hy_freq"], lw["hy_w2"], lw["hy_b2"], lw["hy_w3"],
                              lw["hy_decay"])
    return dict(spectra=_hyena_spectra(full, hy_tabs), na_bias=_na_bias_table(lw["attn_rpb"]))


def _trunk(x, p, layers, consts, final_norm, fn_tabs, hy_tabs):
    for i, (lw, lc) in enumerate(zip(layers, consts)):
        (qa, ka, va, zr, zi, uc, qd, vd, od, grow, kdt) = _inproj(
            x, lw["norm_mix"], lw["w_in"], lw["ml_gate_b"], (fn_tabs["fc"], fn_tabs["fs"]))
        ya = _na(qa, ka, va, lc["na_bias"])
        f = _fnet(zr, zi, fn_tabs)
        yc = _hyena(uc, lw["hy_conv_w"], lw["hy_conv_b"], lw["hy_skip"], lc["spectra"], hy_tabs)
        hf, hb = _mlstm(qd, vd, kdt, grow)
        x = _post(x, ya, f, yc, hf, hb, od, p[i], lw, final_norm, final=(i == len(layers) - 1))
    return x


_LAYER_KEYS = ("norm_mix", "w_in", "attn_rpb", "fnet_w", "hy_conv_w", "hy_conv_b", "hy_w1", "hy_b1", "hy_freq",
               "hy_w2", "hy_b2", "hy_w3", "hy_decay", "hy_skip", "ml_gate_b", "out_norm", "w_out", "norm_ffn",
               "w_gate", "w_up", "w_down", "ple_norm", "w_ple_gate", "w_ple_proj")


def kernel(x_prompt, x_sample, p_prompt, p_sample, norm_mix, w_in, attn_rpb, fnet_w, hy_conv_w, hy_conv_b, hy_w1,
           hy_b1, hy_freq, hy_w2, hy_b2, hy_w3, hy_decay, hy_skip, ml_gate_b, out_norm, w_out, norm_ffn, w_gate,
           w_up, w_down, ple_norm, w_ple_gate, w_ple_proj, final_norm):
    stacked = dict(zip(_LAYER_KEYS, (norm_mix, w_in, attn_rpb, fnet_w, hy_conv_w, hy_conv_b, hy_w1, hy_b1, hy_freq,
                                     hy_w2, hy_b2, hy_w3, hy_decay, hy_skip, ml_gate_b, out_norm, w_out, norm_ffn,
                                     w_gate, w_up, w_down, ple_norm, w_ple_gate, w_ple_proj)))
    depth = norm_mix.shape[0]
    layers = [{k: v[i] for k, v in stacked.items()} for i in range(depth)]
    outs = []
    cache = {}
    for x, p in ((x_prompt, p_prompt), (x_sample, p_sample)):
        seq = x.shape[1]
        if seq not in cache:
            fn_tabs = _fnet_tables(seq)
            hy_tabs = _hyena_tables(seq)
            cache[seq] = (fn_tabs, hy_tabs, [_layer_consts(lw, seq, hy_tabs) for lw in layers])
        fn_tabs, hy_tabs, consts = cache[seq]
        outs.append(_trunk(x, p, layers, consts, final_norm, fn_tabs, hy_tabs))
    return tuple(outs)
```

```python
import functools
import math

import numpy as np
import jax
import jax.numpy as jnp
from jax import lax
from jax.experimental import pallas as pl
from jax.experimental.pallas import tpu as pltpu

F32, BF16 = jnp.float32, jnp.bfloat16

D_MODEL = 1024
DEPTH = 2
GRID_W = 64
HEAD_DIM = 64
GROUP_W = 256
N_HEADS = 4
KH = 8
KW = 16
HY_ORDER = 2
HY_EMB = 33
HY_BANDS = 16
ML_CHUNK = 128
PLE_DIM = 256
D_FF = 2816
EPS = 1e-6
QK_SCALE = HEAD_DIM ** -0.5
NEG = -1e30
N_GATES = 4 * N_HEADS

LANES = 128
VMEM_LIMIT = 56 * 1024 * 1024
TOKEN_TILE = 512
INPROJ_TILE = 1024
FF_CHUNK = 256
NA_ROWS_PER_STEP = 8
HY_CH_BLOCK = 16
FN_CH_BLOCK = 32
FILT_TILE = 1024
ML_CHUNKS_PER_STEP = 4


def _cparams(*sem):
    return pltpu.CompilerParams(dimension_semantics=sem, vmem_limit_bytes=VMEM_LIMIT)


def _dot(a, b):
    return jnp.dot(a, b, preferred_element_type=F32)


def _dot_nt(a, b):
    return lax.dot_general(a, b, (((1,), (1,)), ((), ())), preferred_element_type=F32)


def _split2(x):
    hi = x.astype(BF16)
    return hi, (x - hi.astype(F32)).astype(BF16)


def _rms(x, g):
    return x * lax.rsqrt(jnp.mean(x * x, axis=-1, keepdims=True) + EPS) * g


def _sigmoid(x):
    return 1.0 / (1.0 + jnp.exp(-x))


def _full_spec(a):
    nd = a.ndim
    return pl.BlockSpec(a.shape, lambda *_: (0,) * nd, pipeline_mode=pl.Buffered(1))


def _inproj_kernel(x_ref, g_ref, wa_ref, wb_ref, wc_ref, wd_ref, wgt_ref, wkt_ref, fc_ref, fs_ref, gbr_ref,
                   qa_ref, ka_ref, va_ref, zr_ref, zi_ref, uc_ref, qd_ref, vd_ref, od_ref, grow_ref, kdt_ref):
    xn = _rms(x_ref[0], g_ref[...]).astype(BF16)
    a = _dot(xn, wa_ref[...])
    qa_ref[0] = (a[:, :GROUP_W] * QK_SCALE).astype(BF16)
    ka_ref[0] = a[:, GROUP_W:2 * GROUP_W].astype(BF16)
    va_ref[0] = a[:, 2 * GROUP_W:].astype(BF16)
    ub = _dot(xn, wb_ref[...]).astype(BF16)
    zr = _dot_nt(fc_ref[...], ub)
    zi = _dot_nt(fs_ref[...], ub)
    uc = _dot_nt(wc_ref[...], xn)
    for j in range(x_ref.shape[1] // LANES):
        lanes = slice(j * LANES, (j + 1) * LANES)
        zr_ref[0, :, j, :] = zr[:, lanes]
        zi_ref[0, :, j, :] = zi[:, lanes]
        uc_ref[0, :, j, :] = uc[:, lanes]
    d = _dot(xn, wd_ref[...])
    qd_ref[0] = d[:, :GROUP_W].astype(BF16)
    vd_ref[0] = d[:, 2 * GROUP_W:3 * GROUP_W].astype(BF16)
    od_ref[0] = d[:, 3 * GROUP_W:]
    grow_ref[0] = _dot_nt(wgt_ref[...], xn) + gbr_ref[...]
    kdt_ref[0] = (_dot_nt(wkt_ref[...], xn) * QK_SCALE).astype(BF16)


def _inproj(x, g, w_in, gate_b, fcs):
    bsz, seq, _ = x.shape
    tm = min(INPROJ_TILE, seq)
    wb16 = w_in.astype(BF16)
    o = 0
    wa = wb16[:, o:o + 3 * GROUP_W]; o += 3 * GROUP_W
    wb = wb16[:, o:o + GROUP_W]; o += GROUP_W
    wct = wb16[:, o:o + 3 * GROUP_W].T; o += 3 * GROUP_W
    wd = wb16[:, o:o + 4 * GROUP_W]; o += 4 * GROUP_W
    wgt = wb16[:, o:o + N_GATES].T
    wkt = wd[:, GROUP_W:2 * GROUP_W].T
    gbr = gate_b.astype(F32).reshape(N_GATES, 1)
    fc, fs = fcs
    weights = (g.astype(F32).reshape(1, D_MODEL), wa, wb, wct, wd, wgt, wkt, fc.T, fs.T, gbr)

    def tok(width, dtype):
        return jax.ShapeDtypeStruct((bsz, seq, width), dtype), pl.BlockSpec((1, tm, width), lambda b, t: (b, t, 0))

    def chan(height, dtype):
        return jax.ShapeDtypeStruct((bsz, height, seq), dtype), pl.BlockSpec((1, height, tm), lambda b, t: (b, 0, t))

    def chan_tiles(height):
        return (jax.ShapeDtypeStruct((bsz, height, seq // LANES, LANES), F32),
                pl.BlockSpec((1, height, tm // LANES, LANES), lambda b, t: (b, 0, t, 0)))

    outs = [tok(GROUP_W, BF16), tok(GROUP_W, BF16), tok(GROUP_W, BF16),
            chan_tiles(GROUP_W), chan_tiles(GROUP_W), chan_tiles(3 * GROUP_W),
            tok(GROUP_W, BF16), tok(GROUP_W, BF16), tok(GROUP_W, F32),
            chan(N_GATES, F32), chan(GROUP_W, BF16)]
    return pl.pallas_call(
        _inproj_kernel,
        grid=(bsz, seq // tm),
        in_specs=[pl.BlockSpec((1, tm, D_MODEL), lambda b, t: (b, t, 0))] + [_full_spec(w) for w in weights],
        out_specs=[s for _, s in outs],
        out_shape=[s for s, _ in outs],
        compiler_params=_cparams("parallel", "parallel"),
        name="inproj",
    )(x, *weights)


def _na_kernel(q_ref, k_ref, v_ref, bias_ref, o_ref, *, rows, rows_per_step):
    i = pl.program_id(1)
    lane_head = lax.broadcasted_iota(jnp.int32, (GRID_W, GROUP_W), 1) // HEAD_DIM
    masks = [lane_head == h for h in range(N_HEADS)]
    for j in range(rows_per_step):
        r = i * rows_per_step + j
        kr0 = jnp.clip(r - KH // 2, 0, rows - KH)
        case = r - kr0
        ks = pl.multiple_of(kr0 * GRID_W, GRID_W)
        q = q_ref[0, j * GRID_W:(j + 1) * GRID_W, :]
        qs = jnp.concatenate([jnp.where(m, q, jnp.zeros_like(q)) for m in masks], axis=0)
        kw = k_ref[0, pl.ds(ks, KH * GRID_W), :]
        vw = v_ref[0, pl.ds(ks, KH * GRID_W), :]
        s = _dot_nt(qs, kw) + bias_ref[case]
        p = jnp.exp(s - jnp.max(s, axis=-1, keepdims=True))
        den = jnp.sum(p, axis=-1, keepdims=True)
        o = _dot(p.astype(BF16), vw) / den
        out = jnp.zeros((GRID_W, GROUP_W), F32)
        for h in range(N_HEADS):
            out = out + jnp.where(masks[h], o[h * GRID_W:(h + 1) * GRID_W, :], 0.0)
        o_ref[0, j * GRID_W:(j + 1) * GRID_W, :] = out


def _na_bias_table(rpb):
    c = np.arange(GRID_W)
    kc = np.arange(GRID_W)
    kc0 = np.clip(c - KW // 2, 0, GRID_W - KW)
    valid = (kc[None, :] >= kc0[:, None]) & (kc[None, :] < kc0[:, None] + KW)
    dc = np.clip(kc[None, :] - c[:, None] + KW - 1, 0, 2 * KW - 2)
    case = np.arange(KH)
    j = np.arange(KH)
    dr = j[None, :] - case[:, None] + KH - 1
    row_sel = np.zeros((KH * KH, 2 * KH - 1), np.float32)
    row_sel[np.arange(KH * KH), dr.reshape(-1)] = 1.0
    col_sel = np.zeros((2 * KW - 1, GRID_W * GRID_W), np.float32)
    col_sel[dc.reshape(-1), np.arange(GRID_W * GRID_W)] = 1.0
    hp = lax.Precision.HIGHEST
    tab = jnp.einsum("rd,hde->hre", jnp.asarray(row_sel), rpb.astype(F32), precision=hp)
    tab = jnp.einsum("hre,ex->hrx", tab, jnp.asarray(col_sel), precision=hp)
    tab = tab.reshape(N_HEADS, KH, KH, GRID_W, GRID_W).transpose(1, 0, 3, 2, 4)
    tab = jnp.where(jnp.asarray(valid)[None, None, :, None, :], tab, NEG)
    return tab.reshape(KH, N_HEADS * GRID_W, KH * GRID_W)


def _na(qa, ka, va, bias):
    bsz, seq, _ = qa.shape
    rows = seq // GRID_W
    assert rows >= KH and rows % NA_ROWS_PER_STEP == 0
    rb = NA_ROWS_PER_STEP
    return pl.pallas_call(
        functools.partial(_na_kernel, rows=rows, rows_per_step=rb),
        grid=(bsz, rows // rb),
        in_specs=[pl.BlockSpec((1, rb * GRID_W, GROUP_W), lambda b, i: (b, i, 0)),
                  pl.BlockSpec((1, seq, GROUP_W), lambda b, i: (b, 0, 0)),
                  pl.BlockSpec((1, seq, GROUP_W), lambda b, i: (b, 0, 0)),
                  _full_spec(bias)],
        out_specs=pl.BlockSpec((1, rb * GRID_W, GROUP_W), lambda b, i: (b, i, 0)),
        out_shape=jax.ShapeDtypeStruct((bsz, seq, GROUP_W), F32),
        compiler_params=_cparams("parallel", "arbitrary"),
        name="nbr_attn",
    )(qa, ka, va, bias)


def _cs(num, den):
    ang = 2.0 * np.pi * (np.asarray(num, np.float64) % den) / den
    return np.cos(ang), np.sin(ang)


def _hilo(m):
    m32 = jnp.asarray(m, F32)
    hi = m32.astype(BF16)
    return hi, (m32 - hi.astype(F32)).astype(BF16)


def _fnet_tables(seq):
    n1f = seq // LANES
    c, s = _cs(np.outer(np.arange(HEAD_DIM), np.arange(HEAD_DIM)), HEAD_DIM)
    norm = 1.0 / math.sqrt(HEAD_DIM * seq)
    eye = np.eye(GROUP_W // HEAD_DIM)
    fc = np.kron(eye, c) * norm
    fs = np.kron(eye, -s) * norm
    c1, s1 = _cs(np.outer(np.arange(n1f), np.arange(n1f)), n1f)
    m1 = np.block([[c1, -s1], [s1, c1]])
    ct, st = _cs(np.outer(np.arange(LANES), np.arange(n1f)), seq)
    t1 = np.concatenate([ct, ct], axis=1)
    t2 = np.concatenate([st, -st], axis=1)
    c2, s2 = _cs(np.outer(np.arange(LANES), np.arange(LANES)), LANES)
    g2 = np.concatenate([c2, s2], axis=0)
    q = LANES // n1f
    g2 = np.stack([g2[:, a::q] for a in range(q)], axis=0)
    return dict(fc=jnp.asarray(fc, F32).astype(BF16), fs=jnp.asarray(fs, F32).astype(BF16),
                m1=jnp.asarray(m1, F32).astype(BF16), t1=jnp.asarray(t1, F32), t2=jnp.asarray(t2, F32),
                g2=jnp.asarray(g2, F32).astype(BF16))


def _hyena_tables(seq):
    n = 2 * seq
    n1 = n // LANES
    c1, s1 = _cs(np.outer(np.arange(n1), np.arange(n1)), n1)
    f1 = np.concatenate([c1, -s1], axis=1)
    ct, st = _cs(np.outer(np.arange(LANES), np.arange(n1)), n)
    c2, s2 = _cs(np.outer(np.arange(LANES), np.arange(LANES)), LANES)
    g2 = np.block([[c2, -s2], [s2, c2]])
    gi2 = np.block([[c2, s2], [-s2, c2]])
    gi1 = np.concatenate([c1, -s1], axis=0) / n
    gi1[:, n1 // 2:] = 0.0
    return dict(f1=_hilo(f1), tc=jnp.asarray(ct, F32), ts=jnp.asarray(st, F32),
                tct=jnp.asarray(ct.T, F32), tst=jnp.asarray(st.T, F32),
                g2=_hilo(g2), gi2=_hilo(gi2), gi1=_hilo(gi1))


def _mm(a, tab, precise):
    hi, lo = tab
    if not precise:
        return _dot(a.astype(BF16), hi)
    a_hi, a_lo = _split2(a)
    return _dot(a_hi, hi) + (_dot(a_lo, hi) + _dot(a_hi, lo))


def _fnet_kernel(zr_ref, zi_ref, m1_ref, t1_ref, t2_ref, g2_ref, o_ref):
    cb, n1f, n2 = zr_ref.shape[1:]
    w = 2 * n1f
    z = jnp.swapaxes(jnp.concatenate([zr_ref[0], zi_ref[0]], axis=1), 1, 2)
    a = _dot(z.reshape(cb * n2, w).astype(BF16), m1_ref[...])
    sw = pltpu.roll(a, n1f, 1).reshape(cb, n2, w)
    a = a.reshape(cb, n2, w) * t1_ref[...] + sw * t2_ref[...]
    at = jnp.swapaxes(a, 1, 2)
    op = jnp.concatenate([at[:, :n1f, :], at[:, n1f:, :]], axis=-1)
    op = op.reshape(cb * n1f, 2 * n2).astype(BF16)
    xt = jnp.concatenate([_dot(op, g2_ref[g]).reshape(cb, n1f, n1f) for g in range(g2_ref.shape[0])], axis=1)
    o_ref[0] = jnp.swapaxes(xt, 1, 2).reshape(cb, n1f * n2)


def _fnet(zr, zi, tabs):
    bsz, ch, n1f, _ = zr.shape
    cb = FN_CH_BLOCK
    consts = (tabs["m1"], tabs["t1"], tabs["t2"], tabs["g2"])
    z_spec = pl.BlockSpec((1, cb, n1f, LANES), lambda b, c: (b, c, 0, 0))
    return pl.pallas_call(
        _fnet_kernel,
        grid=(bsz, ch // cb),
        in_specs=[z_spec, z_spec] + [_full_spec(t) for t in consts],
        out_specs=pl.BlockSpec((1, cb, n1f * LANES), lambda b, c: (b, c, 0)),
        out_shape=jax.ShapeDtypeStruct((bsz, ch, n1f * LANES), F32),
        compiler_params=_cparams("parallel", "parallel"),
        name="fourier_mix",
    )(zr, zi, *consts)


def _filter_mlp_kernel(feat_ref, t_ref, w1_ref, b1_ref, f0_ref, w2_ref, b2_ref, f1_ref, w3_ref, dec_ref, o_ref):
    hp = lax.Precision.HIGHEST
    h = jnp.sin(f0_ref[...] * (jnp.dot(feat_ref[...], w1_ref[...], precision=hp, preferred_element_type=F32)
                               + b1_ref[...]))
    h = jnp.sin(f1_ref[...] * (jnp.dot(h, w2_ref[...], precision=hp, preferred_element_type=F32) + b2_ref[...]))
    h = jnp.dot(h, w3_ref[0], precision=hp, preferred_element_type=F32)
    o_ref[...] = h * (jnp.exp(-t_ref[:, 0:1] * dec_ref[0]) * t_ref[:, 1:2])


def _hyena_filter_taps(seq, w1, b1, freq, w2, b2, w3, decay):
    ch = decay.shape[-1]
    r = jnp.arange(2 * seq)
    s = jnp.where(r < seq, r, 2 * seq - r).astype(F32)
    t = s / max(seq - 1, 1)
    ang = (2.0 * math.pi / seq) * s
    bands = jnp.linspace(1e-4, HY_BANDS - 1, HY_BANDS, dtype=F32)
    fb = ang[:, None] * bands[None, :]
    feats = jnp.concatenate([t[:, None], jnp.cos(fb), -jnp.sin(fb)], axis=-1)
    t_keep = jnp.stack([t, (r != seq).astype(F32)], axis=1)
    filt = w1.shape[1]
    pad_c = LANES - filt
    feats = jnp.pad(feats, ((0, 0), (0, LANES - HY_EMB)))
    w1p = jnp.pad(w1.astype(F32), ((0, LANES - HY_EMB), (0, pad_c)))
    w2p = jnp.pad(w2.astype(F32), ((0, pad_c), (0, pad_c)))
    w3s = w3.astype(F32).reshape(filt, HY_ORDER, 2, ch).transpose(2, 0, 1, 3).reshape(2, filt, HY_ORDER * ch)
    w3s = jnp.pad(w3s, ((0, 0), (0, pad_c), (0, 0)))
    decs = decay.astype(F32).transpose(1, 0, 2).reshape(2, 1, HY_ORDER * ch)
    row = lambda v: jnp.pad(v.astype(F32), (0, pad_c)).reshape(1, LANES)
    n_out = HY_ORDER * ch
    tl = min(FILT_TILE, seq)
    half = seq // tl
    side = lambda i: (i // half, 0, 0)
    args = (feats, t_keep, w1p, row(b1), row(freq[0]), w2p, row(b2), row(freq[1]), w3s, decs)
    specs = [pl.BlockSpec((tl, LANES), lambda i: (i, 0)), pl.BlockSpec((tl, 2), lambda i: (i, 0))]
    specs += [_full_spec(a) for a in args[2:8]]
    specs += [pl.BlockSpec((1, LANES, n_out), side), pl.BlockSpec((1, 1, n_out), side)]
    return pl.pallas_call(
        _filter_mlp_kernel,
        grid=(2 * half,),
        in_specs=specs,
        out_specs=pl.BlockSpec((tl, n_out), lambda i: (i, 0)),
        out_shape=jax.ShapeDtypeStruct((2 * seq, n_out), F32),
        compiler_params=_cparams("parallel"),
        name="hyena_filter_mlp",
    )(*args)


def _fft_fwd(u, f1, tc, ts, g2, precise):
    cb, n2, n1_in = u.shape
    n1 = f1[0].shape[1] // 2
    a = _mm(u.reshape(cb * n2, n1_in), f1, precise)
    ar = a[:, :n1].reshape(cb, n2, n1)
    ai = a[:, n1:].reshape(cb, n2, n1)
    br = ar * tc + ai * ts
    bi = ai * tc - ar * ts
    op = jnp.concatenate([jnp.swapaxes(br, 1, 2), jnp.swapaxes(bi, 1, 2)], axis=-1)
    return _mm(op.reshape(cb * n1, 2 * n2), g2, precise)


def _fft_inv(y, gi2, tct, tst, gi1, cb, n1, precise):
    n2 = y.shape[1] // 2
    d = _mm(y, gi2, precise)
    dr = d[:, :n2].reshape(cb, n1, n2)
    di = d[:, n2:].reshape(cb, n1, n2)
    er = dr * tct - di * tst
    ei = dr * tst + di * tct
    op = jnp.concatenate([jnp.swapaxes(er, 1, 2), jnp.swapaxes(ei, 1, 2)], axis=-1)
    return _mm(op.reshape(cb * n2, 2 * n1), gi1, precise).reshape(cb, n2, gi1[0].shape[1])


def _spectrum_kernel(h_ref, f1h, f1l, tc_ref, ts_ref, g2h, g2l, o_ref):
    x = _fft_fwd(h_ref[...], (f1h[...], f1l[...]), tc_ref[...], ts_ref[...], (g2h[...], g2l[...]), True)
    o_ref[...] = x.reshape(o_ref.shape)


def _hyena_spectra(full, tabs):
    n, nch = full.shape
    n1 = n // LANES
    hcm = full.reshape(n1, LANES, nch).transpose(2, 1, 0)
    cb = HY_CH_BLOCK
    consts = (*tabs["f1"], tabs["tc"], tabs["ts"], *tabs["g2"])
    return pl.pallas_call(
        _spectrum_kernel,
        grid=(nch // cb,),
        in_specs=[pl.BlockSpec((cb, LANES, n1), lambda c: (c, 0, 0))] + [_full_spec(t) for t in consts],
        out_specs=pl.BlockSpec((cb, n1, 2 * LANES), lambda c: (c, 0, 0)),
        out_shape=jax.ShapeDtypeStruct((nch, n1, 2 * LANES), F32),
        compiler_params=_cparams("parallel"),
        name="hyena_filter_spectrum",
    )(hcm, *consts)


def _hyena_kernel(uv_ref, u1_ref, u2_ref, cv_ref, c1_ref, c2_ref, skip_ref, k0_ref, k1_ref,
                  f1_ref, tc_ref, ts_ref, g2_ref, gi2_ref, tct_ref, tst_ref, gi1_ref, o_ref):
    cb, h1, n2 = uv_ref.shape[1:]
    n1 = k0_ref.shape[1]
    rows = cb * h1
    lane = lax.broadcasted_iota(jnp.int32, (rows, n2), 1)
    row_n1 = lax.broadcasted_iota(jnp.int32, (rows, n2), 0) % h1
    lane_first, lane_last = lane == 0, lane == n2 - 1
    seq_first = lane_first & (row_n1 == 0)
    seq_last = lane_last & (row_n1 == h1 - 1)

    def short_conv(u_ref, c_ref):
        u = u_ref[0]
        u2 = u.reshape(rows, n2)
        r = pltpu.roll(u2, 1, 1)
        prev = jnp.where(seq_first, 0.0, jnp.where(lane_first, pltpu.roll(r, 1, 0), r))
        r = pltpu.roll(u2, n2 - 1, 1)
        nxt = jnp.where(seq_last, 0.0, jnp.where(lane_last, pltpu.roll(r, rows - 1, 0), r))
        c = c_ref[...]
        return (prev.reshape(cb, h1, n2) * c[:, 0:1, :] + u * c[:, 1:2, :] + nxt.reshape(cb, h1, n2) * c[:, 2:3, :]
                + c[:, 3:4, :])

    tabs_f = ((f1_ref[...], None), tc_ref[...], ts_ref[...], (g2_ref[...], None))
    tabs_i = ((gi2_ref[...], None), tct_ref[...], tst_ref[...], (gi1_ref[...], None))

    def fftconv(u, k_ref):
        x = _fft_fwd(jnp.swapaxes(u, 1, 2), *tabs_f, False)
        kf = k_ref[...].reshape(cb * n1, 2 * n2)
        xr, xi = x[:, :n2], x[:, n2:]
        kr, ki = kf[:, :n2], kf[:, n2:]
        y = jnp.concatenate([xr * kr - xi * ki, xr * ki + xi * kr], axis=-1)
        return jnp.swapaxes(_fft_inv(y, *tabs_i, cb, n1, False), 1, 2)

    skip = skip_ref[...]
    v = short_conv(uv_ref, cv_ref)
    y1 = short_conv(u1_ref, c1_ref) * (fftconv(v, k0_ref) + v * skip[:, 0:1, :])
    y2 = short_conv(u2_ref, c2_ref) * (fftconv(y1, k1_ref) + y1 * skip[:, 1:2, :])
    o_ref[0] = y2.reshape(cb, h1 * n2)


def _hyena(u, conv_w, conv_b, skip, spectra, tabs):
    bsz, _, h1, _ = u.shape
    ch = GROUP_W
    n1 = 2 * h1
    cw =jnp.concatenate([conv_w.astype(F32), conv_b.astype(F32)[None]], axis=0).T
    cw = jnp.broadcast_to(cw[:, :, None], (3 * ch, 4, LANES))
    sk = jnp.broadcast_to(skip.astype(F32).T[:, :, None], (ch, HY_ORDER, LANES))
    cb = HY_CH_BLOCK
    nblk = ch // cb
    consts = (tabs["f1"][0][:h1], tabs["tc"], tabs["ts"], tabs["g2"][0], tabs["gi2"][0], tabs["tct"], tabs["tst"],
              tabs["gi1"][0][:, :h1])
    u_spec = lambda g: pl.BlockSpec((1, cb, h1, LANES), lambda c, b, g=g: (b, c + g * nblk, 0, 0))
    c_spec = lambda g: pl.BlockSpec((cb, 4, LANES), lambda c, b, g=g: (c + g * nblk, 0, 0))
    k_spec = lambda o: pl.BlockSpec((cb, n1, 2 * LANES), lambda c, b, o=o: (c + o * nblk, 0, 0))
    y = pl.pallas_call(
        _hyena_kernel,
        grid=(nblk, bsz),
        in_specs=[u_spec(0), u_spec(1), u_spec(2), c_spec(0), c_spec(1), c_spec(2),
                  pl.BlockSpec((cb, HY_ORDER, LANES), lambda c, b: (c, 0, 0)), k_spec(0), k_spec(1)]
                 + [_full_spec(t) for t in consts],
        out_specs=pl.BlockSpec((1, cb, h1 * LANES), lambda c, b: (b, c, 0)),
        out_shape=jax.ShapeDtypeStruct((bsz, ch, h1 * LANES), F32),
        compiler_params=_cparams("parallel", "parallel"),
        name="hyena",
    )(u, u, u, cw, cw, cw, sk, spectra, spectra, *consts)
    return y


def _lane_scan(x, c, reverse, op):
    n = x.shape[-1]
    ax = x.ndim - 1
    pos = lax.broadcasted_iota(jnp.int32, x.shape, ax) % c
    k = 1
    while k < c:
        if reverse:
            x = jnp.where(pos < c - k, op(x, pltpu.roll(x, n - k, ax)), x)
        else:
            x = jnp.where(pos >= k, op(x, pltpu.roll(x, k, ax)), x)
        k *= 2
    return x


ML_STAT_LANES = 16


def _mlstm_direction(d, q, v, kt, gr, s, m0, chunks):
    c = ML_CHUNK
    n = chunks * c
    nh = N_HEADS
    reverse = d == 1
    i_idx = 2 * nh * d
    li = gr[i_idx:i_idx + nh, :]
    b = _lane_scan(jax.nn.log_sigmoid(gr[i_idx + nh:i_idx + 2 * nh, :]), c, reverse, jnp.add)
    row = li - b
    cmax = _lane_scan(row, c, reverse, jnp.maximum)
    order = list(range(chunks - 1, -1, -1) if reverse else range(chunks))

    m_in, m_top, d_old = {}, {}, {}
    m = m0
    for k in order:
        edge = k * c if reverse else (k + 1) * c - 1
        m_in[k] = m
        m_top[k] = jnp.maximum(m, cmax[:, edge:edge + 1])
        d_old[k] = jnp.exp(m - m_top[k])
        m = b[:, edge:edge + 1] + m_top[k]
    m_in_row = jnp.concatenate([jnp.broadcast_to(m_in[k], (nh, c)) for k in range(chunks)], axis=1)
    m_top_row = jnp.concatenate([jnp.broadcast_to(m_top[k], (nh, c)) for k in range(chunks)], axis=1)
    mx = jnp.maximum(m_in_row, cmax)
    wi = jnp.exp(m_in_row - mx)
    einv = jnp.exp(-b - mx)
    w = jnp.exp(row - m_top_row)
    zeros4 = jnp.zeros_like(mx)
    stat_a = jnp.transpose(jnp.concatenate([mx, wi, zeros4, zeros4], axis=0))
    stat_b = jnp.transpose(jnp.concatenate([zeros4, einv, zeros4, zeros4], axis=0))

    jj = lax.broadcasted_iota(jnp.int32, (c, c), 0)
    ss = lax.broadcasted_iota(jnp.int32, (c, c), 1)
    causal = (ss >= jj) if reverse else (ss <= jj)
    lane_head = lax.broadcasted_iota(jnp.int32, (c, GROUP_W), 1) // HEAD_DIM
    row_head = lax.broadcasted_iota(jnp.int32, (GROUP_W, c), 0) // HEAD_DIM
    stat_lane = lax.broadcasted_iota(jnp.int32, (c, ML_STAT_LANES), 1)
    head_lanes = (stat_lane >= nh) & (stat_lane < 2 * nh)
    ones_bd = (lax.broadcasted_iota(jnp.int32, (nh * c, LANES), 0) // c + nh
               == lax.broadcasted_iota(jnp.int32, (nh * c, LANES), 1)).astype(BF16)
    expand = (lax.broadcasted_iota(jnp.int32, (ML_STAT_LANES, GROUP_W), 0) - nh
              == lax.broadcasted_iota(jnp.int32, (ML_STAT_LANES, GROUP_W), 1) // HEAD_DIM).astype(BF16)
    s_rh = lax.broadcasted_iota(jnp.int32, (GROUP_W, GROUP_W + LANES), 0) // HEAD_DIM
    s_col = lax.broadcasted_iota(jnp.int32, (GROUP_W, GROUP_W + LANES), 1)
    s_mask = jnp.where(s_col < GROUP_W, s_col // HEAD_DIM, s_col - GROUP_W - nh) == s_rh
    ones_cols = jnp.ones((c, LANES), BF16)

    outs = {}
    for k in order:
        tok = slice(k * c, (k + 1) * c)
        qc, vc, ktc = q[tok, :], v[tok, :], kt[:, tok]
        k_bd = jnp.concatenate([jnp.where(row_head == h, ktc, jnp.zeros_like(ktc)) for h in range(nh)], axis=1)
        v_bd = jnp.concatenate([jnp.where(lane_head == h, vc, jnp.zeros_like(vc)) for h in range(nh)], axis=0)
        qk = _dot(qc, k_bd)
        sa = stat_a[tok, :]
        p = jnp.concatenate(
            [jnp.exp(jnp.where(causal, row[h:h + 1, tok] - sa[:, h:h + 1], NEG)) * qk[:, h * c:(h + 1) * c]
             for h in range(nh)], axis=1).astype(BF16)
        pv = _dot(p, v_bd)
        p_sum = _dot(p, ones_bd)[:, :ML_STAT_LANES]
        qs = _dot(qc, s.astype(BF16))
        den = sa * qs[:, GROUP_W:GROUP_W + ML_STAT_LANES] + p_sum
        rden = jnp.where(head_lanes, 1.0 / jnp.maximum(jnp.abs(den), stat_b[tok, :]), 0.0)
        wi_c = jnp.where(head_lanes, sa, 0.0)
        e_hi, e_lo = _split2(jnp.concatenate([wi_c, rden], axis=0))
        ex = _dot(e_hi, expand) + _dot(e_lo, expand)
        outs[k] = (ex[:c] * qs[:, :GROUP_W] + pv) * ex[c:]

        w_full = jnp.concatenate([jnp.broadcast_to(w[h:h + 1, tok], (HEAD_DIM, c)) for h in range(nh)], axis=0)
        d_full = jnp.concatenate([jnp.broadcast_to(d_old[k][h:h + 1, :], (HEAD_DIM, 1)) for h in range(nh)], axis=0)
        ktw = (ktc.astype(F32) * w_full).astype(BF16)
        s_loc = _dot(ktw, jnp.concatenate([vc, ones_cols], axis=1))
        s = d_full * s + jnp.where(s_mask, s_loc, 0.0)
    return outs, s, m


def _mlstm_kernel(qf, vf, ktf, grf, qb, vb, ktb, grb, hf_ref, hb_ref, s_scr, m_scr, *, chunks):
    @pl.when(pl.program_id(1) == 0)
    def _():
        s_scr[...] = jnp.zeros_like(s_scr)
        m_scr[...] = jnp.zeros_like(m_scr)

    c = ML_CHUNK
    results = []
    for d, (q_ref, v_ref, kt_ref, gr_ref) in enumerate(((qf, vf, ktf, grf), (qb, vb, ktb, grb))):
        m0 = m_scr[d * 8:d * 8 + N_HEADS, :][:, :1]
        results.append(_mlstm_direction(d, q_ref[0], v_ref[0], kt_ref[0], gr_ref[0], s_scr[d], m0, chunks))
    for d, out_ref in enumerate((hf_ref, hb_ref)):
        outs, s, m = results[d]
        for k in range(chunks):
            out_ref[0, k * c:(k + 1) * c, :] = outs[k]
        s_scr[d] = s
        m_scr[d * 8:d * 8 + N_HEADS, :] = jnp.broadcast_to(m, (N_HEADS, LANES))


def _mlstm(qd, vd, kdt, grow):
    bsz, seq, _ = qd.shape
    g = min(ML_CHUNKS_PER_STEP, seq // ML_CHUNK)
    blk = g * ML_CHUNK
    nb = seq // blk
    fwd = lambda b, i: (b, i, 0)
    bwd = lambda b, i: (b, nb - 1 - i, 0)
    fwd_t = lambda b, i: (b, 0, i)
    bwd_t = lambda b, i: (b, 0, nb - 1 - i)

    def specs(tok, chan):
        return [pl.BlockSpec((1, blk, GROUP_W), tok), pl.BlockSpec((1, blk, GROUP_W), tok),
                pl.BlockSpec((1, GROUP_W, blk), chan), pl.BlockSpec((1, N_GATES, blk), chan)]

    args = (qd, vd, kdt, grow)
    return pl.pallas_call(
        functools.partial(_mlstm_kernel, chunks=g),
        grid=(bsz, nb),
        in_specs=specs(fwd, fwd_t) + specs(bwd, bwd_t),
        out_specs=[pl.BlockSpec((1, blk, GROUP_W), fwd), pl.BlockSpec((1, blk, GROUP_W), bwd)],
        out_shape=[jax.ShapeDtypeStruct((bsz, seq, GROUP_W), F32)] * 2,
        scratch_shapes=[pltpu.VMEM((2, GROUP_W, GROUP_W + LANES), F32), pltpu.VMEM((16, LANES), F32)],
        compiler_params=_cparams("parallel", "arbitrary"),
        name="mlstm",
    )(*args, *args)


def _post_kernel(x_ref, ya_ref, f_ref, yc_ref, hf_ref, hb_ref, od_ref, p_ref,
                 wfn_ref, onorm_ref, gsum_ref, gbc_ref, wout_ref, nffn_ref, wgate_ref, wup_ref, wdown_ref,
                 pnorm_ref, wpg_ref, wpp_ref, fnorm_ref, o_ref, *, final):
    yb = _dot(jnp.transpose(f_ref[0]).astype(BF16), wfn_ref[...])
    yd = _sigmoid(od_ref[0]) * (hf_ref[0] + hb_ref[0])
    y = jnp.concatenate([ya_ref[0], yb, jnp.transpose(yc_ref[0]), yd], axis=-1)
    sq_hi, sq_lo = _split2(y * y)
    ss = _dot(sq_hi, gsum_ref[...]) + _dot(sq_lo, gsum_ref[...])
    r_hi, r_lo = _split2(lax.rsqrt(ss * (1.0 / HEAD_DIM) + EPS))
    rb = _dot(r_hi, gbc_ref[...]) + _dot(r_lo, gbc_ref[...])
    x = x_ref[0] + _dot((y * rb * onorm_ref[...]).astype(BF16), wout_ref[...])
    hn = _rms(x, nffn_ref[...]).astype(BF16)
    acc = jnp.zeros_like(x)
    for c in range(D_FF // FF_CHUNK):
        sl = slice(c * FF_CHUNK, (c + 1) * FF_CHUNK)
        g = _dot(hn, wgate_ref[:, sl])
        u = _dot(hn, wup_ref[:, sl])
        acc = acc + _dot((g * _sigmoid(g) * u).astype(BF16), wdown_ref[sl, :])
    x = x + acc
    gate = _sigmoid(_dot(_rms(x, pnorm_ref[...]).astype(BF16), wpg_ref[...]))
    x = x + gate * _dot(p_ref[0].astype(BF16), wpp_ref[...])
    if final:
        x = _rms(x, fnorm_ref[...])
    o_ref[0] = x


def _post(x, ya, f, yc, hf, hb, od, p, lw, final_norm, final):
    bsz, seq, _ = x.shape
    tm = min(TOKEN_TILE, seq)
    n_groups = D_MODEL // HEAD_DIM
    ind = np.zeros((D_MODEL, LANES), np.float32)
    ind[np.arange(D_MODEL), np.arange(D_MODEL) // HEAD_DIM] = 1.0
    gsum = jnp.asarray(ind, BF16)
    gbc = jnp.asarray(ind.T, BF16)
    del n_groups
    fw = lw["fnet_w"].astype(F32)
    wfn = jax.scipy.linalg.block_diag(*[fw[g] for g in range(fw.shape[0])]).astype(BF16)
    row = lambda v: v.astype(F32).reshape(1, D_MODEL)
    weights = (wfn, row(lw["out_norm"]), gsum, gbc, lw["w_out"].astype(BF16), row(lw["norm_ffn"]),
               lw["w_gate"].astype(BF16), lw["w_up"].astype(BF16), lw["w_down"].astype(BF16),
               row(lw["ple_norm"]), lw["w_ple_gate"].astype(BF16), lw["w_ple_proj"].astype(BF16), row(final_norm))
    tok = lambda width: pl.BlockSpec((1, tm, width), lambda b, t: (b, t, 0))
    chan = pl.BlockSpec((1, GROUP_W, tm), lambda b, t: (b, 0, t))
    return pl.pallas_call(
        functools.partial(_post_kernel, final=final),
        grid=(bsz, seq // tm),
        in_specs=[tok(D_MODEL), tok(GROUP_W), chan, chan] + [tok(GROUP_W)] * 3 + [tok(PLE_DIM)]
                 + [_full_spec(w) for w in weights],
        out_specs=tok(D_MODEL),
        out_shape=jax.ShapeDtypeStruct((bsz, seq, D_MODEL), F32),
        compiler_params=_cparams("parallel", "parallel"),
        name="post",
    )(x, ya, f, yc, hf, hb, od, p, *weights)


def _layer_consts(lw, seq, hy_tabs):
    full = _hyena_filter_taps(seq, lw["hy_w1"], lw["hy_b1"], lw["hy_freq"], lw["hy_w2"], lw["hy_b2"], lw["hy_w3"],
                              lw["hy_decay"])
    return dict(spectra=_hyena_spectra(full, hy_tabs), na_bias=_na_bias_table(lw["attn_rpb"]))


def _trunk(x, p, layers, consts, final_norm, fn_tabs, hy_tabs):
    for i, (lw, lc) in enumerate(zip(layers, consts)):
        (qa, ka, va, zr, zi, uc, qd, vd, od, grow, kdt) = _inproj(
            x, lw["norm_mix"], lw["w_in"], lw["ml_gate_b"], (fn_tabs["fc"], fn_tabs["fs"]))
        ya = _na(qa, ka, va, lc["na_bias"])
        f = _fnet(zr, zi, fn_tabs)
        yc = _hyena(uc, lw["hy_conv_w"], lw["hy_conv_b"], lw["hy_skip"], lc["spectra"], hy_tabs)
        hf, hb = _mlstm(qd, vd, kdt, grow)
        x = _post(x, ya, f, yc, hf, hb, od, p[i], lw, final_norm, final=(i == len(layers) - 1))
    return x


_LAYER_KEYS = ("norm_mix", "w_in", "attn_rpb", "fnet_w", "hy_conv_w", "hy_conv_b", "hy_w1", "hy_b1", "hy_freq",
               "hy_w2", "hy_b2", "hy_w3", "hy_decay", "hy_skip", "ml_gate_b", "out_norm", "w_out", "norm_ffn",
               "w_gate", "w_up", "w_down", "ple_norm", "w_ple_gate", "w_ple_proj")


def kernel(x_prompt, x_sample, p_prompt, p_sample, norm_mix, w_in, attn_rpb, fnet_w, hy_conv_w, hy_conv_b, hy_w1,
           hy_b1, hy_freq, hy_w2, hy_b2, hy_w3, hy_decay, hy_skip, ml_gate_b, out_norm, w_out, norm_ffn, w_gate,
           w_up, w_down, ple_norm, w_ple_gate, w_ple_proj, final_norm):
    stacked = dict(zip(_LAYER_KEYS, (norm_mix, w_in, attn_rpb, fnet_w, hy_conv_w, hy_conv_b, hy_w1, hy_b1, hy_freq,
                                     hy_w2, hy_b2, hy_w3, hy_decay, hy_skip, ml_gate_b, out_norm, w_out, norm_ffn,
                                     w_gate, w_up, w_down, ple_norm, w_ple_gate, w_ple_proj)))
    depth = norm_mix.shape[0]
    layers = [{k: v[i] for k, v in stacked.items()} for i in range(depth)]
    outs = []
    cache = {}
    for x, p in ((x_prompt, p_prompt), (x_sample, p_sample)):
        seq = x.shape[1]
        if seq not in cache:
            fn_tabs = _fnet_tables(seq)
            hy_tabs = _hyena_tables(seq)
            cache[seq] = (fn_tabs, hy_tabs, [_layer_consts(lw, seq, hy_tabs) for lw in layers])
        fn_tabs, hy_tabs, consts = cache[seq]
        outs.append(_trunk(x, p, layers, consts, final_norm, fn_tabs, hy_tabs))
    return tuple(outs)
```

```python
import functools
import math

import numpy as np
import jax
import jax.numpy as jnp
from jax import lax
from jax.experimental import pallas as pl
from jax.experimental.pallas import tpu as pltpu

F32, BF16 = jnp.float32, jnp.bfloat16

D_MODEL = 1024
DEPTH = 2
GRID_W = 64
HEAD_DIM = 64
GROUP_W = 256
N_HEADS = 4
KH = 8
KW = 16
HY_ORDER = 2
HY_EMB = 33
HY_BANDS = 16
ML_CHUNK = 128
PLE_DIM = 256
D_FF = 2816
EPS = 1e-6
QK_SCALE = HEAD_DIM ** -0.5
NEG = -1e30
N_GATES = 4 * N_HEADS
ML_STAT_ROWS = 6 * N_HEADS

LANES = 128
VMEM_LIMIT = 56 * 1024 * 1024
TOKEN_TILE = 512
INPROJ_TILE = 1024
FF_CHUNK = 256
NA_ROWS_PER_STEP = 8
HY_CH_BLOCK = 32
FN_CH_BLOCK = 32
FILT_TILE = 1024
ML_CHUNKS_PER_STEP = 8


def _cparams(*sem):
    return pltpu.CompilerParams(dimension_semantics=sem, vmem_limit_bytes=VMEM_LIMIT)


def _dot(a, b):
    return jnp.dot(a, b, preferred_element_type=F32)


def _dot_nt(a, b):
    return lax.dot_general(a, b, (((1,), (1,)), ((), ())), preferred_element_type=F32)


def _split2(x):
    hi = x.astype(BF16)
    return hi, (x - hi.astype(F32)).astype(BF16)


def _rms(x, g):
    return x * lax.rsqrt(jnp.mean(x * x, axis=-1, keepdims=True) + EPS) * g


def _sigmoid(x):
    return 1.0 / (1.0 + jnp.exp(-x))


def _full_spec(a):
    nd = a.ndim
    return pl.BlockSpec(a.shape, lambda *_: (0,) * nd, pipeline_mode=pl.Buffered(1))


def _inproj_kernel(x_ref, g_ref, wa_ref, wb_ref, wc_ref, wd_ref, wgt_ref, wkt_ref, fc_ref, fs_ref, gbr_ref,
                   qa_ref, ka_ref, va_ref, zr_ref, zi_ref, uc_ref, qd_ref, vd_ref, od_ref, grow_ref, kdt_ref):
    xn = _rms(x_ref[0], g_ref[...]).astype(BF16)
    gates = _dot_nt(wgt_ref[...], xn) + gbr_ref[...]
    stats = []
    for rev in range(2):
        li = gates[2 * N_HEADS * rev:2 * N_HEADS * rev + N_HEADS, :]
        lf = jax.nn.log_sigmoid(gates[2 * N_HEADS * rev + N_HEADS:2 * N_HEADS * (rev + 1), :])
        b = _lane_scan(lf, ML_CHUNK, rev == 1, jnp.add)
        stats += [b, li - b, _lane_scan(li - b, ML_CHUNK, rev == 1, jnp.maximum)]
    grow_ref[0] = jnp.concatenate(stats, axis=0)
    a = _dot(xn, wa_ref[...])
    qa_ref[0] = (a[:, :GROUP_W] * QK_SCALE).astype(BF16)
    ka_ref[0] = a[:, GROUP_W:2 * GROUP_W].astype(BF16)
    va_ref[0] = a[:, 2 * GROUP_W:].astype(BF16)
    ub = _dot(xn, wb_ref[...]).astype(BF16)
    zr = _dot_nt(fc_ref[...], ub)
    zi = _dot_nt(fs_ref[...], ub)
    uc = _dot_nt(wc_ref[...], xn)
    for j in range(x_ref.shape[1] // LANES):
        lanes = slice(j * LANES, (j + 1) * LANES)
        zr_ref[0, :, j, :] = zr[:, lanes]
        zi_ref[0, :, j, :] = zi[:, lanes]
        uc_ref[0, :, j, :] = uc[:, lanes]
    d = _dot(xn, wd_ref[...])
    qd_ref[0] = d[:, :GROUP_W].astype(BF16)
    vd_ref[0] = d[:, 2 * GROUP_W:3 * GROUP_W].astype(BF16)
    od_ref[0] = d[:, 3 * GROUP_W:]
    kdt_ref[0] = (_dot_nt(wkt_ref[...], xn) * QK_SCALE).astype(BF16)


def _inproj(x, g, w_in, gate_b, fcs):
    bsz, seq, _ = x.shape
    tm = min(INPROJ_TILE, seq)
    wb16 = w_in.astype(BF16)
    o = 0
    wa = wb16[:, o:o + 3 * GROUP_W]; o += 3 * GROUP_W
    wb = wb16[:, o:o + GROUP_W]; o += GROUP_W
    wct = wb16[:, o:o + 3 * GROUP_W].T; o += 3 * GROUP_W
    wd = wb16[:, o:o + 4 * GROUP_W]; o += 4 * GROUP_W
    wgt = wb16[:, o:o + N_GATES].T
    wkt = wd[:, GROUP_W:2 * GROUP_W].T
    gbr = gate_b.astype(F32).reshape(N_GATES, 1)
    fc, fs = fcs
    weights = (g.astype(F32).reshape(1, D_MODEL), wa, wb, wct, wd, wgt, wkt, fc.T, fs.T, gbr)

    def tok(width, dtype):
        return jax.ShapeDtypeStruct((bsz, seq, width), dtype), pl.BlockSpec((1, tm, width), lambda b, t: (b, t, 0))

    def chan(height, dtype):
        return jax.ShapeDtypeStruct((bsz, height, seq), dtype), pl.BlockSpec((1, height, tm), lambda b, t: (b, 0, t))

    def chan_tiles(height):
        return (jax.ShapeDtypeStruct((bsz, height, seq // LANES, LANES), F32),
                pl.BlockSpec((1, height, tm // LANES, LANES), lambda b, t: (b, 0, t, 0)))

    outs = [tok(GROUP_W, BF16), tok(GROUP_W, BF16), tok(GROUP_W, BF16),
            chan_tiles(GROUP_W), chan_tiles(GROUP_W), chan_tiles(3 * GROUP_W),
            tok(GROUP_W, BF16), tok(GROUP_W, BF16), tok(GROUP_W, F32),
            chan(ML_STAT_ROWS, F32), chan(GROUP_W, BF16)]
    return pl.pallas_call(
        _inproj_kernel,
        grid=(bsz, seq // tm),
        in_specs=[pl.BlockSpec((1, tm, D_MODEL), lambda b, t: (b, t, 0))] + [_full_spec(w) for w in weights],
        out_specs=[s for _, s in outs],
        out_shape=[s for s, _ in outs],
        compiler_params=_cparams("parallel", "parallel"),
        name="inproj",
    )(x, *weights)


def _na_kernel(q_ref, k_ref, v_ref, bias_ref, o_ref, *, rows, rows_per_step):
    i = pl.program_id(1)
    lane_head = lax.broadcasted_iota(jnp.int32, (GRID_W, GROUP_W), 1) // HEAD_DIM
    masks = [lane_head == h for h in range(N_HEADS)]
    for j in range(rows_per_step):
        r = i * rows_per_step + j
        kr0 = jnp.clip(r - KH // 2, 0, rows - KH)
        case = r - kr0
        ks = pl.multiple_of(kr0 * GRID_W, GRID_W)
        q = q_ref[0, j * GRID_W:(j + 1) * GRID_W, :]
        qs = jnp.concatenate([jnp.where(m, q, jnp.zeros_like(q)) for m in masks], axis=0)
        kw = k_ref[0, pl.ds(ks, KH * GRID_W), :]
        vw = v_ref[0, pl.ds(ks, KH * GRID_W), :]
        s = _dot_nt(qs, kw) + bias_ref[case]
        p = jnp.exp(s - jnp.max(s, axis=-1, keepdims=True))
        den = jnp.sum(p, axis=-1, keepdims=True)
        o = _dot(p.astype(BF16), vw) / den
        out = jnp.zeros((GRID_W, GROUP_W), F32)
        for h in range(N_HEADS):
            out = out + jnp.where(masks[h], o[h * GRID_W:(h + 1) * GRID_W, :], 0.0)
        o_ref[0, j * GRID_W:(j + 1) * GRID_W, :] = out


def _na_bias_table(rpb):
    c = np.arange(GRID_W)
    kc = np.arange(GRID_W)
    kc0 = np.clip(c - KW // 2, 0, GRID_W - KW)
    valid = (kc[None, :] >= kc0[:, None]) & (kc[None, :] < kc0[:, None] + KW)
    dc = np.clip(kc[None, :] - c[:, None] + KW - 1, 0, 2 * KW - 2)
    case = np.arange(KH)
    j = np.arange(KH)
    dr = j[None, :] - case[:, None] + KH - 1
    row_sel = np.zeros((KH * KH, 2 * KH - 1), np.float32)
    row_sel[np.arange(KH * KH), dr.reshape(-1)] = 1.0
    col_sel = np.zeros((2 * KW - 1, GRID_W * GRID_W), np.float32)
    col_sel[dc.reshape(-1), np.arange(GRID_W * GRID_W)] = 1.0
    hp = lax.Precision.HIGHEST
    tab = jnp.einsum("rd,hde->hre", jnp.asarray(row_sel), rpb.astype(F32), precision=hp)
    tab = jnp.einsum("hre,ex->hrx", tab, jnp.asarray(col_sel), precision=hp)
    tab = tab.reshape(N_HEADS, KH, KH, GRID_W, GRID_W).transpose(1, 0, 3, 2, 4)
    tab = jnp.where(jnp.asarray(valid)[None, None, :, None, :], tab, NEG)
    return tab.reshape(KH, N_HEADS * GRID_W, KH * GRID_W)


def _na(qa, ka, va, bias):
    bsz, seq, _ = qa.shape
    rows = seq // GRID_W
    assert rows >= KH and rows % NA_ROWS_PER_STEP == 0
    rb = NA_ROWS_PER_STEP
    return pl.pallas_call(
        functools.partial(_na_kernel, rows=rows, rows_per_step=rb),
        grid=(bsz, rows // rb),
        in_specs=[pl.BlockSpec((1, rb * GRID_W, GROUP_W), lambda b, i: (b, i, 0)),
                  pl.BlockSpec((1, seq, GROUP_W), lambda b, i: (b, 0, 0)),
                  pl.BlockSpec((1, seq, GROUP_W), lambda b, i: (b, 0, 0)),
                  _full_spec(bias)],
        out_specs=pl.BlockSpec((1, rb * GRID_W, GROUP_W), lambda b, i: (b, i, 0)),
        out_shape=jax.ShapeDtypeStruct((bsz, seq, GROUP_W), F32),
        compiler_params=_cparams("parallel", "arbitrary"),
        name="nbr_attn",
    )(qa, ka, va, bias)


def _cs(num, den):
    ang = 2.0 * np.pi * (np.asarray(num, np.float64) % den) / den
    return np.cos(ang), np.sin(ang)


def _hilo(m):
    m32 = jnp.asarray(m, F32)
    hi = m32.astype(BF16)
    return hi, (m32 - hi.astype(F32)).astype(BF16)


def _fnet_tables(seq):
    n1f = seq // LANES
    c, s = _cs(np.outer(np.arange(HEAD_DIM), np.arange(HEAD_DIM)), HEAD_DIM)
    norm = 1.0 / math.sqrt(HEAD_DIM * seq)
    eye = np.eye(GROUP_W // HEAD_DIM)
    fc = np.kron(eye, c) * norm
    fs = np.kron(eye, -s) * norm
    c1, s1 = _cs(np.outer(np.arange(n1f), np.arange(n1f)), n1f)
    m1 = np.block([[c1, -s1], [s1, c1]])
    ct, st = _cs(np.outer(np.arange(LANES), np.arange(n1f)), seq)
    t1 = np.concatenate([ct, ct], axis=1)
    t2 = np.concatenate([st, -st], axis=1)
    c2, s2 = _cs(np.outer(np.arange(LANES), np.arange(LANES)), LANES)
    g2 = np.concatenate([c2, s2], axis=0)
    q = LANES // n1f
    g2 = np.stack([g2[:, a::q] for a in range(q)], axis=0)
    return dict(fc=jnp.asarray(fc, F32).astype(BF16), fs=jnp.asarray(fs, F32).astype(BF16),
                m1=jnp.asarray(m1, F32).astype(BF16), t1=jnp.asarray(t1, F32), t2=jnp.asarray(t2, F32),
                g2=jnp.asarray(g2, F32).astype(BF16))


def _hyena_tables(seq):
    n = 2 * seq
    n1 = n // LANES
    c1, s1 = _cs(np.outer(np.arange(n1), np.arange(n1)), n1)
    f1 = np.concatenate([c1, -s1], axis=1)
    ct, st = _cs(np.outer(np.arange(LANES), np.arange(n1)), n)
    c2, s2 = _cs(np.outer(np.arange(LANES), np.arange(LANES)), LANES)
    g2 = np.block([[c2, -s2], [s2, c2]])
    gi2 = np.block([[c2, s2], [-s2, c2]])
    gi1 = np.concatenate([c1, -s1], axis=0) / n
    gi1[:, n1 // 2:] = 0.0
    return dict(f1=_hilo(f1), tc=jnp.asarray(ct, F32), ts=jnp.asarray(st, F32),
                tct=jnp.asarray(ct.T, F32), tst=jnp.asarray(st.T, F32),
                g2=_hilo(g2), gi2=_hilo(gi2), gi1=_hilo(gi1))


def _mm(a, tab, precise):
    hi, lo = tab
    if not precise:
        return _dot(a.astype(BF16), hi)
    a_hi, a_lo = _split2(a)
    return _dot(a_hi, hi) + (_dot(a_lo, hi) + _dot(a_hi, lo))


def _fnet_kernel(zr_ref, zi_ref, m1_ref, t1_ref, t2_ref, g2_ref, o_ref):
    cb, n1f, n2 = zr_ref.shape[1:]
    w = 2 * n1f
    z = jnp.swapaxes(jnp.concatenate([zr_ref[0], zi_ref[0]], axis=1), 1, 2)
    a = _dot(z.reshape(cb * n2, w).astype(BF16), m1_ref[...])
    sw = pltpu.roll(a, n1f, 1).reshape(cb, n2, w)
    a = a.reshape(cb, n2, w) * t1_ref[...] + sw * t2_ref[...]
    at = jnp.swapaxes(a, 1, 2)
    op = jnp.concatenate([at[:, :n1f, :], at[:, n1f:, :]], axis=-1)
    op = op.reshape(cb * n1f, 2 * n2).astype(BF16)
    xt = jnp.concatenate([_dot(op, g2_ref[g]).reshape(cb, n1f, n1f) for g in range(g2_ref.shape[0])], axis=1)
    o_ref[0] = jnp.swapaxes(xt, 1, 2).reshape(cb, n1f * n2)


def _fnet(zr, zi, tabs):
    bsz, ch, n1f, _ = zr.shape
    cb = FN_CH_BLOCK
    consts = (tabs["m1"], tabs["t1"], tabs["t2"], tabs["g2"])
    z_spec = pl.BlockSpec((1, cb, n1f, LANES), lambda b, c: (b, c, 0, 0))
    return pl.pallas_call(
        _fnet_kernel,
        grid=(bsz, ch // cb),
        in_specs=[z_spec, z_spec] + [_full_spec(t) for t in consts],
        out_specs=pl.BlockSpec((1, cb, n1f * LANES), lambda b, c: (b, c, 0)),
        out_shape=jax.ShapeDtypeStruct((bsz, ch, n1f * LANES), F32),
        compiler_params=_cparams("parallel", "parallel"),
        name="fourier_mix",
    )(zr, zi, *consts)


def _filter_mlp_kernel(feat_ref, t_ref, w1_ref, b1_ref, f0_ref, w2_ref, b2_ref, f1_ref, w3_ref, dec_ref, o_ref):
    hp = lax.Precision.HIGHEST
    h = jnp.sin(f0_ref[...] * (jnp.dot(feat_ref[...], w1_ref[...], precision=hp, preferred_element_type=F32)
                               + b1_ref[...]))
    h = jnp.sin(f1_ref[...] * (jnp.dot(h, w2_ref[...], precision=hp, preferred_element_type=F32) + b2_ref[...]))
    h = jnp.dot(h, w3_ref[0], precision=hp, preferred_element_type=F32)
    o_ref[...] = h * (jnp.exp(-t_ref[:, 0:1] * dec_ref[0]) * t_ref[:, 1:2])


def _hyena_filter_taps(seq, w1, b1, freq, w2, b2, w3, decay):
    ch = decay.shape[-1]
    r = jnp.arange(2 * seq)
    s = jnp.where(r < seq, r, 2 * seq - r).astype(F32)
    t = s / max(seq - 1, 1)
    ang = (2.0 * math.pi / seq) * s
    bands = jnp.linspace(1e-4, HY_BANDS - 1, HY_BANDS, dtype=F32)
    fb = ang[:, None] * bands[None, :]
    feats = jnp.concatenate([t[:, None], jnp.cos(fb), -jnp.sin(fb)], axis=-1)
    t_keep = jnp.stack([t, (r != seq).astype(F32)], axis=1)
    filt = w1.shape[1]
    pad_c = LANES - filt
    feats = jnp.pad(feats, ((0, 0), (0, LANES - HY_EMB)))
    w1p = jnp.pad(w1.astype(F32), ((0, LANES - HY_EMB), (0, pad_c)))
    w2p = jnp.pad(w2.astype(F32), ((0, pad_c), (0, pad_c)))
    w3s = w3.astype(F32).reshape(filt, HY_ORDER, 2, ch).transpose(2, 0, 1, 3).reshape(2, filt, HY_ORDER * ch)
    w3s = jnp.pad(w3s, ((0, 0), (0, pad_c), (0, 0)))
    decs = decay.astype(F32).transpose(1, 0, 2).reshape(2, 1, HY_ORDER * ch)
    row = lambda v: jnp.pad(v.astype(F32), (0, pad_c)).reshape(1, LANES)
    n_out = HY_ORDER * ch
    tl = min(FILT_TILE, seq)
    half = seq // tl
    side = lambda i: (i // half, 0, 0)
    args = (feats, t_keep, w1p, row(b1), row(freq[0]), w2p, row(b2), row(freq[1]), w3s, decs)
    specs = [pl.BlockSpec((tl, LANES), lambda i: (i, 0)), pl.BlockSpec((tl, 2), lambda i: (i, 0))]
    specs += [_full_spec(a) for a in args[2:8]]
    specs += [pl.BlockSpec((1, LANES, n_out), side), pl.BlockSpec((1, 1, n_out), side)]
    return pl.pallas_call(
        _filter_mlp_kernel,
        grid=(2 * half,),
        in_specs=specs,
        out_specs=pl.BlockSpec((tl, n_out), lambda i: (i, 0)),
        out_shape=jax.ShapeDtypeStruct((2 * seq, n_out), F32),
        compiler_params=_cparams("parallel"),
        name="hyena_filter_mlp",
    )(*args)


def _fft_fwd(u, f1, tc, ts, g2, precise):
    cb, n2, n1_in = u.shape
    n1 = f1[0].shape[1] // 2
    a = _mm(u.reshape(cb * n2, n1_in), f1, precise)
    ar = a[:, :n1].reshape(cb, n2, n1)
    ai = a[:, n1:].reshape(cb, n2, n1)
    br = ar * tc + ai * ts
    bi = ai * tc - ar * ts
    op = jnp.concatenate([jnp.swapaxes(br, 1, 2), jnp.swapaxes(bi, 1, 2)], axis=-1)
    return _mm(op.reshape(cb * n1, 2 * n2), g2, precise)


def _fft_inv(y, gi2, tct, tst, gi1, cb, n1, precise):
    n2 = y.shape[1] // 2
    d = _mm(y, gi2, precise)
    dr = d[:, :n2].reshape(cb, n1, n2)
    di = d[:, n2:].reshape(cb, n1, n2)
    er = dr * tct - di * tst
    ei = dr * tst + di * tct
    op = jnp.concatenate([jnp.swapaxes(er, 1, 2), jnp.swapaxes(ei, 1, 2)], axis=-1)
    return _mm(op.reshape(cb * n2, 2 * n1), gi1, precise).reshape(cb, n2, gi1[0].shape[1])


def _spectrum_kernel(h_ref, f1h, f1l, tc_ref, ts_ref, g2h, g2l, o_ref):
    x = _fft_fwd(h_ref[...], (f1h[...], f1l[...]), tc_ref[...], ts_ref[...], (g2h[...], g2l[...]), True)
    o_ref[...] = x.reshape(o_ref.shape)


def _hyena_spectra(full, tabs):
    n, nch = full.shape
    n1 = n // LANES
    hcm = full.reshape(n1, LANES, nch).transpose(2, 1, 0)
    cb = HY_CH_BLOCK
    consts = (*tabs["f1"], tabs["tc"], tabs["ts"], *tabs["g2"])
    return pl.pallas_call(
        _spectrum_kernel,
        grid=(nch // cb,),
        in_specs=[pl.BlockSpec((cb, LANES, n1), lambda c: (c, 0, 0))] + [_full_spec(t) for t in consts],
        out_specs=pl.BlockSpec((cb, n1, 2 * LANES), lambda c: (c, 0, 0)),
        out_shape=jax.ShapeDtypeStruct((nch, n1, 2 * LANES), F32),
        compiler_params=_cparams("parallel"),
        name="hyena_filter_spectrum",
    )(hcm, *consts)


def _hyena_kernel(uv_ref, u1_ref, u2_ref, cv_ref, c1_ref, c2_ref, skip_ref, k0_ref, k1_ref,
                  f1_ref, tc_ref, ts_ref, g2_ref, gi2_ref, tct_ref, tst_ref, gi1_ref, o_ref):
    cb, h1, n2 = uv_ref.shape[1:]
    n1 = k0_ref.shape[1]
    rows = cb * h1
    lane = lax.broadcasted_iota(jnp.int32, (rows, n2), 1)
    row_n1 = lax.broadcasted_iota(jnp.int32, (rows, n2), 0) % h1
    lane_first, lane_last = lane == 0, lane == n2 - 1
    seq_first = lane_first & (row_n1 == 0)
    seq_last = lane_last & (row_n1 == h1 - 1)

    def short_conv(u_ref, c_ref):
        u = u_ref[0]
        u2 = u.reshape(rows, n2)
        r = pltpu.roll(u2, 1, 1)
        prev = jnp.where(seq_first, 0.0, jnp.where(lane_first, pltpu.roll(r, 1, 0), r))
        r = pltpu.roll(u2, n2 - 1, 1)
        nxt = jnp.where(seq_last, 0.0, jnp.where(lane_last, pltpu.roll(r, rows - 1, 0), r))
        c = c_ref[...]
        return (prev.reshape(cb, h1, n2) * c[:, 0:1, :] + u * c[:, 1:2, :] + nxt.reshape(cb, h1, n2) * c[:, 2:3, :]
                + c[:, 3:4, :])

    tabs_f = ((f1_ref[...], None), tc_ref[...], ts_ref[...], (g2_ref[...], None))
    tabs_i = ((gi2_ref[...], None), tct_ref[...], tst_ref[...], (gi1_ref[...], None))

    def fftconv(u, k_ref):
        x = _fft_fwd(jnp.swapaxes(u, 1, 2), *tabs_f, False)
        kf = k_ref[...].reshape(cb * n1, 2 * n2)
        xr, xi = x[:, :n2], x[:, n2:]
        kr, ki = kf[:, :n2], kf[:, n2:]
        y = jnp.concatenate([xr * kr - xi * ki, xr * ki + xi * kr], axis=-1)
        return jnp.swapaxes(_fft_inv(y, *tabs_i, cb, n1, False), 1, 2)

    skip = skip_ref[...]
    v = short_conv(uv_ref, cv_ref)
    y1 = short_conv(u1_ref, c1_ref) * (fftconv(v, k0_ref) + v * skip[:, 0:1, :])
    y2 = short_conv(u2_ref, c2_ref) * (fftconv(y1, k1_ref) + y1 * skip[:, 1:2, :])
    o_ref[0] = y2.reshape(cb, h1 * n2)


def _hyena(u, conv_w, conv_b, skip, spectra, tabs):
    bsz, _, h1, _ = u.shape
    ch = GROUP_W
    n1 = 2 * h1
    cw =jnp.concatenate([conv_w.astype(F32), conv_b.astype(F32)[None]], axis=0).T
    cw = jnp.broadcast_to(cw[:, :, None], (3 * ch, 4, LANES))
    sk = jnp.broadcast_to(skip.astype(F32).T[:, :, None], (ch, HY_ORDER, LANES))
    cb = HY_CH_BLOCK
    nblk = ch // cb
    consts = (tabs["f1"][0][:h1], tabs["tc"], tabs["ts"], tabs["g2"][0], tabs["gi2"][0], tabs["tct"], tabs["tst"],
              tabs["gi1"][0][:, :h1])
    u_spec = lambda g: pl.BlockSpec((1, cb, h1, LANES), lambda c, b, g=g: (b, c + g * nblk, 0, 0))
    c_spec = lambda g: pl.BlockSpec((cb, 4, LANES), lambda c, b, g=g: (c + g * nblk, 0, 0))
    k_spec = lambda o: pl.BlockSpec((cb, n1, 2 * LANES), lambda c, b, o=o: (c + o * nblk, 0, 0))
    y = pl.pallas_call(
        _hyena_kernel,
        grid=(nblk, bsz),
        in_specs=[u_spec(0), u_spec(1), u_spec(2), c_spec(0), c_spec(1), c_spec(2),
                  pl.BlockSpec((cb, HY_ORDER, LANES), lambda c, b: (c, 0, 0)), k_spec(0), k_spec(1)]
                 + [_full_spec(t) for t in consts],
        out_specs=pl.BlockSpec((1, cb, h1 * LANES), lambda c, b: (b, c, 0)),
        out_shape=jax.ShapeDtypeStruct((bsz, ch, h1 * LANES), F32),
        compiler_params=_cparams("parallel", "parallel"),
        name="hyena",
    )(u, u, u, cw, cw, cw, sk, spectra, spectra, *consts)
    return y


def _lane_scan(x, c, reverse, op):
    n = x.shape[-1]
    ax = x.ndim - 1
    pos = lax.broadcasted_iota(jnp.int32, x.shape, ax) % c
    k = 1
    while k < c:
        if reverse:
            x = jnp.where(pos < c - k, op(x, pltpu.roll(x, n - k, ax)), x)
        else:
            x = jnp.where(pos >= k, op(x, pltpu.roll(x, k, ax)), x)
        k *= 2
    return x


ML_STAT_LANES = 16


def _mlstm_direction(d, q, v, kt, gr, s, m0, chunks):
    c = ML_CHUNK
    n = chunks * c
    nh = N_HEADS
    reverse = d == 1
    b, row, cmax = (gr[(3 * d + i) * nh:(3 * d + i + 1) * nh, :] for i in range(3))
    order = list(range(chunks - 1, -1, -1) if reverse else range(chunks))

    m_in, m_top, d_old = {}, {}, {}
    m = m0
    for k in order:
        edge = k * c if reverse else (k + 1) * c - 1
        m_in[k] = m
        m_top[k] = jnp.maximum(m, cmax[:, edge:edge + 1])
        d_old[k] = jnp.exp(m - m_top[k])
        m = b[:, edge:edge + 1] + m_top[k]
    m_in_row = jnp.concatenate([jnp.broadcast_to(m_in[k], (nh, c)) for k in range(chunks)], axis=1)
    m_top_row = jnp.concatenate([jnp.broadcast_to(m_top[k], (nh, c)) for k in range(chunks)], axis=1)
    mx = jnp.maximum(m_in_row, cmax)
    wi = jnp.exp(m_in_row - mx)
    einv = jnp.exp(-b - mx)
    w = jnp.exp(row - m_top_row)
    zeros4 = jnp.zeros_like(mx)
    stat_a = jnp.transpose(jnp.concatenate([mx, wi, zeros4, zeros4], axis=0))
    stat_b = jnp.transpose(jnp.concatenate([zeros4, einv, zeros4, zeros4], axis=0))

    jj = lax.broadcasted_iota(jnp.int32, (c, c), 0)
    ss = lax.broadcasted_iota(jnp.int32, (c, c), 1)
    causal = (ss >= jj) if reverse else (ss <= jj)
    lane_head = lax.broadcasted_iota(jnp.int32, (c, GROUP_W), 1) // HEAD_DIM
    row_head = lax.broadcasted_iota(jnp.int32, (GROUP_W, c), 0) // HEAD_DIM
    stat_lane = lax.broadcasted_iota(jnp.int32, (c, ML_STAT_LANES), 1)
    head_lanes = (stat_lane >= nh) & (stat_lane < 2 * nh)
    ones_bd = (lax.broadcasted_iota(jnp.int32, (nh * c, LANES), 0) // c + nh
               == lax.broadcasted_iota(jnp.int32, (nh * c, LANES), 1)).astype(BF16)
    expand = (lax.broadcasted_iota(jnp.int32, (ML_STAT_LANES, GROUP_W), 0) - nh
              == lax.broadcasted_iota(jnp.int32, (ML_STAT_LANES, GROUP_W), 1) // HEAD_DIM).astype(BF16)
    s_rh = lax.broadcasted_iota(jnp.int32, (GROUP_W, GROUP_W + LANES), 0) // HEAD_DIM
    s_col = lax.broadcasted_iota(jnp.int32, (GROUP_W, GROUP_W + LANES), 1)
    s_mask = jnp.where(s_col < GROUP_W, s_col // HEAD_DIM, s_col - GROUP_W - nh) == s_rh
    ones_cols = jnp.ones((c, LANES), BF16)

    outs = {}
    for k in order:
        tok = slice(k * c, (k + 1) * c)
        qc, vc, ktc = q[tok, :], v[tok, :], kt[:, tok]
        k_bd = jnp.concatenate([jnp.where(row_head == h, ktc, jnp.zeros_like(ktc)) for h in range(nh)], axis=1)
        v_bd = jnp.concatenate([jnp.where(lane_head == h, vc, jnp.zeros_like(vc)) for h in range(nh)], axis=0)
        qk = _dot(qc, k_bd)
        sa = stat_a[tok, :]
        p = jnp.concatenate(
            [jnp.exp(jnp.where(causal, row[h:h + 1, tok] - sa[:, h:h + 1], NEG)) * qk[:, h * c:(h + 1) * c]
             for h in range(nh)], axis=1).astype(BF16)
        pv = _dot(p, v_bd)
        p_sum = _dot(p, ones_bd)[:, :ML_STAT_LANES]
        qs = _dot(qc, s.astype(BF16))
        den = sa * qs[:, GROUP_W:GROUP_W + ML_STAT_LANES] + p_sum
        rden = jnp.where(head_lanes, 1.0 / jnp.maximum(jnp.abs(den), stat_b[tok, :]), 0.0)
        wi_c = jnp.where(head_lanes, sa, 0.0)
        ex = _dot(jnp.concatenate([wi_c, rden], axis=0).astype(BF16), expand)
        outs[k] = (ex[:c] * qs[:, :GROUP_W] + pv) * ex[c:]

        w_full = jnp.concatenate([jnp.broadcast_to(w[h:h + 1, tok], (HEAD_DIM, c)) for h in range(nh)], axis=0)
        d_full = jnp.concatenate([jnp.broadcast_to(d_old[k][h:h + 1, :], (HEAD_DIM, 1)) for h in range(nh)], axis=0)
        ktw = (ktc.astype(F32) * w_full).astype(BF16)
        s_loc = _dot(ktw, jnp.concatenate([vc, ones_cols], axis=1))
        s = d_full * s + jnp.where(s_mask, s_loc, 0.0)
    return outs, s, m


def _mlstm_kernel(qf, vf, ktf, grf, qb, vb, ktb, grb, hf_ref, hb_ref, s_scr, m_scr, *, chunks):
    @pl.when(pl.program_id(1) == 0)
    def _():
        s_scr[...] = jnp.zeros_like(s_scr)
        m_scr[...] = jnp.zeros_like(m_scr)

    c = ML_CHUNK
    results = []
    for d, (q_ref, v_ref, kt_ref, gr_ref) in enumerate(((qf, vf, ktf, grf), (qb, vb, ktb, grb))):
        m0 = m_scr[d * 8:d * 8 + N_HEADS, :][:, :1]
        results.append(_mlstm_direction(d, q_ref[0], v_ref[0], kt_ref[0], gr_ref[0], s_scr[d], m0, chunks))
    for d, out_ref in enumerate((hf_ref, hb_ref)):
        outs, s, m = results[d]
        for k in range(chunks):
            out_ref[0, k * c:(k + 1) * c, :] = outs[k]
        s_scr[d] = s
        m_scr[d * 8:d * 8 + N_HEADS, :] = jnp.broadcast_to(m, (N_HEADS, LANES))


def _mlstm(qd, vd, kdt, grow):
    bsz, seq, _ = qd.shape
    g = min(ML_CHUNKS_PER_STEP, seq // ML_CHUNK)
    blk = g * ML_CHUNK
    nb = seq // blk
    fwd = lambda b, i: (b, i, 0)
    bwd = lambda b, i: (b, nb - 1 - i, 0)
    fwd_t = lambda b, i: (b, 0, i)
    bwd_t = lambda b, i: (b, 0, nb - 1 - i)

    def specs(tok, chan):
        return [pl.BlockSpec((1, blk, GROUP_W), tok), pl.BlockSpec((1, blk, GROUP_W), tok),
                pl.BlockSpec((1, GROUP_W, blk), chan), pl.BlockSpec((1, ML_STAT_ROWS, blk), chan)]

    args = (qd, vd, kdt, grow)
    return pl.pallas_call(
        functools.partial(_mlstm_kernel, chunks=g),
        grid=(bsz, nb),
        in_specs=specs(fwd, fwd_t) + specs(bwd, bwd_t),
        out_specs=[pl.BlockSpec((1, blk, GROUP_W), fwd), pl.BlockSpec((1, blk, GROUP_W), bwd)],
        out_shape=[jax.ShapeDtypeStruct((bsz, seq, GROUP_W), F32)] * 2,
        scratch_shapes=[pltpu.VMEM((2, GROUP_W, GROUP_W + LANES), F32), pltpu.VMEM((16, LANES), F32)],
        compiler_params=_cparams("parallel", "arbitrary"),
        name="mlstm",
    )(*args, *args)


def _post_kernel(x_ref, ya_ref, f_ref, yc_ref, hf_ref, hb_ref, od_ref, p_ref,
                 wfn_ref, onorm_ref, gsum_ref, wout_ref, nffn_ref, wgate_ref, wup_ref, wdown_ref,
                 pnorm_ref, wpg_ref, wpp_ref, fnorm_ref, o_ref, *, final):
    yb = _dot(jnp.transpose(f_ref[0]).astype(BF16), wfn_ref[...])
    yd = _sigmoid(od_ref[0]) * (hf_ref[0] + hb_ref[0])
    y = jnp.concatenate([ya_ref[0], yb, jnp.transpose(yc_ref[0]), yd], axis=-1)
    ss = _dot((y * y).astype(BF16), gsum_ref[...])
    r = lax.rsqrt(ss * (1.0 / HEAD_DIM) + EPS)
    low_half = lax.broadcasted_iota(jnp.int32, (y.shape[0], LANES), 1) < HEAD_DIM
    rb = jnp.concatenate([jnp.where(low_half, r[:, 2 * j:2 * j + 1], r[:, 2 * j + 1:2 * j + 2])
                          for j in range(D_MODEL // LANES)], axis=1)
    x = x_ref[0] + _dot((y * rb * onorm_ref[...]).astype(BF16), wout_ref[...])
    hn = _rms(x, nffn_ref[...]).astype(BF16)
    acc = jnp.zeros_like(x)
    for c in range(D_FF // FF_CHUNK):
        sl = slice(c * FF_CHUNK, (c + 1) * FF_CHUNK)
        g = _dot(hn, wgate_ref[:, sl])
        u = _dot(hn, wup_ref[:, sl])
        acc = acc + _dot((g * _sigmoid(g) * u).astype(BF16), wdown_ref[sl, :])
    x = x + acc
    gate = _sigmoid(_dot(_rms(x, pnorm_ref[...]).astype(BF16), wpg_ref[...]))
    x = x + gate * _dot(p_ref[0].astype(BF16), wpp_ref[...])
    if final:
        x = _rms(x, fnorm_ref[...])
    o_ref[0] = x


def _post(x, ya, f, yc, hf, hb, od, p, lw, final_norm, final):
    bsz, seq, _ = x.shape
    tm = min(TOKEN_TILE, seq)
    ind = np.zeros((D_MODEL, LANES), np.float32)
    ind[np.arange(D_MODEL), np.arange(D_MODEL) // HEAD_DIM] = 1.0
    gsum = jnp.asarray(ind, BF16)
    fw = lw["fnet_w"].astype(F32)
    wfn = jax.scipy.linalg.block_diag(*[fw[g] for g in range(fw.shape[0])]).astype(BF16)
    row = lambda v: v.astype(F32).reshape(1, D_MODEL)
    weights = (wfn, row(lw["out_norm"]), gsum, lw["w_out"].astype(BF16), row(lw["norm_ffn"]),
               lw["w_gate"].astype(BF16), lw["w_up"].astype(BF16), lw["w_down"].astype(BF16),
               row(lw["ple_norm"]), lw["w_ple_gate"].astype(BF16), lw["w_ple_proj"].astype(BF16), row(final_norm))
    tok = lambda width: pl.BlockSpec((1, tm, width), lambda b, t: (b, t, 0))
    chan = pl.BlockSpec((1, GROUP_W, tm), lambda b, t: (b, 0, t))
    return pl.pallas_call(
        functools.partial(_post_kernel, final=final),
        grid=(bsz, seq // tm),
        in_specs=[tok(D_MODEL), tok(GROUP_W), chan, chan] + [tok(GROUP_W)] * 3 + [tok(PLE_DIM)]
                 + [_full_spec(w) for w in weights],
        out_specs=tok(D_MODEL),
        out_shape=jax.ShapeDtypeStruct((bsz, seq, D_MODEL), F32),
        compiler_params=_cparams("parallel", "parallel"),
        name="post",
    )(x, ya, f, yc, hf, hb, od, p, *weights)


def _layer_consts(lw, seq, hy_tabs):
    full = _hyena_filter_taps(seq, lw["hy_w1"], lw["hy_b1"], lw["hy_freq"], lw["hy_w2"], lw["hy_b2"], lw["hy_w3"],
                              lw["hy_decay"])
    return dict(spectra=_hyena_spectra(full, hy_tabs), na_bias=_na_bias_table(lw["attn_rpb"]))


def _trunk(x, p, layers, consts, final_norm, fn_tabs, hy_tabs):
    for i, (lw, lc) in enumerate(zip(layers, consts)):
        (qa, ka, va, zr, zi, uc, qd, vd, od, grow, kdt) = _inproj(
            x, lw["norm_mix"], lw["w_in"], lw["ml_gate_b"], (fn_tabs["fc"], fn_tabs["fs"]))
        ya = _na(qa, ka, va, lc["na_bias"])
        f = _fnet(zr, zi, fn_tabs)
        yc = _hyena(uc, lw["hy_conv_w"], lw["hy_conv_b"], lw["hy_skip"], lc["spectra"], hy_tabs)
        hf, hb = _mlstm(qd, vd, kdt, grow)
        x = _post(x, ya, f, yc, hf, hb, od, p[i], lw, final_norm, final=(i == len(layers) - 1))
    return x


_LAYER_KEYS = ("norm_mix", "w_in", "attn_rpb", "fnet_w", "hy_conv_w", "hy_conv_b", "hy_w1", "hy_b1", "hy_freq",
               "hy_w2", "hy_b2", "hy_w3", "hy_decay", "hy_skip", "ml_gate_b", "out_norm", "w_out", "norm_ffn",
               "w_gate", "w_up", "w_down", "ple_norm", "w_ple_gate", "w_ple_proj")


def kernel(x_prompt, x_sample, p_prompt, p_sample, norm_mix, w_in, attn_rpb, fnet_w, hy_conv_w, hy_conv_b, hy_w1,
           hy_b1, hy_freq, hy_w2, hy_b2, hy_w3, hy_decay, hy_skip, ml_gate_b, out_norm, w_out, norm_ffn, w_gate,
           w_up, w_down, ple_norm, w_ple_gate, w_ple_proj, final_norm):
    stacked = dict(zip(_LAYER_KEYS, (norm_mix, w_in, attn_rpb, fnet_w, hy_conv_w, hy_conv_b, hy_w1, hy_b1, hy_freq,
                                     hy_w2, hy_b2, hy_w3, hy_decay, hy_skip, ml_gate_b, out_norm, w_out, norm_ffn,
                                     w_gate, w_up, w_down, ple_norm, w_ple_gate, w_ple_proj)))
    depth = norm_mix.shape[0]
    layers = [{k: v[i] for k, v in stacked.items()} for i in range(depth)]
    outs = []
    cache = {}
    for x, p in ((x_prompt, p_prompt), (x_sample, p_sample)):
        seq = x.shape[1]
        if seq not in cache:
            fn_tabs = _fnet_tables(seq)
            hy_tabs = _hyena_tables(seq)
            cache[seq] = (fn_tabs, hy_tabs, [_layer_consts(lw, seq, hy_tabs) for lw in layers])
        fn_tabs, hy_tabs, consts = cache[seq]
        outs.append(_trunk(x, p, layers, consts, final_norm, fn_tabs, hy_tabs))
    return tuple(outs)
```

```python
import functools
import math

import numpy as np
import jax
import jax.numpy as jnp
from jax import lax
from jax.experimental import pallas as pl
from jax.experimental.pallas import tpu as pltpu

F32, BF16 = jnp.float32, jnp.bfloat16

D_MODEL = 1024
DEPTH = 2
GRID_W = 64
HEAD_DIM = 64
GROUP_W = 256
N_HEADS = 4
KH = 8
KW = 16
HY_ORDER = 2
HY_EMB = 33
HY_BANDS = 16
ML_CHUNK = 128
PLE_DIM = 256
D_FF = 2816
EPS = 1e-6
QK_SCALE = HEAD_DIM ** -0.5
NEG = -1e30
N_GATES = 4 * N_HEADS
ML_STAT_ROWS = 6 * N_HEADS

LANES = 128
VMEM_LIMIT = 56 * 1024 * 1024
TOKEN_TILE = 512
INPROJ_TILE = 1024
FF_CHUNK = 256
NA_ROWS_PER_STEP = 8
HY_CH_BLOCK = 32
FN_CH_BLOCK = 32
FILT_TILE = 1024
ML_CHUNKS_PER_STEP = 8


def _cparams(*sem):
    return pltpu.CompilerParams(dimension_semantics=sem, vmem_limit_bytes=VMEM_LIMIT)


def _dot(a, b):
    return jnp.dot(a, b, preferred_element_type=F32)


def _dot_nt(a, b):
    return lax.dot_general(a, b, (((1,), (1,)), ((), ())), preferred_element_type=F32)


def _split2(x):
    hi = x.astype(BF16)
    return hi, (x - hi.astype(F32)).astype(BF16)


def _rms(x, g):
    return x * lax.rsqrt(jnp.mean(x * x, axis=-1, keepdims=True) + EPS) * g


def _sigmoid(x):
    return 1.0 / (1.0 + jnp.exp(-x))


def _full_spec(a):
    nd = a.ndim
    return pl.BlockSpec(a.shape, lambda *_: (0,) * nd, pipeline_mode=pl.Buffered(1))


def _inproj_kernel(x_ref, g_ref, wa_ref, wb_ref, wc_ref, wd_ref, wgt_ref, fc_ref, fs_ref, gbr_ref,
                   qa_ref, ka_ref, va_ref, zr_ref, zi_ref, uc_ref, qd_ref, vd_ref, od_ref, grow_ref, kdt_ref):
    xn = _rms(x_ref[0], g_ref[...]).astype(BF16)
    gates = _dot_nt(wgt_ref[...], xn) + gbr_ref[...]
    stats = []
    for rev in range(2):
        li = gates[2 * N_HEADS * rev:2 * N_HEADS * rev + N_HEADS, :]
        lf = jax.nn.log_sigmoid(gates[2 * N_HEADS * rev + N_HEADS:2 * N_HEADS * (rev + 1), :])
        b = _lane_scan(lf, ML_CHUNK, rev == 1, jnp.add)
        stats += [b, li - b, _lane_scan(li - b, ML_CHUNK, rev == 1, jnp.maximum)]
    grow_ref[0] = jnp.concatenate(stats, axis=0)
    a = _dot(xn, wa_ref[...])
    qa_ref[0] = (a[:, :GROUP_W] * QK_SCALE).astype(BF16)
    ka_ref[0] = a[:, GROUP_W:2 * GROUP_W].astype(BF16)
    va_ref[0] = a[:, 2 * GROUP_W:].astype(BF16)
    ub = _dot(xn, wb_ref[...]).astype(BF16)
    zr = _dot_nt(fc_ref[...], ub)
    zi = _dot_nt(fs_ref[...], ub)
    uc = _dot_nt(wc_ref[...], xn)
    for j in range(x_ref.shape[1] // LANES):
        lanes = slice(j * LANES, (j + 1) * LANES)
        zr_ref[0, :, j, :] = zr[:, lanes]
        zi_ref[0, :, j, :] = zi[:, lanes]
        uc_ref[0, :, j, :] = uc[:, lanes]
    d = _dot(xn, wd_ref[...])
    qd_ref[0] = d[:, :GROUP_W].astype(BF16)
    vd_ref[0] = d[:, 2 * GROUP_W:3 * GROUP_W].astype(BF16)
    od_ref[0] = d[:, 3 * GROUP_W:]
    kdt_ref[0] = jnp.transpose(d[:, GROUP_W:2 * GROUP_W] * QK_SCALE).astype(BF16)


def _inproj(x, g, w_in, gate_b, fcs):
    bsz, seq, _ = x.shape
    tm = min(INPROJ_TILE, seq)
    wb16 = w_in.astype(BF16)
    o = 0
    wa = wb16[:, o:o + 3 * GROUP_W]; o += 3 * GROUP_W
    wb = wb16[:, o:o + GROUP_W]; o += GROUP_W
    wct = wb16[:, o:o + 3 * GROUP_W].T; o += 3 * GROUP_W
    wd = wb16[:, o:o + 4 * GROUP_W]; o += 4 * GROUP_W
    wgt = wb16[:, o:o + N_GATES].T
    gbr = gate_b.astype(F32).reshape(N_GATES, 1)
    fc, fs = fcs
    weights = (g.astype(F32).reshape(1, D_MODEL), wa, wb, wct, wd, wgt, fc.T, fs.T, gbr)

    def tok(width, dtype):
        return jax.ShapeDtypeStruct((bsz, seq, width), dtype), pl.BlockSpec((1, tm, width), lambda b, t: (b, t, 0))

    def chan(height, dtype):
        return jax.ShapeDtypeStruct((bsz, height, seq), dtype), pl.BlockSpec((1, height, tm), lambda b, t: (b, 0, t))

    def chan_tiles(height):
        return (jax.ShapeDtypeStruct((bsz, height, seq // LANES, LANES), F32),
                pl.BlockSpec((1, height, tm // LANES, LANES), lambda b, t: (b, 0, t, 0)))

    outs = [tok(GROUP_W, BF16), tok(GROUP_W, BF16), tok(GROUP_W, BF16),
            chan_tiles(GROUP_W), chan_tiles(GROUP_W), chan_tiles(3 * GROUP_W),
            tok(GROUP_W, BF16), tok(GROUP_W, BF16), tok(GROUP_W, F32),
            chan(ML_STAT_ROWS, F32), chan(GROUP_W, BF16)]
    return pl.pallas_call(
        _inproj_kernel,
        grid=(bsz, seq // tm),
        in_specs=[pl.BlockSpec((1, tm, D_MODEL), lambda b, t: (b, t, 0))] + [_full_spec(w) for w in weights],
        out_specs=[s for _, s in outs],
        out_shape=[s for s, _ in outs],
        compiler_params=_cparams("parallel", "parallel"),
        name="inproj",
    )(x, *weights)


def _na_kernel(q_ref, k_ref, v_ref, bias_ref, o_ref, *, rows, rows_per_step):
    i = pl.program_id(1)
    lane_head = lax.broadcasted_iota(jnp.int32, (GRID_W, GROUP_W), 1) // HEAD_DIM
    masks = [lane_head == h for h in range(N_HEADS)]
    for j in range(rows_per_step):
        r = i * rows_per_step + j
        kr0 = jnp.clip(r - KH // 2, 0, rows - KH)
        case = r - kr0
        ks = pl.multiple_of(kr0 * GRID_W, GRID_W)
        q = q_ref[0, j * GRID_W:(j + 1) * GRID_W, :]
        qs = jnp.concatenate([jnp.where(m, q, jnp.zeros_like(q)) for m in masks], axis=0)
        kw = k_ref[0, pl.ds(ks, KH * GRID_W), :]
        vw = v_ref[0, pl.ds(ks, KH * GRID_W), :]
        s = _dot_nt(qs, kw) + bias_ref[case]
        p = jnp.exp(s - jnp.max(s, axis=-1, keepdims=True))
        den = jnp.sum(p, axis=-1, keepdims=True)
        o = _dot(p.astype(BF16), vw) / den
        out = jnp.zeros((GRID_W, GROUP_W), F32)
        for h in range(N_HEADS):
            out = out + jnp.where(masks[h], o[h * GRID_W:(h + 1) * GRID_W, :], 0.0)
        o_ref[0, j * GRID_W:(j + 1) * GRID_W, :] = out


def _na_bias_table(rpb):
    c = np.arange(GRID_W)
    kc = np.arange(GRID_W)
    kc0 = np.clip(c - KW // 2, 0, GRID_W - KW)
    valid = (kc[None, :] >= kc0[:, None]) & (kc[None, :] < kc0[:, None] + KW)
    dc = np.clip(kc[None, :] - c[:, None] + KW - 1, 0, 2 * KW - 2)
    case = np.arange(KH)
    j = np.arange(KH)
    dr = j[None, :] - case[:, None] + KH - 1
    row_sel = np.zeros((KH * KH, 2 * KH - 1), np.float32)
    row_sel[np.arange(KH * KH), dr.reshape(-1)] = 1.0
    col_sel = np.zeros((2 * KW - 1, GRID_W * GRID_W), np.float32)
    col_sel[dc.reshape(-1), np.arange(GRID_W * GRID_W)] = 1.0
    hp = lax.Precision.HIGHEST
    tab = jnp.einsum("rd,hde->hre", jnp.asarray(row_sel), rpb.astype(F32), precision=hp)
    tab = jnp.einsum("hre,ex->hrx", tab, jnp.asarray(col_sel), precision=hp)
    tab = tab.reshape(N_HEADS, KH, KH, GRID_W, GRID_W).transpose(1, 0, 3, 2, 4)
    tab = jnp.where(jnp.asarray(valid)[None, None, :, None, :], tab, NEG)
    return tab.reshape(KH, N_HEADS * GRID_W, KH * GRID_W)


def _na(qa, ka, va, bias):
    bsz, seq, _ = qa.shape
    rows = seq // GRID_W
    assert rows >= KH and rows % NA_ROWS_PER_STEP == 0
    rb = NA_ROWS_PER_STEP
    return pl.pallas_call(
        functools.partial(_na_kernel, rows=rows, rows_per_step=rb),
        grid=(bsz, rows // rb),
        in_specs=[pl.BlockSpec((1, rb * GRID_W, GROUP_W), lambda b, i: (b, i, 0)),
                  pl.BlockSpec((1, seq, GROUP_W), lambda b, i: (b, 0, 0)),
                  pl.BlockSpec((1, seq, GROUP_W), lambda b, i: (b, 0, 0)),
                  _full_spec(bias)],
        out_specs=pl.BlockSpec((1, rb * GRID_W, GROUP_W), lambda b, i: (b, i, 0)),
        out_shape=jax.ShapeDtypeStruct((bsz, seq, GROUP_W), F32),
        compiler_params=_cparams("parallel", "arbitrary"),
        name="nbr_attn",
    )(qa, ka, va, bias)


def _cs(num, den):
    ang = 2.0 * np.pi * (np.asarray(num, np.float64) % den) / den
    return np.cos(ang), np.sin(ang)


def _hilo(m):
    m32 = jnp.asarray(m, F32)
    hi = m32.astype(BF16)
    return hi, (m32 - hi.astype(F32)).astype(BF16)


def _fnet_tables(seq):
    n1f = seq // LANES
    c, s = _cs(np.outer(np.arange(HEAD_DIM), np.arange(HEAD_DIM)), HEAD_DIM)
    norm = 1.0 / math.sqrt(HEAD_DIM * seq)
    eye = np.eye(GROUP_W // HEAD_DIM)
    fc = np.kron(eye, c) * norm
    fs = np.kron(eye, -s) * norm
    c1, s1 = _cs(np.outer(np.arange(n1f), np.arange(n1f)), n1f)
    m1 = np.block([[c1, -s1], [s1, c1]])
    ct, st = _cs(np.outer(np.arange(LANES), np.arange(n1f)), seq)
    t1 = np.concatenate([ct, ct], axis=1)
    t2 = np.concatenate([st, -st], axis=1)
    c2, s2 = _cs(np.outer(np.arange(LANES), np.arange(LANES)), LANES)
    g2 = np.concatenate([c2, s2], axis=0)
    q = LANES // n1f
    g2 = np.stack([g2[:, a::q] for a in range(q)], axis=0)
    return dict(fc=jnp.asarray(fc, F32).astype(BF16), fs=jnp.asarray(fs, F32).astype(BF16),
                m1=jnp.asarray(m1, F32).astype(BF16), t1=jnp.asarray(t1, F32), t2=jnp.asarray(t2, F32),
                g2=jnp.asarray(g2, F32).astype(BF16))


def _hyena_tables(seq):
    n = 2 * seq
    n1 = n // LANES
    c1, s1 = _cs(np.outer(np.arange(n1), np.arange(n1)), n1)
    f1 = np.concatenate([c1, -s1], axis=1)
    ct, st = _cs(np.outer(np.arange(LANES), np.arange(n1)), n)
    c2, s2 = _cs(np.outer(np.arange(LANES), np.arange(LANES)), LANES)
    g2 = np.block([[c2, -s2], [s2, c2]])
    gi2 = np.block([[c2, s2], [-s2, c2]])
    gi1 = np.concatenate([c1, -s1], axis=0) / n
    gi1[:, n1 // 2:] = 0.0
    return dict(f1=_hilo(f1), tc=jnp.asarray(ct, F32), ts=jnp.asarray(st, F32),
                tct=jnp.asarray(ct.T, F32), tst=jnp.asarray(st.T, F32),
                g2=_hilo(g2), gi2=_hilo(gi2), gi1=_hilo(gi1))


def _mm(a, tab, precise):
    hi, lo = tab
    if not precise:
        return _dot(a.astype(BF16), hi)
    a_hi, a_lo = _split2(a)
    return _dot(a_hi, hi) + (_dot(a_lo, hi) + _dot(a_hi, lo))


def _fnet_kernel(zr_ref, zi_ref, m1_ref, t1_ref, t2_ref, g2_ref, o_ref):
    cb, n1f, n2 = zr_ref.shape[1:]
    w = 2 * n1f
    z = jnp.swapaxes(jnp.concatenate([zr_ref[0], zi_ref[0]], axis=1), 1, 2)
    a = _dot(z.reshape(cb * n2, w).astype(BF16), m1_ref[...])
    sw = pltpu.roll(a, n1f, 1).reshape(cb, n2, w)
    a = a.reshape(cb, n2, w) * t1_ref[...] + sw * t2_ref[...]
    at = jnp.swapaxes(a, 1, 2)
    op = jnp.concatenate([at[:, :n1f, :], at[:, n1f:, :]], axis=-1)
    op = op.reshape(cb * n1f, 2 * n2).astype(BF16)
    xt = jnp.concatenate([_dot(op, g2_ref[g]).reshape(cb, n1f, n1f) for g in range(g2_ref.shape[0])], axis=1)
    o_ref[0] = jnp.swapaxes(xt, 1, 2).reshape(cb, n1f * n2)


def _fnet(zr, zi, tabs):
    bsz, ch, n1f, _ = zr.shape
    cb = FN_CH_BLOCK
    consts = (tabs["m1"], tabs["t1"], tabs["t2"], tabs["g2"])
    z_spec = pl.BlockSpec((1, cb, n1f, LANES), lambda b, c: (b, c, 0, 0))
    return pl.pallas_call(
        _fnet_kernel,
        grid=(bsz, ch // cb),
        in_specs=[z_spec, z_spec] + [_full_spec(t) for t in consts],
        out_specs=pl.BlockSpec((1, cb, n1f * LANES), lambda b, c: (b, c, 0)),
        out_shape=jax.ShapeDtypeStruct((bsz, ch, n1f * LANES), F32),
        compiler_params=_cparams("parallel", "parallel"),
        name="fourier_mix",
    )(zr, zi, *consts)


def _filter_mlp_kernel(feat_ref, t_ref, w1_ref, b1_ref, f0_ref, w2_ref, b2_ref, f1_ref, w3_ref, dec_ref, o_ref):
    hp = lax.Precision.HIGHEST
    h = jnp.sin(f0_ref[...] * (jnp.dot(feat_ref[...], w1_ref[...], precision=hp, preferred_element_type=F32)
                               + b1_ref[...]))
    h = jnp.sin(f1_ref[...] * (jnp.dot(h, w2_ref[...], precision=hp, preferred_element_type=F32) + b2_ref[...]))
    h = jnp.dot(h, w3_ref[0], precision=hp, preferred_element_type=F32)
    o_ref[...] = h * (jnp.exp(-t_ref[:, 0:1] * dec_ref[0]) * t_ref[:, 1:2])


def _hyena_filter_taps(seq, w1, b1, freq, w2, b2, w3, decay):
    ch = decay.shape[-1]
    r = jnp.arange(2 * seq)
    s = jnp.where(r < seq, r, 2 * seq - r).astype(F32)
    t = s / max(seq - 1, 1)
    ang = (2.0 * math.pi / seq) * s
    bands = jnp.linspace(1e-4, HY_BANDS - 1, HY_BANDS, dtype=F32)
    fb = ang[:, None] * bands[None, :]
    feats = jnp.concatenate([t[:, None], jnp.cos(fb), -jnp.sin(fb)], axis=-1)
    t_keep = jnp.stack([t, (r != seq).astype(F32)], axis=1)
    filt = w1.shape[1]
    pad_c = LANES - filt
    feats = jnp.pad(feats, ((0, 0), (0, LANES - HY_EMB)))
    w1p = jnp.pad(w1.astype(F32), ((0, LANES - HY_EMB), (0, pad_c)))
    w2p = jnp.pad(w2.astype(F32), ((0, pad_c), (0, pad_c)))
    w3s = w3.astype(F32).reshape(filt, HY_ORDER, 2, ch).transpose(2, 0, 1, 3).reshape(2, filt, HY_ORDER * ch)
    w3s = jnp.pad(w3s, ((0, 0), (0, pad_c), (0, 0)))
    decs = decay.astype(F32).transpose(1, 0, 2).reshape(2, 1, HY_ORDER * ch)
    row = lambda v: jnp.pad(v.astype(F32), (0, pad_c)).reshape(1, LANES)
    n_out = HY_ORDER * ch
    tl = min(FILT_TILE, seq)
    half = seq // tl
    side = lambda i: (i // half, 0, 0)
    args = (feats, t_keep, w1p, row(b1), row(freq[0]), w2p, row(b2), row(freq[1]), w3s, decs)
    specs = [pl.BlockSpec((tl, LANES), lambda i: (i, 0)), pl.BlockSpec((tl, 2), lambda i: (i, 0))]
    specs += [_full_spec(a) for a in args[2:8]]
    specs += [pl.BlockSpec((1, LANES, n_out), side), pl.BlockSpec((1, 1, n_out), side)]
    return pl.pallas_call(
        _filter_mlp_kernel,
        grid=(2 * half,),
        in_specs=specs,
        out_specs=pl.BlockSpec((tl, n_out), lambda i: (i, 0)),
        out_shape=jax.ShapeDtypeStruct((2 * seq, n_out), F32),
        compiler_params=_cparams("parallel"),
        name="hyena_filter_mlp",
    )(*args)


def _fft_fwd(u, f1, tc, ts, g2, precise):
    cb, n2, n1_in = u.shape
    n1 = f1[0].shape[1] // 2
    a = _mm(u.reshape(cb * n2, n1_in), f1, precise)
    ar = a[:, :n1].reshape(cb, n2, n1)
    ai = a[:, n1:].reshape(cb, n2, n1)
    br = ar * tc + ai * ts
    bi = ai * tc - ar * ts
    op = jnp.concatenate([jnp.swapaxes(br, 1, 2), jnp.swapaxes(bi, 1, 2)], axis=-1)
    return _mm(op.reshape(cb * n1, 2 * n2), g2, precise)


def _fft_inv(y, gi2, tct, tst, gi1, cb, n1, precise):
    n2 = y.shape[1] // 2
    d = _mm(y, gi2, precise)
    dr = d[:, :n2].reshape(cb, n1, n2)
    di = d[:, n2:].reshape(cb, n1, n2)
    er = dr * tct - di * tst
    ei = dr * tst + di * tct
    op = jnp.concatenate([jnp.swapaxes(er, 1, 2), jnp.swapaxes(ei, 1, 2)], axis=-1)
    return _mm(op.reshape(cb * n2, 2 * n1), gi1, precise).reshape(cb, n2, gi1[0].shape[1])


def _spectrum_kernel(h_ref, f1h, f1l, tc_ref, ts_ref, g2h, g2l, o_ref):
    x = _fft_fwd(h_ref[...], (f1h[...], f1l[...]), tc_ref[...], ts_ref[...], (g2h[...], g2l[...]), True)
    o_ref[...] = x.reshape(o_ref.shape)


def _hyena_spectra(full, tabs):
    n, nch = full.shape
    n1 = n // LANES
    hcm = full.reshape(n1, LANES, nch).transpose(2, 1, 0)
    cb = HY_CH_BLOCK
    consts = (*tabs["f1"], tabs["tc"], tabs["ts"], *tabs["g2"])
    return pl.pallas_call(
        _spectrum_kernel,
        grid=(nch // cb,),
        in_specs=[pl.BlockSpec((cb, LANES, n1), lambda c: (c, 0, 0))] + [_full_spec(t) for t in consts],
        out_specs=pl.BlockSpec((cb, n1, 2 * LANES), lambda c: (c, 0, 0)),
        out_shape=jax.ShapeDtypeStruct((nch, n1, 2 * LANES), F32),
        compiler_params=_cparams("parallel"),
        name="hyena_filter_spectrum",
    )(hcm, *consts)


def _hyena_kernel(uv_ref, u1_ref, u2_ref, cv_ref, c1_ref, c2_ref, skip_ref, k0_ref, k1_ref,
                  f1_ref, tc_ref, ts_ref, g2_ref, gi2_ref, tct_ref, tst_ref, gi1_ref, o_ref):
    cb, h1, n2 = uv_ref.shape[1:]
    n1 = k0_ref.shape[1]
    rows = cb * h1
    lane = lax.broadcasted_iota(jnp.int32, (rows, n2), 1)
    row_n1 = lax.broadcasted_iota(jnp.int32, (rows, n2), 0) % h1
    lane_first, lane_last = lane == 0, lane == n2 - 1
    seq_first = lane_first & (row_n1 == 0)
    seq_last = lane_last & (row_n1 == h1 - 1)

    def short_conv(u_ref, c_ref):
        u = u_ref[0]
        u2 = u.reshape(rows, n2)
        r = pltpu.roll(u2, 1, 1)
        prev = jnp.where(seq_first, 0.0, jnp.where(lane_first, pltpu.roll(r, 1, 0), r))
        r = pltpu.roll(u2, n2 - 1, 1)
        nxt = jnp.where(seq_last, 0.0, jnp.where(lane_last, pltpu.roll(r, rows - 1, 0), r))
        c = c_ref[...]
        return (prev.reshape(cb, h1, n2) * c[:, 0:1, :] + u * c[:, 1:2, :] + nxt.reshape(cb, h1, n2) * c[:, 2:3, :]
                + c[:, 3:4, :])

    tabs_f = ((f1_ref[...], None), tc_ref[...], ts_ref[...], (g2_ref[...], None))
    tabs_i = ((gi2_ref[...], None), tct_ref[...], tst_ref[...], (gi1_ref[...], None))

    def fftconv(u, k_ref):
        x = _fft_fwd(jnp.swapaxes(u, 1, 2), *tabs_f, False)
        kf = k_ref[...].reshape(cb * n1, 2 * n2)
        xr, xi = x[:, :n2], x[:, n2:]
        kr, ki = kf[:, :n2], kf[:, n2:]
        y = jnp.concatenate([xr * kr - xi * ki, xr * ki + xi * kr], axis=-1)
        return jnp.swapaxes(_fft_inv(y, *tabs_i, cb, n1, False), 1, 2)

    skip = skip_ref[...]
    v = short_conv(uv_ref, cv_ref)
    y1 = short_conv(u1_ref, c1_ref) * (fftconv(v, k0_ref) + v * skip[:, 0:1, :])
    y2 = short_conv(u2_ref, c2_ref) * (fftconv(y1, k1_ref) + y1 * skip[:, 1:2, :])
    o_ref[0] = y2.reshape(cb, h1 * n2)


def _hyena(u, conv_w, conv_b, skip, spectra, tabs):
    bsz, _, h1, _ = u.shape
    ch = GROUP_W
    n1 = 2 * h1
    cw =jnp.concatenate([conv_w.astype(F32), conv_b.astype(F32)[None]], axis=0).T
    cw = jnp.broadcast_to(cw[:, :, None], (3 * ch, 4, LANES))
    sk = jnp.broadcast_to(skip.astype(F32).T[:, :, None], (ch, HY_ORDER, LANES))
    cb = HY_CH_BLOCK
    nblk = ch // cb
    consts = (tabs["f1"][0][:h1], tabs["tc"], tabs["ts"], tabs["g2"][0], tabs["gi2"][0], tabs["tct"], tabs["tst"],
              tabs["gi1"][0][:, :h1])
    u_spec = lambda g: pl.BlockSpec((1, cb, h1, LANES), lambda c, b, g=g: (b, c + g * nblk, 0, 0))
    c_spec = lambda g: pl.BlockSpec((cb, 4, LANES), lambda c, b, g=g: (c + g * nblk, 0, 0))
    k_spec = lambda o: pl.BlockSpec((cb, n1, 2 * LANES), lambda c, b, o=o: (c + o * nblk, 0, 0))
    y = pl.pallas_call(
        _hyena_kernel,
        grid=(nblk, bsz),
        in_specs=[u_spec(0), u_spec(1), u_spec(2), c_spec(0), c_spec(1), c_spec(2),
                  pl.BlockSpec((cb, HY_ORDER, LANES), lambda c, b: (c, 0, 0)), k_spec(0), k_spec(1)]
                 + [_full_spec(t) for t in consts],
        out_specs=pl.BlockSpec((1, cb, h1 * LANES), lambda c, b: (b, c, 0)),
        out_shape=jax.ShapeDtypeStruct((bsz, ch, h1 * LANES), F32),
        compiler_params=_cparams("parallel", "parallel"),
        name="hyena",
    )(u, u, u, cw, cw, cw, sk, spectra, spectra, *consts)
    return y


def _lane_scan(x, c, reverse, op):
    n = x.shape[-1]
    ax = x.ndim - 1
    pos = lax.broadcasted_iota(jnp.int32, x.shape, ax) % c
    k = 1
    while k < c:
        if reverse:
            x = jnp.where(pos < c - k, op(x, pltpu.roll(x, n - k, ax)), x)
        else:
            x = jnp.where(pos >= k, op(x, pltpu.roll(x, k, ax)), x)
        k *= 2
    return x


ML_STAT_LANES = 16


def _mlstm_direction(d, q, v, kt, gr, s, m0, chunks):
    c = ML_CHUNK
    n = chunks * c
    nh = N_HEADS
    reverse = d == 1
    b, row, cmax = (gr[(3 * d + i) * nh:(3 * d + i + 1) * nh, :] for i in range(3))
    order = list(range(chunks - 1, -1, -1) if reverse else range(chunks))

    m_in, m_top, d_old = {}, {}, {}
    m = m0
    for k in order:
        edge = k * c if reverse else (k + 1) * c - 1
        m_in[k] = m
        m_top[k] = jnp.maximum(m, cmax[:, edge:edge + 1])
        d_old[k] = jnp.exp(m - m_top[k])
        m = b[:, edge:edge + 1] + m_top[k]
    m_in_row = jnp.concatenate([jnp.broadcast_to(m_in[k], (nh, c)) for k in range(chunks)], axis=1)
    m_top_row = jnp.concatenate([jnp.broadcast_to(m_top[k], (nh, c)) for k in range(chunks)], axis=1)
    mx = jnp.maximum(m_in_row, cmax)
    wi = jnp.exp(m_in_row - mx)
    einv = jnp.exp(-b - mx)
    w = jnp.exp(row - m_top_row)
    zeros4 = jnp.zeros_like(mx)
    stat_a = jnp.transpose(jnp.concatenate([mx, wi, zeros4, zeros4], axis=0))
    stat_b = jnp.transpose(jnp.concatenate([zeros4, einv, zeros4, zeros4], axis=0))

    jj = lax.broadcasted_iota(jnp.int32, (c, c), 0)
    ss = lax.broadcasted_iota(jnp.int32, (c, c), 1)
    causal = (ss >= jj) if reverse else (ss <= jj)
    lane_head = lax.broadcasted_iota(jnp.int32, (c, GROUP_W), 1) // HEAD_DIM
    row_head = lax.broadcasted_iota(jnp.int32, (GROUP_W, c), 0) // HEAD_DIM
    stat_lane = lax.broadcasted_iota(jnp.int32, (c, ML_STAT_LANES), 1)
    head_lanes = (stat_lane >= nh) & (stat_lane < 2 * nh)
    ones_bd = (lax.broadcasted_iota(jnp.int32, (nh * c, LANES), 0) // c + nh
               == lax.broadcasted_iota(jnp.int32, (nh * c, LANES), 1)).astype(BF16)
    expand = (lax.broadcasted_iota(jnp.int32, (ML_STAT_LANES, GROUP_W), 0) - nh
              == lax.broadcasted_iota(jnp.int32, (ML_STAT_LANES, GROUP_W), 1) // HEAD_DIM).astype(BF16)
    s_rh = lax.broadcasted_iota(jnp.int32, (GROUP_W, GROUP_W + LANES), 0) // HEAD_DIM
    s_col = lax.broadcasted_iota(jnp.int32, (GROUP_W, GROUP_W + LANES), 1)
    s_mask = jnp.where(s_col < GROUP_W, s_col // HEAD_DIM, s_col - GROUP_W - nh) == s_rh
    ones_cols = jnp.ones((c, LANES), BF16)

    outs = {}
    for k in order:
        tok = slice(k * c, (k + 1) * c)
        qc, vc, ktc = q[tok, :], v[tok, :], kt[:, tok]
        k_bd = jnp.concatenate([jnp.where(row_head == h, ktc, jnp.zeros_like(ktc)) for h in range(nh)], axis=1)
        v_bd = jnp.concatenate([jnp.where(lane_head == h, vc, jnp.zeros_like(vc)) for h in range(nh)], axis=0)
        qk = _dot(qc, k_bd)
        sa = stat_a[tok, :]
        p = jnp.concatenate(
            [jnp.exp(jnp.where(causal, row[h:h + 1, tok] - sa[:, h:h + 1], NEG)) * qk[:, h * c:(h + 1) * c]
             for h in range(nh)], axis=1).astype(BF16)
        pv = _dot(p, v_bd)
        p_sum = _dot(p, ones_bd)[:, :ML_STAT_LANES]
        qs = _dot(qc, s.astype(BF16))
        den = sa * qs[:, GROUP_W:GROUP_W + ML_STAT_LANES] + p_sum
        rden = jnp.where(head_lanes, 1.0 / jnp.maximum(jnp.abs(den), stat_b[tok, :]), 0.0)
        wi_c = jnp.where(head_lanes, sa, 0.0)
        ex = _dot(jnp.concatenate([wi_c, rden], axis=0).astype(BF16), expand)
        outs[k] = (ex[:c] * qs[:, :GROUP_W] + pv) * ex[c:]

        w_full = jnp.concatenate([jnp.broadcast_to(w[h:h + 1, tok], (HEAD_DIM, c)) for h in range(nh)], axis=0)
        d_full = jnp.concatenate([jnp.broadcast_to(d_old[k][h:h + 1, :], (HEAD_DIM, 1)) for h in range(nh)], axis=0)
        ktw = (ktc.astype(F32) * w_full).astype(BF16)
        s_loc = _dot(ktw, jnp.concatenate([vc, ones_cols], axis=1))
        s = d_full * s + jnp.where(s_mask, s_loc, 0.0)
    return outs, s, m


def _mlstm_kernel(qf, vf, ktf, grf, qb, vb, ktb, grb, hf_ref, hb_ref, s_scr, m_scr, *, chunks):
    @pl.when(pl.program_id(1) == 0)
    def _():
        s_scr[...] = jnp.zeros_like(s_scr)
        m_scr[...] = jnp.zeros_like(m_scr)

    c = ML_CHUNK
    results = []
    for d, (q_ref, v_ref, kt_ref, gr_ref) in enumerate(((qf, vf, ktf, grf), (qb, vb, ktb, grb))):
        m0 = m_scr[d * 8:d * 8 + N_HEADS, :][:, :1]
        results.append(_mlstm_direction(d, q_ref[0], v_ref[0], kt_ref[0], gr_ref[0], s_scr[d], m0, chunks))
    for d, out_ref in enumerate((hf_ref, hb_ref)):
        outs, s, m = results[d]
        for k in range(chunks):
            out_ref[0, k * c:(k + 1) * c, :] = outs[k]
        s_scr[d] = s
        m_scr[d * 8:d * 8 + N_HEADS, :] = jnp.broadcast_to(m, (N_HEADS, LANES))


def _mlstm(qd, vd, kdt, grow):
    bsz, seq, _ = qd.shape
    g = min(ML_CHUNKS_PER_STEP, seq // ML_CHUNK)
    blk = g * ML_CHUNK
    nb = seq // blk
    fwd = lambda b, i: (b, i, 0)
    bwd = lambda b, i: (b, nb - 1 - i, 0)
    fwd_t = lambda b, i: (b, 0, i)
    bwd_t = lambda b, i: (b, 0, nb - 1 - i)

    def specs(tok, chan):
        return [pl.BlockSpec((1, blk, GROUP_W), tok), pl.BlockSpec((1, blk, GROUP_W), tok),
                pl.BlockSpec((1, GROUP_W, blk), chan), pl.BlockSpec((1, ML_STAT_ROWS, blk), chan)]

    args = (qd, vd, kdt, grow)
    return pl.pallas_call(
        functools.partial(_mlstm_kernel, chunks=g),
        grid=(bsz, nb),
        in_specs=specs(fwd, fwd_t) + specs(bwd, bwd_t),
        out_specs=[pl.BlockSpec((1, blk, GROUP_W), fwd), pl.BlockSpec((1, blk, GROUP_W), bwd)],
        out_shape=[jax.ShapeDtypeStruct((bsz, seq, GROUP_W), F32)] * 2,
        scratch_shapes=[pltpu.VMEM((2, GROUP_W, GROUP_W + LANES), F32), pltpu.VMEM((16, LANES), F32)],
        compiler_params=_cparams("parallel", "arbitrary"),
        name="mlstm",
    )(*args, *args)


def _post_kernel(x_ref, ya_ref, f_ref, yc_ref, hf_ref, hb_ref, od_ref, p_ref,
                 wfn_ref, onorm_ref, gsum_ref, gbc_ref, wout_ref, nffn_ref, wgate_ref, wup_ref, wdown_ref,
                 pnorm_ref, wpg_ref, wpp_ref, fnorm_ref, o_ref, *, final):
    yb = _dot(jnp.transpose(f_ref[0]).astype(BF16), wfn_ref[...])
    yd = _sigmoid(od_ref[0]) * (hf_ref[0] + hb_ref[0])
    y = jnp.concatenate([ya_ref[0], yb, jnp.transpose(yc_ref[0]), yd], axis=-1)
    ss = _dot((y * y).astype(BF16), gsum_ref[...])
    rb = _dot(lax.rsqrt(ss * (1.0 / HEAD_DIM) + EPS).astype(BF16), gbc_ref[...])
    x = x_ref[0] + _dot((y * rb * onorm_ref[...]).astype(BF16), wout_ref[...])
    hn = _rms(x, nffn_ref[...]).astype(BF16)
    acc = jnp.zeros_like(x)
    for c in range(D_FF // FF_CHUNK):
        sl = slice(c * FF_CHUNK, (c + 1) * FF_CHUNK)
        g = _dot(hn, wgate_ref[:, sl])
        u = _dot(hn, wup_ref[:, sl])
        acc = acc + _dot((g * _sigmoid(g) * u).astype(BF16), wdown_ref[sl, :])
    x = x + acc
    gate = _sigmoid(_dot(_rms(x, pnorm_ref[...]).astype(BF16), wpg_ref[...]))
    x = x + gate * _dot(p_ref[0].astype(BF16), wpp_ref[...])
    if final:
        x = _rms(x, fnorm_ref[...])
    o_ref[0] = x


def _post(x, ya, f, yc, hf, hb, od, p, lw, final_norm, final):
    bsz, seq, _ = x.shape
    tm = min(TOKEN_TILE, seq)
    ind = np.zeros((D_MODEL, LANES), np.float32)
    ind[np.arange(D_MODEL), np.arange(D_MODEL) // HEAD_DIM] = 1.0
    gsum = jnp.asarray(ind, BF16)
    gbc = jnp.asarray(ind.T, BF16)
    fw = lw["fnet_w"].astype(F32)
    wfn = jax.scipy.linalg.block_diag(*[fw[g] for g in range(fw.shape[0])]).astype(BF16)
    row = lambda v: v.astype(F32).reshape(1, D_MODEL)
    weights = (wfn, row(lw["out_norm"]), gsum, gbc, lw["w_out"].astype(BF16), row(lw["norm_ffn"]),
               lw["w_gate"].astype(BF16), lw["w_up"].astype(BF16), lw["w_down"].astype(BF16),
               row(lw["ple_norm"]), lw["w_ple_gate"].astype(BF16), lw["w_ple_proj"].astype(BF16), row(final_norm))
    tok = lambda width: pl.BlockSpec((1, tm, width), lambda b, t: (b, t, 0))
    chan = pl.BlockSpec((1, GROUP_W, tm), lambda b, t: (b, 0, t))
    return pl.pallas_call(
        functools.partial(_post_kernel, final=final),
        grid=(bsz, seq // tm),
        in_specs=[tok(D_MODEL), tok(GROUP_W), chan, chan] + [tok(GROUP_W)] * 3 + [tok(PLE_DIM)]
                 + [_full_spec(w) for w in weights],
        out_specs=tok(D_MODEL),
        out_shape=jax.ShapeDtypeStruct((bsz, seq, D_MODEL), F32),
        compiler_params=_cparams("parallel", "parallel"),
        name="post",
    )(x, ya, f, yc, hf, hb, od, p, *weights)


def _layer_consts(lw, seq, hy_tabs):
    full = _hyena_filter_taps(seq, lw["hy_w1"], lw["hy_b1"], lw["hy_freq"], lw["hy_w2"], lw["hy_b2"], lw["hy_w3"],
                              lw["hy_decay"])
    return dict(spectra=_hyena_spectra(full, hy_tabs), na_bias=_na_bias_table(lw["attn_rpb"]))


def _trunk(x, p, layers, consts, final_norm, fn_tabs, hy_tabs):
    for i, (lw, lc) in enumerate(zip(layers, consts)):
        (qa, ka, va, zr, zi, uc, qd, vd, od, grow, kdt) = _inproj(
            x, lw["norm_mix"], lw["w_in"], lw["ml_gate_b"], (fn_tabs["fc"], fn_tabs["fs"]))
        ya = _na(qa, ka, va, lc["na_bias"])
        f = _fnet(zr, zi, fn_tabs)
        yc = _hyena(uc, lw["hy_conv_w"], lw["hy_conv_b"], lw["hy_skip"], lc["spectra"], hy_tabs)
        hf, hb = _mlstm(qd, vd, kdt, grow)
        x = _post(x, ya, f, yc, hf, hb, od, p[i], lw, final_norm, final=(i == len(layers) - 1))
    return x


_LAYER_KEYS = ("norm_mix", "w_in", "attn_rpb", "fnet_w", "hy_conv_w", "hy_conv_b", "hy_w1", "hy_b1", "hy_freq",
               "hy_w2", "hy_b2", "hy_w3", "hy_decay", "hy_skip", "ml_gate_b", "out_norm", "w_out", "norm_ffn",
               "w_gate", "w_up", "w_down", "ple_norm", "w_ple_gate", "w_ple_proj")


def kernel(x_prompt, x_sample, p_prompt, p_sample, norm_mix, w_in, attn_rpb, fnet_w, hy_conv_w, hy_conv_b, hy_w1,
           hy_b1, hy_freq, hy_w2, hy_b2, hy_w3, hy_decay, hy_skip, ml_gate_b, out_norm, w_out, norm_ffn, w_gate,
           w_up, w_down, ple_norm, w_ple_gate, w_ple_proj, final_norm):
    stacked = dict(zip(_LAYER_KEYS, (norm_mix, w_in, attn_rpb, fnet_w, hy_conv_w, hy_conv_b, hy_w1, hy_b1, hy_freq,
                                     hy_w2, hy_b2, hy_w3, hy_decay, hy_skip, ml_gate_b, out_norm, w_out, norm_ffn,
                                     w_gate, w_up, w_down, ple_norm, w_ple_gate, w_ple_proj)))
    depth = norm_mix.shape[0]
    layers = [{k: v[i] for k, v in stacked.items()} for i in range(depth)]
    outs = []
    cache = {}
    for x, p in ((x_prompt, p_prompt), (x_sample, p_sample)):
        seq = x.shape[1]
        if seq not in cache:
            fn_tabs = _fnet_tables(seq)
            hy_tabs = _hyena_tables(seq)
            cache[seq] = (fn_tabs, hy_tabs, [_layer_consts(lw, seq, hy_tabs) for lw in layers])
        fn_tabs, hy_tabs, consts = cache[seq]
        outs.append(_trunk(x, p, layers, consts, final_norm, fn_tabs, hy_tabs))
    return tuple(outs)
```

```python
import functools
import math

import numpy as np
import jax
import jax.numpy as jnp
from jax import lax
from jax.experimental import pallas as pl
from jax.experimental.pallas import tpu as pltpu

F32, BF16 = jnp.float32, jnp.bfloat16

D_MODEL = 1024
DEPTH = 2
GRID_W = 64
HEAD_DIM = 64
GROUP_W = 256
N_HEADS = 4
KH = 8
KW = 16
HY_ORDER = 2
HY_EMB = 33
HY_BANDS = 16
ML_CHUNK = 128
PLE_DIM = 256
D_FF = 2816
EPS = 1e-6
QK_SCALE = HEAD_DIM ** -0.5
NEG = -1e30
N_GATES = 4 * N_HEADS
ML_STAT_ROWS = 6 * N_HEADS

LANES = 128
VMEM_LIMIT = 56 * 1024 * 1024
TOKEN_TILE = 512
INPROJ_TILE = 1024
FF_CHUNK = 256
NA_ROWS_PER_STEP = 8
HY_CH_BLOCK = 32
HY_PARTS = 1
FN_CH_BLOCK = 32
FILT_TILE = 1024
ML_CHUNKS_PER_STEP = 8


def _cparams(*sem):
    return pltpu.CompilerParams(dimension_semantics=sem, vmem_limit_bytes=VMEM_LIMIT)


def _dot(a, b):
    return jnp.dot(a, b, preferred_element_type=F32)


def _dot_nt(a, b):
    return lax.dot_general(a, b, (((1,), (1,)), ((), ())), preferred_element_type=F32)


def _split2(x):
    hi = x.astype(BF16)
    return hi, (x - hi.astype(F32)).astype(BF16)


def _rms(x, g):
    return x * lax.rsqrt(jnp.mean(x * x, axis=-1, keepdims=True) + EPS) * g


def _sigmoid(x):
    return 1.0 / (1.0 + jnp.exp(-x))


def _full_spec(a):
    nd = a.ndim
    return pl.BlockSpec(a.shape, lambda *_: (0,) * nd, pipeline_mode=pl.Buffered(1))


def _inproj_kernel(x_ref, g_ref, wab_ref, wcg_ref, wd_ref, fc_ref, fs_ref, gbr_ref,
                   qa_ref, ka_ref, va_ref, zr_ref, zi_ref, uc_ref, qd_ref, vd_ref, od_ref, grow_ref, kdt_ref):
    xn = _rms(x_ref[0], g_ref[...]).astype(BF16)
    cg = _dot(xn, wcg_ref[...])
    gates = jnp.transpose(cg[:, 3 * GROUP_W:3 * GROUP_W + LANES])[:N_GATES, :] + gbr_ref[...]
    stats = []
    for rev in range(2):
        li = gates[2 * N_HEADS * rev:2 * N_HEADS * rev + N_HEADS, :]
        lf = jax.nn.log_sigmoid(gates[2 * N_HEADS * rev + N_HEADS:2 * N_HEADS * (rev + 1), :])
        b = _lane_scan(lf, ML_CHUNK, rev == 1, jnp.add)
        stats += [b, li - b, _lane_scan(li - b, ML_CHUNK, rev == 1, jnp.maximum)]
    grow_ref[0] = jnp.concatenate(stats, axis=0)
    blocks = [slice(j * LANES, (j + 1) * LANES) for j in range(x_ref.shape[1] // LANES)]
    for j, blk in enumerate(blocks):
        uc_ref[0, :, j, :] = jnp.transpose(cg[blk, :3 * GROUP_W])
    ab = _dot(xn, wab_ref[...])
    ub = ab[:, 3 * GROUP_W:].astype(BF16)
    zr = _dot(ub, fc_ref[...])
    zi = _dot(ub, fs_ref[...])
    for j, blk in enumerate(blocks):
        zr_ref[0, :, j, :] = jnp.transpose(zr[blk, :])
        zi_ref[0, :, j, :] = jnp.transpose(zi[blk, :])
    qa_ref[0] = (ab[:, :GROUP_W] * QK_SCALE).astype(BF16)
    ka_ref[0] = ab[:, GROUP_W:2 * GROUP_W].astype(BF16)
    va_ref[0] = ab[:, 2 * GROUP_W:3 * GROUP_W].astype(BF16)
    d = _dot(xn, wd_ref[...])
    for blk in blocks:
        kdt_ref[0, :, blk] = jnp.transpose(d[blk, GROUP_W:2 * GROUP_W] * QK_SCALE).astype(BF16)
    qd_ref[0] = d[:, :GROUP_W].astype(BF16)
    vd_ref[0] = d[:, 2 * GROUP_W:3 * GROUP_W].astype(BF16)
    od_ref[0] = d[:, 3 * GROUP_W:]


def _inproj(x, g, w_in, gate_b, fcs):
    bsz, seq, _ = x.shape
    tm = min(INPROJ_TILE, seq)
    wb16 = w_in.astype(BF16)
    wab = wb16[:, :4 * GROUP_W]
    wc = wb16[:, 4 * GROUP_W:7 * GROUP_W]
    wd = wb16[:, 7 * GROUP_W:11 * GROUP_W]
    wg = wb16[:, 11 * GROUP_W:11 * GROUP_W + N_GATES]
    wcg = jnp.concatenate([wc, jnp.pad(wg, ((0, 0), (0, GROUP_W - N_GATES)))], axis=1)
    gbr = gate_b.astype(F32).reshape(N_GATES, 1)
    fc, fs = fcs
    weights = (g.astype(F32).reshape(1, D_MODEL), wab, wcg, wd, fc, fs, gbr)

    def tok(width, dtype):
        return jax.ShapeDtypeStruct((bsz, seq, width), dtype), pl.BlockSpec((1, tm, width), lambda b, t: (b, t, 0))

    def chan(height, dtype):
        return jax.ShapeDtypeStruct((bsz, height, seq), dtype), pl.BlockSpec((1, height, tm), lambda b, t: (b, 0, t))

    def chan_tiles(height):
        return (jax.ShapeDtypeStruct((bsz, height, seq // LANES, LANES), F32),
                pl.BlockSpec((1, height, tm // LANES, LANES), lambda b, t: (b, 0, t, 0)))

    outs = [tok(GROUP_W, BF16), tok(GROUP_W, BF16), tok(GROUP_W, BF16),
            chan_tiles(GROUP_W), chan_tiles(GROUP_W), chan_tiles(3 * GROUP_W),
            tok(GROUP_W, BF16), tok(GROUP_W, BF16), tok(GROUP_W, F32),
            chan(ML_STAT_ROWS, F32), chan(GROUP_W, BF16)]
    return pl.pallas_call(
        _inproj_kernel,
        grid=(bsz, seq // tm),
        in_specs=[pl.BlockSpec((1, tm, D_MODEL), lambda b, t: (b, t, 0))] + [_full_spec(w) for w in weights],
        out_specs=[s for _, s in outs],
        out_shape=[s for s, _ in outs],
        compiler_params=_cparams("parallel", "parallel"),
        name="inproj",
    )(x, *weights)


def _na_kernel(q_ref, k_ref, v_ref, bias_ref, o_ref, *, rows, rows_per_step):
    i = pl.program_id(1)
    lane_head = lax.broadcasted_iota(jnp.int32, (GRID_W, GROUP_W), 1) // HEAD_DIM
    masks = [lane_head == h for h in range(N_HEADS)]
    for j in range(rows_per_step):
        r = i * rows_per_step + j
        kr0 = jnp.clip(r - KH // 2, 0, rows - KH)
        case = r - kr0
        ks = pl.multiple_of(kr0 * GRID_W, GRID_W)
        q = q_ref[0, j * GRID_W:(j + 1) * GRID_W, :]
        qs = jnp.concatenate([jnp.where(m, q, jnp.zeros_like(q)) for m in masks], axis=0)
        kw = k_ref[0, pl.ds(ks, KH * GRID_W), :]
        vw = v_ref[0, pl.ds(ks, KH * GRID_W), :]
        s = _dot_nt(qs, kw) + bias_ref[case]
        p = jnp.exp(s - jnp.max(s, axis=-1, keepdims=True))
        den = jnp.sum(p, axis=-1, keepdims=True)
        o = _dot(p.astype(BF16), vw) * (1.0 / den)
        out = jnp.zeros((GRID_W, GROUP_W), F32)
        for h in range(N_HEADS):
            out = out + jnp.where(masks[h], o[h * GRID_W:(h + 1) * GRID_W, :], 0.0)
        o_ref[0, j * GRID_W:(j + 1) * GRID_W, :] = out


def _na_bias_table(rpb):
    c = np.arange(GRID_W)
    kc = np.arange(GRID_W)
    kc0 = np.clip(c - KW // 2, 0, GRID_W - KW)
    valid = (kc[None, :] >= kc0[:, None]) & (kc[None, :] < kc0[:, None] + KW)
    dc = np.clip(kc[None, :] - c[:, None] + KW - 1, 0, 2 * KW - 2)
    case = np.arange(KH)
    j = np.arange(KH)
    dr = j[None, :] - case[:, None] + KH - 1
    row_sel = np.zeros((KH * KH, 2 * KH - 1), np.float32)
    row_sel[np.arange(KH * KH), dr.reshape(-1)] = 1.0
    col_sel = np.zeros((2 * KW - 1, GRID_W * GRID_W), np.float32)
    col_sel[dc.reshape(-1), np.arange(GRID_W * GRID_W)] = 1.0
    hp = lax.Precision.HIGHEST
    tab = jnp.einsum("rd,hde->hre", jnp.asarray(row_sel), rpb.astype(F32), precision=hp)
    tab = jnp.einsum("hre,ex->hrx", tab, jnp.asarray(col_sel), precision=hp)
    tab = tab.reshape(N_HEADS, KH, KH, GRID_W, GRID_W).transpose(1, 0, 3, 2, 4)
    tab = jnp.where(jnp.asarray(valid)[None, None, :, None, :], tab, NEG)
    return tab.reshape(KH, N_HEADS * GRID_W, KH * GRID_W)


def _na(qa, ka, va, bias):
    bsz, seq, _ = qa.shape
    rows = seq // GRID_W
    assert rows >= KH and rows % NA_ROWS_PER_STEP == 0
    rb = NA_ROWS_PER_STEP
    return pl.pallas_call(
        functools.partial(_na_kernel, rows=rows, rows_per_step=rb),
        grid=(bsz, rows // rb),
        in_specs=[pl.BlockSpec((1, rb * GRID_W, GROUP_W), lambda b, i: (b, i, 0)),
                  pl.BlockSpec((1, seq, GROUP_W), lambda b, i: (b, 0, 0)),
                  pl.BlockSpec((1, seq, GROUP_W), lambda b, i: (b, 0, 0)),
                  _full_spec(bias)],
        out_specs=pl.BlockSpec((1, rb * GRID_W, GROUP_W), lambda b, i: (b, i, 0)),
        out_shape=jax.ShapeDtypeStruct((bsz, seq, GROUP_W), F32),
        compiler_params=_cparams("parallel", "arbitrary"),
        name="nbr_attn",
    )(qa, ka, va, bias)


def _cs(num, den):
    ang = 2.0 * np.pi * (np.asarray(num, np.float64) % den) / den
    return np.cos(ang), np.sin(ang)


def _hilo(m):
    m32 = jnp.asarray(m, F32)
    hi = m32.astype(BF16)
    return hi, (m32 - hi.astype(F32)).astype(BF16)


def _fnet_tables(seq):
    n1f = seq // LANES
    c, s = _cs(np.outer(np.arange(HEAD_DIM), np.arange(HEAD_DIM)), HEAD_DIM)
    norm = 1.0 / math.sqrt(HEAD_DIM * seq)
    eye = np.eye(GROUP_W // HEAD_DIM)
    fc = np.kron(eye, c) * norm
    fs = np.kron(eye, -s) * norm
    c1, s1 = _cs(np.outer(np.arange(n1f), np.arange(n1f)), n1f)
    m1 = np.block([[c1, -s1], [s1, c1]])
    ct, st = _cs(np.outer(np.arange(LANES), np.arange(n1f)), seq)
    t1 = np.concatenate([ct, ct], axis=1)
    t2 = np.concatenate([st, -st], axis=1)
    c2, s2 = _cs(np.outer(np.arange(LANES), np.arange(LANES)), LANES)
    g2 = np.concatenate([c2, s2], axis=0)
    q = LANES // n1f
    g2 = np.stack([g2[:, a::q] for a in range(q)], axis=0)
    return dict(fc=jnp.asarray(fc, F32).astype(BF16), fs=jnp.asarray(fs, F32).astype(BF16),
                m1=jnp.asarray(m1, F32).astype(BF16), t1=jnp.asarray(t1, F32), t2=jnp.asarray(t2, F32),
                g2=jnp.asarray(g2, F32).astype(BF16))


def _hyena_tables(seq):
    n = 2 * seq
    n1 = n // LANES
    c1, s1 = _cs(np.outer(np.arange(n1), np.arange(n1)), n1)
    f1 = np.concatenate([c1, -s1], axis=1)
    ct, st = _cs(np.outer(np.arange(LANES), np.arange(n1)), n)
    c2, s2 = _cs(np.outer(np.arange(LANES), np.arange(LANES)), LANES)
    g2 = np.block([[c2, -s2], [s2, c2]])
    gi2 = np.block([[c2, s2], [-s2, c2]])
    gi1 = np.concatenate([c1, -s1], axis=0) / n
    gi1[:, n1 // 2:] = 0.0
    return dict(f1=_hilo(f1), tc=jnp.asarray(ct, F32), ts=jnp.asarray(st, F32),
                tct=jnp.asarray(ct.T, F32), tst=jnp.asarray(st.T, F32),
                g2=_hilo(g2), gi2=_hilo(gi2), gi1=_hilo(gi1))


def _mm(a, tab, precise):
    hi, lo = tab
    if not precise:
        return _dot(a.astype(BF16), hi)
    a_hi, a_lo = _split2(a)
    return _dot(a_hi, hi) + (_dot(a_lo, hi) + _dot(a_hi, lo))


def _fnet_kernel(zr_ref, zi_ref, m1_ref, t1_ref, t2_ref, g2_ref, o_ref):
    cb, n1f, n2 = zr_ref.shape[1:]
    w = 2 * n1f
    z = jnp.swapaxes(jnp.concatenate([zr_ref[0], zi_ref[0]], axis=1), 1, 2)
    a = _dot(z.reshape(cb * n2, w).astype(BF16), m1_ref[...])
    sw = pltpu.roll(a, n1f, 1).reshape(cb, n2, w)
    a = a.reshape(cb, n2, w) * t1_ref[...] + sw * t2_ref[...]
    at = jnp.swapaxes(a, 1, 2)
    op = jnp.concatenate([at[:, :n1f, :], at[:, n1f:, :]], axis=-1)
    op = op.reshape(cb * n1f, 2 * n2).astype(BF16)
    xt = jnp.concatenate([_dot(op, g2_ref[g]).reshape(cb, n1f, n1f) for g in range(g2_ref.shape[0])], axis=1)
    o_ref[0] = jnp.swapaxes(xt, 1, 2).reshape(cb, n1f * n2)


def _fnet(zr, zi, tabs):
    bsz, ch, n1f, _ = zr.shape
    cb = FN_CH_BLOCK
    consts = (tabs["m1"], tabs["t1"], tabs["t2"], tabs["g2"])
    z_spec = pl.BlockSpec((1, cb, n1f, LANES), lambda b, c: (b, c, 0, 0))
    return pl.pallas_call(
        _fnet_kernel,
        grid=(bsz, ch // cb),
        in_specs=[z_spec, z_spec] + [_full_spec(t) for t in consts],
        out_specs=pl.BlockSpec((1, cb, n1f * LANES), lambda b, c: (b, c, 0)),
        out_shape=jax.ShapeDtypeStruct((bsz, ch, n1f * LANES), F32),
        compiler_params=_cparams("parallel", "parallel"),
        name="fourier_mix",
    )(zr, zi, *consts)


def _filter_mlp_kernel(feat_ref, t_ref, w1_ref, b1_ref, f0_ref, w2_ref, b2_ref, f1_ref, w3_ref, dec_ref, o_ref):
    def dot3(a, w):
        return _mm(a, _split2(w), True)

    h = jnp.sin(f0_ref[...] * (dot3(feat_ref[...], w1_ref[...]) + b1_ref[...]))
    h = jnp.sin(f1_ref[...] * (dot3(h, w2_ref[...]) + b2_ref[...]))
    h = dot3(h, w3_ref[0])
    o_ref[...] = h * (jnp.exp(-t_ref[:, 0:1] * dec_ref[0]) * t_ref[:, 1:2])


def _hyena_filter_taps(seq, w1, b1, freq, w2, b2, w3, decay):
    ch = decay.shape[-1]
    r = jnp.arange(2 * seq)
    s = jnp.where(r < seq, r, 2 * seq - r).astype(F32)
    t = s / max(seq - 1, 1)
    ang = (2.0 * math.pi / seq) * s
    bands = jnp.linspace(1e-4, HY_BANDS - 1, HY_BANDS, dtype=F32)
    fb = ang[:, None] * bands[None, :]
    feats = jnp.concatenate([t[:, None], jnp.cos(fb), -jnp.sin(fb)], axis=-1)
    t_keep = jnp.stack([t, (r != seq).astype(F32)], axis=1)
    filt = w1.shape[1]
    pad_c = LANES - filt
    feats = jnp.pad(feats, ((0, 0), (0, LANES - HY_EMB)))
    w1p = jnp.pad(w1.astype(F32), ((0, LANES - HY_EMB), (0, pad_c)))
    w2p = jnp.pad(w2.astype(F32), ((0, pad_c), (0, pad_c)))
    w3s = w3.astype(F32).reshape(filt, HY_ORDER, 2, ch).transpose(2, 0, 1, 3).reshape(2, filt, HY_ORDER * ch)
    w3s = jnp.pad(w3s, ((0, 0), (0, pad_c), (0, 0)))
    decs = decay.astype(F32).transpose(1, 0, 2).reshape(2, 1, HY_ORDER * ch)
    row = lambda v: jnp.pad(v.astype(F32), (0, pad_c)).reshape(1, LANES)
    n_out = HY_ORDER * ch
    tl = min(FILT_TILE, seq)
    half = seq // tl
    side = lambda i: (i // half, 0, 0)
    args = (feats, t_keep, w1p, row(b1), row(freq[0]), w2p, row(b2), row(freq[1]), w3s, decs)
    specs = [pl.BlockSpec((tl, LANES), lambda i: (i, 0)), pl.BlockSpec((tl, 2), lambda i: (i, 0))]
    specs += [_full_spec(a) for a in args[2:8]]
    specs += [pl.BlockSpec((1, LANES, n_out), side), pl.BlockSpec((1, 1, n_out), side)]
    return pl.pallas_call(
        _filter_mlp_kernel,
        grid=(2 * half,),
        in_specs=specs,
        out_specs=pl.BlockSpec((tl, n_out), lambda i: (i, 0)),
        out_shape=jax.ShapeDtypeStruct((2 * seq, n_out), F32),
        compiler_params=_cparams("parallel"),
        name="hyena_filter_mlp",
    )(*args)


def _fft_fwd(u, f1, tc, ts, g2, precise):
    cb, n2, n1_in = u.shape
    n1 = f1[0].shape[1] // 2
    a = _mm(u.reshape(cb * n2, n1_in), f1, precise)
    ar = a[:, :n1].reshape(cb, n2, n1)
    ai = a[:, n1:].reshape(cb, n2, n1)
    br = ar * tc + ai * ts
    bi = ai * tc - ar * ts
    op = jnp.concatenate([jnp.swapaxes(br, 1, 2), jnp.swapaxes(bi, 1, 2)], axis=-1)
    return _mm(op.reshape(cb * n1, 2 * n2), g2, precise)


def _spectrum_kernel(h_ref, f1h, f1l, tc_ref, ts_ref, g2h, g2l, o_ref):
    x = _fft_fwd(h_ref[...], (f1h[...], f1l[...]), tc_ref[...], ts_ref[...], (g2h[...], g2l[...]), True)
    o_ref[...] = x.reshape(o_ref.shape)


def _hyena_spectra(full, tabs):
    n, nch = full.shape
    n1 = n // LANES
    hcm = full.reshape(n1, LANES, nch).transpose(2, 1, 0)
    cb = HY_CH_BLOCK
    consts = (*tabs["f1"], tabs["tc"], tabs["ts"], *tabs["g2"])
    return pl.pallas_call(
        _spectrum_kernel,
        grid=(nch // cb,),
        in_specs=[pl.BlockSpec((cb, LANES, n1), lambda c: (c, 0, 0))] + [_full_spec(t) for t in consts],
        out_specs=pl.BlockSpec((cb, n1, 2 * LANES), lambda c: (c, 0, 0)),
        out_shape=jax.ShapeDtypeStruct((nch, n1, 2 * LANES), F32),
        compiler_params=_cparams("parallel"),
        name="hyena_filter_spectrum",
    )(hcm, *consts)


def _hyena_kernel(uv_ref, u1_ref, u2_ref, cv_ref, c1_ref, c2_ref, skip_ref, k0_ref, k1_ref,
                  f1_ref, tc_ref, ts_ref, g2_ref, gi2_ref, tct_ref, tst_ref, gi1_ref, o_ref):
    cb, h1, n2 = uv_ref.shape[1:]
    n1 = k0_ref.shape[1]
    rows = cb * h1
    lane = lax.broadcasted_iota(jnp.int32, (rows, n2), 1)
    row_n1 = lax.broadcasted_iota(jnp.int32, (rows, n2), 0) % h1
    lane_first, lane_last = lane == 0, lane == n2 - 1
    seq_first = lane_first & (row_n1 == 0)
    seq_last = lane_last & (row_n1 == h1 - 1)

    def short_conv(u_ref, c_ref):
        u = u_ref[0]
        u2 = u.reshape(rows, n2)
        r = pltpu.roll(u2, 1, 1)
        prev = jnp.where(seq_first, 0.0, jnp.where(lane_first, pltpu.roll(r, 1, 0), r))
        r = pltpu.roll(u2, n2 - 1, 1)
        nxt = jnp.where(seq_last, 0.0, jnp.where(lane_last, pltpu.roll(r, rows - 1, 0), r))
        c = c_ref[...]
        return (prev.reshape(cb, h1, n2) * c[:, 0:1, :] + u * c[:, 1:2, :] + nxt.reshape(cb, h1, n2) * c[:, 2:3, :]
                + c[:, 3:4, :])

    parts = [slice(p * cb // HY_PARTS, (p + 1) * cb // HY_PARTS) for p in range(HY_PARTS)]
    pc = cb // HY_PARTS
    tc, ts, tct, tst = tc_ref[...], ts_ref[...], tct_ref[...], tst_ref[...]

    def stage(xs, w_ref):
        return [_dot(x.astype(BF16), w_ref[...]) for x in xs]

    def fftconv(u, k_ref):
        a = stage([jnp.swapaxes(u[p], 1, 2).reshape(pc * n2, h1) for p in parts], f1_ref)
        ops = []
        for x in a:
            ar, ai = x[:, :n1].reshape(pc, n2, n1), x[:, n1:].reshape(pc, n2, n1)
            br, bi = ar * tc + ai * ts, ai * tc - ar * ts
            ops.append(jnp.concatenate([jnp.swapaxes(br, 1, 2), jnp.swapaxes(bi, 1, 2)], axis=-1)
                       .reshape(pc * n1, 2 * n2))
        xs = stage(ops, g2_ref)
        ys = []
        for p, x in zip(parts, xs):
            kf = k_ref[p].reshape(pc * n1, 2 * n2)
            xr, xi, kr, ki = x[:, :n2], x[:, n2:], kf[:, :n2], kf[:, n2:]
            ys.append(jnp.concatenate([xr * kr - xi * ki, xr * ki + xi * kr], axis=-1))
        ds = stage(ys, gi2_ref)
        es = []
        for x in ds:
            dr, di = x[:, :n2].reshape(pc, n1, n2), x[:, n2:].reshape(pc, n1, n2)
            er, ei = dr * tct - di * tst, dr * tst + di * tct
            es.append(jnp.concatenate([jnp.swapaxes(er, 1, 2), jnp.swapaxes(ei, 1, 2)], axis=-1)
                      .reshape(pc * n2, 2 * n1))
        outs = stage(es, gi1_ref)
        return jnp.concatenate([jnp.swapaxes(o.reshape(pc, n2, h1), 1, 2) for o in outs], axis=0)

    skip = skip_ref[...]
    v = short_conv(uv_ref, cv_ref)
    conv1 = fftconv(v, k0_ref)
    y1 = short_conv(u1_ref, c1_ref) * (conv1 + v * skip[:, 0:1, :])
    conv2 = fftconv(y1, k1_ref)
    y2 = short_conv(u2_ref, c2_ref) * (conv2 + y1 * skip[:, 1:2, :])
    o_ref[0] = y2.reshape(cb, h1 * n2)


def _hyena(u, conv_w, conv_b, skip, spectra, tabs):
    bsz, _, h1, _ = u.shape
    ch = GROUP_W
    n1 = 2 * h1
    cw =jnp.concatenate([conv_w.astype(F32), conv_b.astype(F32)[None]], axis=0).T
    cw = jnp.broadcast_to(cw[:, :, None], (3 * ch, 4, LANES))
    sk = jnp.broadcast_to(skip.astype(F32).T[:, :, None], (ch, HY_ORDER, LANES))
    cb = HY_CH_BLOCK
    nblk = ch // cb
    consts = (tabs["f1"][0][:h1], tabs["tc"], tabs["ts"], tabs["g2"][0], tabs["gi2"][0], tabs["tct"], tabs["tst"],
              tabs["gi1"][0][:, :h1])
    u_spec = lambda g: pl.BlockSpec((1, cb, h1, LANES), lambda c, b, g=g: (b, c + g * nblk, 0, 0))
    c_spec = lambda g: pl.BlockSpec((cb, 4, LANES), lambda c, b, g=g: (c + g * nblk, 0, 0))
    k_spec = lambda o: pl.BlockSpec((cb, n1, 2 * LANES), lambda c, b, o=o: (c + o * nblk, 0, 0))
    y = pl.pallas_call(
        _hyena_kernel,
        grid=(nblk, bsz),
        in_specs=[u_spec(0), u_spec(1), u_spec(2), c_spec(0), c_spec(1), c_spec(2),
                  pl.BlockSpec((cb, HY_ORDER, LANES), lambda c, b: (c, 0, 0)), k_spec(0), k_spec(1)]
                 + [_full_spec(t) for t in consts],
        out_specs=pl.BlockSpec((1, cb, h1 * LANES), lambda c, b: (b, c, 0)),
        out_shape=jax.ShapeDtypeStruct((bsz, ch, h1 * LANES), F32),
        compiler_params=_cparams("parallel", "parallel"),
        name="hyena",
    )(u, u, u, cw, cw, cw, sk, spectra, spectra, *consts)
    return y


def _lane_scan(x, c, reverse, op):
    n = x.shape[-1]
    ax = x.ndim - 1
    pos = lax.broadcasted_iota(jnp.int32, x.shape, ax) % c
    k = 1
    while k < c:
        if reverse:
            x = jnp.where(pos < c - k, op(x, pltpu.roll(x, n - k, ax)), x)
        else:
            x = jnp.where(pos >= k, op(x, pltpu.roll(x, k, ax)), x)
        k *= 2
    return x


ML_STAT_LANES = 16


def _mlstm_direction(d, q, v, kt, gr, s, m0, chunks):
    c = ML_CHUNK
    n = chunks * c
    nh = N_HEADS
    reverse = d == 1
    b, row, cmax = (gr[(3 * d + i) * nh:(3 * d + i + 1) * nh, :] for i in range(3))
    order = list(range(chunks - 1, -1, -1) if reverse else range(chunks))

    m_in, m_top, d_old = {}, {}, {}
    m = m0
    for k in order:
        edge = k * c if reverse else (k + 1) * c - 1
        m_in[k] = m
        m_top[k] = jnp.maximum(m, cmax[:, edge:edge + 1])
        d_old[k] = jnp.exp(m - m_top[k])
        m = b[:, edge:edge + 1] + m_top[k]
    m_in_row = jnp.concatenate([jnp.broadcast_to(m_in[k], (nh, c)) for k in range(chunks)], axis=1)
    m_top_row = jnp.concatenate([jnp.broadcast_to(m_top[k], (nh, c)) for k in range(chunks)], axis=1)
    mx = jnp.maximum(m_in_row, cmax)
    wi = jnp.exp(m_in_row - mx)
    einv = jnp.exp(-b - mx)
    w = jnp.exp(row - m_top_row)
    zeros4 = jnp.zeros_like(mx)
    stat_a = jnp.transpose(jnp.concatenate([mx, wi, zeros4, zeros4], axis=0))
    stat_b = jnp.transpose(jnp.concatenate([zeros4, einv, zeros4, zeros4], axis=0))

    jj = lax.broadcasted_iota(jnp.int32, (c, c), 0)
    ss = lax.broadcasted_iota(jnp.int32, (c, c), 1)
    causal = (ss >= jj) if reverse else (ss <= jj)
    lane_head = lax.broadcasted_iota(jnp.int32, (c, GROUP_W), 1) // HEAD_DIM
    row_head = lax.broadcasted_iota(jnp.int32, (GROUP_W, c), 0) // HEAD_DIM
    stat_lane = lax.broadcasted_iota(jnp.int32, (c, ML_STAT_LANES), 1)
    head_lanes = (stat_lane >= nh) & (stat_lane < 2 * nh)
    ones_bd = (lax.broadcasted_iota(jnp.int32, (nh * c, LANES), 0) // c + nh
               == lax.broadcasted_iota(jnp.int32, (nh * c, LANES), 1)).astype(BF16)
    expand = (lax.broadcasted_iota(jnp.int32, (ML_STAT_LANES, GROUP_W), 0) - nh
              == lax.broadcasted_iota(jnp.int32, (ML_STAT_LANES, GROUP_W), 1) // HEAD_DIM).astype(BF16)
    s_rh = lax.broadcasted_iota(jnp.int32, (GROUP_W, GROUP_W + LANES), 0) // HEAD_DIM
    s_col = lax.broadcasted_iota(jnp.int32, (GROUP_W, GROUP_W + LANES), 1)
    s_mask = jnp.where(s_col < GROUP_W, s_col // HEAD_DIM, s_col - GROUP_W - nh) == s_rh
    ones_cols = jnp.ones((c, LANES), BF16)

    outs = {}
    for k in order:
        tok = slice(k * c, (k + 1) * c)
        qc, vc, ktc = q[tok, :], v[tok, :], kt[:, tok]
        k_bd = jnp.concatenate([jnp.where(row_head == h, ktc, jnp.zeros_like(ktc)) for h in range(nh)], axis=1)
        v_bd = jnp.concatenate([jnp.where(lane_head == h, vc, jnp.zeros_like(vc)) for h in range(nh)], axis=0)
        qk = _dot(qc, k_bd)
        sa = stat_a[tok, :]
        p = jnp.concatenate(
            [jnp.exp(jnp.where(causal, row[h:h + 1, tok] - sa[:, h:h + 1], NEG)) * qk[:, h * c:(h + 1) * c]
             for h in range(nh)], axis=1).astype(BF16)
        pv = _dot(p, v_bd)
        p_sum = _dot(p, ones_bd)[:, :ML_STAT_LANES]
        qs = _dot(qc, s.astype(BF16))
        den = sa * qs[:, GROUP_W:GROUP_W + ML_STAT_LANES] + p_sum
        rden = jnp.where(head_lanes, 1.0 / jnp.maximum(jnp.abs(den), stat_b[tok, :]), 0.0)
        wi_c = jnp.where(head_lanes, sa, 0.0)
        ex = _dot(jnp.concatenate([wi_c, rden], axis=0).astype(BF16), expand)
        outs[k] = (ex[:c] * qs[:, :GROUP_W] + pv) * ex[c:]

        w_full = jnp.concatenate([jnp.broadcast_to(w[h:h + 1, tok], (HEAD_DIM, c)) for h in range(nh)], axis=0)
        d_full = jnp.concatenate([jnp.broadcast_to(d_old[k][h:h + 1, :], (HEAD_DIM, 1)) for h in range(nh)], axis=0)
        ktw = (ktc.astype(F32) * w_full).astype(BF16)
        s_loc = _dot(ktw, jnp.concatenate([vc, ones_cols], axis=1))
        s = d_full * s + jnp.where(s_mask, s_loc, 0.0)
    return outs, s, m


def _mlstm_kernel(qf, vf, ktf, grf, qb, vb, ktb, grb, hf_ref, hb_ref, s_scr, m_scr, *, chunks):
    @pl.when(pl.program_id(1) == 0)
    def _():
        s_scr[...] = jnp.zeros_like(s_scr)
        m_scr[...] = jnp.zeros_like(m_scr)

    c = ML_CHUNK
    results = []
    for d, (q_ref, v_ref, kt_ref, gr_ref) in enumerate(((qf, vf, ktf, grf), (qb, vb, ktb, grb))):
        m0 = m_scr[d * 8:d * 8 + N_HEADS, :][:, :1]
        results.append(_mlstm_direction(d, q_ref[0], v_ref[0], kt_ref[0], gr_ref[0], s_scr[d], m0, chunks))
    for d, out_ref in enumerate((hf_ref, hb_ref)):
        outs, s, m = results[d]
        for k in range(chunks):
            out_ref[0, k * c:(k + 1) * c, :] = outs[k]
        s_scr[d] = s
        m_scr[d * 8:d * 8 + N_HEADS, :] = jnp.broadcast_to(m, (N_HEADS, LANES))


def _mlstm(qd, vd, kdt, grow):
    bsz, seq, _ = qd.shape
    g = min(ML_CHUNKS_PER_STEP, seq // ML_CHUNK)
    blk = g * ML_CHUNK
    nb = seq // blk
    fwd = lambda b, i: (b, i, 0)
    bwd = lambda b, i: (b, nb - 1 - i, 0)
    fwd_t = lambda b, i: (b, 0, i)
    bwd_t = lambda b, i: (b, 0, nb - 1 - i)

    def specs(tok, chan):
        return [pl.BlockSpec((1, blk, GROUP_W), tok), pl.BlockSpec((1, blk, GROUP_W), tok),
                pl.BlockSpec((1, GROUP_W, blk), chan), pl.BlockSpec((1, ML_STAT_ROWS, blk), chan)]

    args = (qd, vd, kdt, grow)
    return pl.pallas_call(
        functools.partial(_mlstm_kernel, chunks=g),
        grid=(bsz, nb),
        in_specs=specs(fwd, fwd_t) + specs(bwd, bwd_t),
        out_specs=[pl.BlockSpec((1, blk, GROUP_W), fwd), pl.BlockSpec((1, blk, GROUP_W), bwd)],
        out_shape=[jax.ShapeDtypeStruct((bsz, seq, GROUP_W), F32)] * 2,
        scratch_shapes=[pltpu.VMEM((2, GROUP_W, GROUP_W + LANES), F32), pltpu.VMEM((16, LANES), F32)],
        compiler_params=_cparams("parallel", "arbitrary"),
        name="mlstm",
    )(*args, *args)


def _post_kernel(x_ref, ya_ref, f_ref, yc_ref, hf_ref, hb_ref, od_ref, p_ref,
                 wfn_ref, onorm_ref, gsum_ref, gbc_ref, wout_ref, nffn_ref, wgate_ref, wup_ref, wdown_ref,
                 pnorm_ref, wpg_ref, wpp_ref, fnorm_ref, o_ref, *, final):
    yb = _dot(jnp.transpose(f_ref[0]).astype(BF16), wfn_ref[...])
    yd = _sigmoid(od_ref[0]) * (hf_ref[0] + hb_ref[0])
    y = jnp.concatenate([ya_ref[0], yb, jnp.transpose(yc_ref[0]), yd], axis=-1)
    ss = _dot((y * y).astype(BF16), gsum_ref[...])
    rb = _dot(lax.rsqrt(ss * (1.0 / HEAD_DIM) + EPS).astype(BF16), gbc_ref[...])
    x = x_ref[0] + _dot((y * rb * onorm_ref[...]).astype(BF16), wout_ref[...])
    hn = _rms(x, nffn_ref[...]).astype(BF16)
    acc = jnp.zeros_like(x)
    for c in range(D_FF // FF_CHUNK):
        sl = slice(c * FF_CHUNK, (c + 1) * FF_CHUNK)
        g = _dot(hn, wgate_ref[:, sl])
        u = _dot(hn, wup_ref[:, sl])
        acc = acc + _dot((g * _sigmoid(g) * u).astype(BF16), wdown_ref[sl, :])
    x = x + acc
    gate = _sigmoid(_dot(_rms(x, pnorm_ref[...]).astype(BF16), wpg_ref[...]))
    x = x + gate * _dot(p_ref[0].astype(BF16), wpp_ref[...])
    if final:
        x = _rms(x, fnorm_ref[...])
    o_ref[0] = x


def _post(x, ya, f, yc, hf, hb, od, p, lw, final_norm, final):
    bsz, seq, _ = x.shape
    tm = min(TOKEN_TILE, seq)
    ind = np.zeros((D_MODEL, LANES), np.float32)
    ind[np.arange(D_MODEL), np.arange(D_MODEL) // HEAD_DIM] = 1.0
    gsum = jnp.asarray(ind, BF16)
    gbc = jnp.asarray(ind.T, BF16)
    fw = lw["fnet_w"].astype(F32)
    wfn = jax.scipy.linalg.block_diag(*[fw[g] for g in range(fw.shape[0])]).astype(BF16)
    row = lambda v: v.astype(F32).reshape(1, D_MODEL)
    weights = (wfn, row(lw["out_norm"]), gsum, gbc, lw["w_out"].astype(BF16), row(lw["norm_ffn"]),
               lw["w_gate"].astype(BF16), lw["w_up"].astype(BF16), lw["w_down"].astype(BF16),
               row(lw["ple_norm"]), lw["w_ple_gate"].astype(BF16), lw["w_ple_proj"].astype(BF16), row(final_norm))
    tok = lambda width: pl.BlockSpec((1, tm, width), lambda b, t: (b, t, 0))
    chan = pl.BlockSpec((1, GROUP_W, tm), lambda b, t: (b, 0, t))
    return pl.pallas_call(
        functools.partial(_post_kernel, final=final),
        grid=(bsz, seq // tm),
        in_specs=[tok(D_MODEL), tok(GROUP_W), chan, chan] + [tok(GROUP_W)] * 3 + [tok(PLE_DIM)]
                 + [_full_spec(w) for w in weights],
        out_specs=tok(D_MODEL),
        out_shape=jax.ShapeDtypeStruct((bsz, seq, D_MODEL), F32),
        compiler_params=_cparams("parallel", "parallel"),
        name="post",
    )(x, ya, f, yc, hf, hb, od, p, *weights)


def _layer_consts(lw, seq, hy_tabs):
    full = _hyena_filter_taps(seq, lw["hy_w1"], lw["hy_b1"], lw["hy_freq"], lw["hy_w2"], lw["hy_b2"], lw["hy_w3"],
                              lw["hy_decay"])
    return dict(spectra=_hyena_spectra(full, hy_tabs), na_bias=_na_bias_table(lw["attn_rpb"]))


def _trunk(x, p, layers, consts, final_norm, fn_tabs, hy_tabs):
    for i, (lw, lc) in enumerate(zip(layers, consts)):
        (qa, ka, va, zr, zi, uc, qd, vd, od, grow, kdt) = _inproj(
            x, lw["norm_mix"], lw["w_in"], lw["ml_gate_b"], (fn_tabs["fc"], fn_tabs["fs"]))
        ya = _na(qa, ka, va, lc["na_bias"])
        f = _fnet(zr, zi, fn_tabs)
        yc = _hyena(uc, lw["hy_conv_w"], lw["hy_conv_b"], lw["hy_skip"], lc["spectra"], hy_tabs)
        hf, hb = _mlstm(qd, vd, kdt, grow)
        x = _post(x, ya, f, yc, hf, hb, od, p[i], lw, final_norm, final=(i == len(layers) - 1))
    return x


_LAYER_KEYS = ("norm_mix", "w_in", "attn_rpb", "fnet_w", "hy_conv_w", "hy_conv_b", "hy_w1", "hy_b1", "hy_freq",
               "hy_w2", "hy_b2", "hy_w3", "hy_decay", "hy_skip", "ml_gate_b", "out_norm", "w_out", "norm_ffn",
               "w_gate", "w_up", "w_down", "ple_norm", "w_ple_gate", "w_ple_proj")


def kernel(x_prompt, x_sample, p_prompt, p_sample, norm_mix, w_in, attn_rpb, fnet_w, hy_conv_w, hy_conv_b, hy_w1,
           hy_b1, hy_freq, hy_w2, hy_b2, hy_w3, hy_decay, hy_skip, ml_gate_b, out_norm, w_out, norm_ffn, w_gate,
           w_up, w_down, ple_norm, w_ple_gate, w_ple_proj, final_norm):
    stacked = dict(zip(_LAYER_KEYS, (norm_mix, w_in, attn_rpb, fnet_w, hy_conv_w, hy_conv_b, hy_w1, hy_b1, hy_freq,
                                     hy_w2, hy_b2, hy_w3, hy_decay, hy_skip, ml_gate_b, out_norm, w_out, norm_ffn,
                                     w_gate, w_up, w_down, ple_norm, w_ple_gate, w_ple_proj)))
    depth = norm_mix.shape[0]
    layers = [{k: v[i] for k, v in stacked.items()} for i in range(depth)]
    outs = []
    cache = {}
    for x, p in ((x_prompt, p_prompt), (x_sample, p_sample)):
        seq = x.shape[1]
        if seq not in cache:
            fn_tabs = _fnet_tables(seq)
            hy_tabs = _hyena_tables(seq)
            cache[seq] = (fn_tabs, hy_tabs, [_layer_consts(lw, seq, hy_tabs) for lw in layers])
        fn_tabs, hy_tabs, consts = cache[seq]
        outs.append(_trunk(x, p, layers, consts, final_norm, fn_tabs, hy_tabs))
    return tuple(outs)
```

```python
import functools
import math

import numpy as np
import jax
import jax.numpy as jnp
from jax import lax
from jax.experimental import pallas as pl
from jax.experimental.pallas import tpu as pltpu

F32, BF16 = jnp.float32, jnp.bfloat16

D_MODEL = 1024
DEPTH = 2
GRID_W = 64
HEAD_DIM = 64
GROUP_W = 256
N_HEADS = 4
KH = 8
KW = 16
HY_ORDER = 2
HY_EMB = 33
HY_BANDS = 16
ML_CHUNK = 128
PLE_DIM = 256
D_FF = 2816
EPS = 1e-6
QK_SCALE = HEAD_DIM ** -0.5
NEG = -1e30
N_GATES = 4 * N_HEADS
ML_STAT_ROWS = 6 * N_HEADS

LANES = 128
VMEM_LIMIT = 56 * 1024 * 1024
TOKEN_TILE = 512
INPROJ_TILE = 1024
FF_CHUNK = 256
NA_ROWS_PER_STEP = 8
HY_CH_BLOCK = 64
FN_CH_BLOCK = 32
FILT_TILE = 1024
ML_CHUNKS_PER_STEP = 8


def _cparams(*sem):
    return pltpu.CompilerParams(dimension_semantics=sem, vmem_limit_bytes=VMEM_LIMIT)


def _dot(a, b):
    return jnp.dot(a, b, preferred_element_type=F32)


def _dot_nt(a, b):
    return lax.dot_general(a, b, (((1,), (1,)), ((), ())), preferred_element_type=F32)


def _split2(x):
    hi = x.astype(BF16)
    return hi, (x - hi.astype(F32)).astype(BF16)


def _rms(x, g):
    return x * lax.rsqrt(jnp.mean(x * x, axis=-1, keepdims=True) + EPS) * g


def _sigmoid(x):
    return 1.0 / (1.0 + jnp.exp(-x))


def _full_spec(a):
    nd = a.ndim
    return pl.BlockSpec(a.shape, lambda *_: (0,) * nd, pipeline_mode=pl.Buffered(1))


def _inproj_kernel(x_ref, g_ref, wab_ref, wcg_ref, wd_ref, fc_ref, fs_ref, gbr_ref,
                   qa_ref, ka_ref, va_ref, zr_ref, zi_ref, uc_ref, qd_ref, vd_ref, od_ref, grow_ref, kdt_ref):
    xn = _rms(x_ref[0], g_ref[...]).astype(BF16)
    cg = _dot(xn, wcg_ref[...])
    gates = jnp.transpose(cg[:, 3 * GROUP_W:3 * GROUP_W + LANES])[:N_GATES, :] + gbr_ref[...]
    stats = []
    for rev in range(2):
        li = gates[2 * N_HEADS * rev:2 * N_HEADS * rev + N_HEADS, :]
        lf = jax.nn.log_sigmoid(gates[2 * N_HEADS * rev + N_HEADS:2 * N_HEADS * (rev + 1), :])
        b = _lane_scan(lf, ML_CHUNK, rev == 1, jnp.add)
        stats += [b, li - b, _lane_scan(li - b, ML_CHUNK, rev == 1, jnp.maximum)]
    grow_ref[0] = jnp.concatenate(stats, axis=0)
    blocks = [slice(j * LANES, (j + 1) * LANES) for j in range(x_ref.shape[1] // LANES)]
    for j, blk in enumerate(blocks):
        uc_ref[0, :, j, :] = jnp.transpose(cg[blk, :3 * GROUP_W])
    ab = _dot(xn, wab_ref[...])
    ub = ab[:, 3 * GROUP_W:].astype(BF16)
    zr = _dot(ub, fc_ref[...])
    zi = _dot(ub, fs_ref[...])
    for j, blk in enumerate(blocks):
        zr_ref[0, :, j, :] = jnp.transpose(zr[blk, :])
        zi_ref[0, :, j, :] = jnp.transpose(zi[blk, :])
    qa_ref[0] = (ab[:, :GROUP_W] * QK_SCALE).astype(BF16)
    ka_ref[0] = ab[:, GROUP_W:2 * GROUP_W].astype(BF16)
    va_ref[0] = ab[:, 2 * GROUP_W:3 * GROUP_W].astype(BF16)
    d = _dot(xn, wd_ref[...])
    for blk in blocks:
        kdt_ref[0, :, blk] = jnp.transpose(d[blk, GROUP_W:2 * GROUP_W] * QK_SCALE).astype(BF16)
    qd_ref[0] = d[:, :GROUP_W].astype(BF16)
    vd_ref[0] = d[:, 2 * GROUP_W:3 * GROUP_W].astype(BF16)
    od_ref[0] = d[:, 3 * GROUP_W:]


def _inproj(x, g, w_in, gate_b, fcs):
    bsz, seq, _ = x.shape
    tm = min(INPROJ_TILE, seq)
    wb16 = w_in.astype(BF16)
    wab = wb16[:, :4 * GROUP_W]
    wc = wb16[:, 4 * GROUP_W:7 * GROUP_W]
    wd = wb16[:, 7 * GROUP_W:11 * GROUP_W]
    wg = wb16[:, 11 * GROUP_W:11 * GROUP_W + N_GATES]
    wcg = jnp.concatenate([wc, jnp.pad(wg, ((0, 0), (0, GROUP_W - N_GATES)))], axis=1)
    gbr = gate_b.astype(F32).reshape(N_GATES, 1)
    fc, fs = fcs
    weights = (g.astype(F32).reshape(1, D_MODEL), wab, wcg, wd, fc, fs, gbr)

    def tok(width, dtype):
        return jax.ShapeDtypeStruct((bsz, seq, width), dtype), pl.BlockSpec((1, tm, width), lambda b, t: (b, t, 0))

    def chan(height, dtype):
        return jax.ShapeDtypeStruct((bsz, height, seq), dtype), pl.BlockSpec((1, height, tm), lambda b, t: (b, 0, t))

    def chan_tiles(height):
        return (jax.ShapeDtypeStruct((bsz, height, seq // LANES, LANES), F32),
                pl.BlockSpec((1, height, tm // LANES, LANES), lambda b, t: (b, 0, t, 0)))

    outs = [tok(GROUP_W, BF16), tok(GROUP_W, BF16), tok(GROUP_W, BF16),
            chan_tiles(GROUP_W), chan_tiles(GROUP_W), chan_tiles(3 * GROUP_W),
            tok(GROUP_W, BF16), tok(GROUP_W, BF16), tok(GROUP_W, F32),
            chan(ML_STAT_ROWS, F32), chan(GROUP_W, BF16)]
    return pl.pallas_call(
        _inproj_kernel,
        grid=(bsz, seq // tm),
        in_specs=[pl.BlockSpec((1, tm, D_MODEL), lambda b, t: (b, t, 0))] + [_full_spec(w) for w in weights],
        out_specs=[s for _, s in outs],
        out_shape=[s for s, _ in outs],
        compiler_params=_cparams("parallel", "parallel"),
        name="inproj",
    )(x, *weights)


def _na_kernel(q_ref, k_ref, v_ref, bias_ref, o_ref, *, rows, rows_per_step):
    i = pl.program_id(1)
    lane_head = lax.broadcasted_iota(jnp.int32, (GRID_W, GROUP_W), 1) // HEAD_DIM
    masks = [lane_head == h for h in range(N_HEADS)]
    for j in range(rows_per_step):
        r = i * rows_per_step + j
        kr0 = jnp.clip(r - KH // 2, 0, rows - KH)
        case = r - kr0
        ks = pl.multiple_of(kr0 * GRID_W, GRID_W)
        q = q_ref[0, j * GRID_W:(j + 1) * GRID_W, :]
        qs = jnp.concatenate([jnp.where(m, q, jnp.zeros_like(q)) for m in masks], axis=0)
        kw = k_ref[0, pl.ds(ks, KH * GRID_W), :]
        vw = v_ref[0, pl.ds(ks, KH * GRID_W), :]
        s = _dot_nt(qs, kw) + bias_ref[case]
        p = jnp.exp(s - jnp.max(s, axis=-1, keepdims=True))
        den = jnp.sum(p, axis=-1, keepdims=True)
        o = _dot(p.astype(BF16), vw) * (1.0 / den)
        out = jnp.zeros((GRID_W, GROUP_W), F32)
        for h in range(N_HEADS):
            out = out + jnp.where(masks[h], o[h * GRID_W:(h + 1) * GRID_W, :], 0.0)
        o_ref[0, j * GRID_W:(j + 1) * GRID_W, :] = out


def _na_bias_table(rpb):
    c = np.arange(GRID_W)
    kc = np.arange(GRID_W)
    kc0 = np.clip(c - KW // 2, 0, GRID_W - KW)
    valid = (kc[None, :] >= kc0[:, None]) & (kc[None, :] < kc0[:, None] + KW)
    dc = np.clip(kc[None, :] - c[:, None] + KW - 1, 0, 2 * KW - 2)
    case = np.arange(KH)
    j = np.arange(KH)
    dr = j[None, :] - case[:, None] + KH - 1
    row_sel = np.zeros((KH * KH, 2 * KH - 1), np.float32)
    row_sel[np.arange(KH * KH), dr.reshape(-1)] = 1.0
    col_sel = np.zeros((2 * KW - 1, GRID_W * GRID_W), np.float32)
    col_sel[dc.reshape(-1), np.arange(GRID_W * GRID_W)] = 1.0
    hp = lax.Precision.HIGHEST
    tab = jnp.einsum("rd,hde->hre", jnp.asarray(row_sel), rpb.astype(F32), precision=hp)
    tab = jnp.einsum("hre,ex->hrx", tab, jnp.asarray(col_sel), precision=hp)
    tab = tab.reshape(N_HEADS, KH, KH, GRID_W, GRID_W).transpose(1, 0, 3, 2, 4)
    tab = jnp.where(jnp.asarray(valid)[None, None, :, None, :], tab, NEG)
    return tab.reshape(KH, N_HEADS * GRID_W, KH * GRID_W)


def _na(qa, ka, va, bias):
    bsz, seq, _ = qa.shape
    rows = seq // GRID_W
    assert rows >= KH and rows % NA_ROWS_PER_STEP == 0
    rb = NA_ROWS_PER_STEP
    return pl.pallas_call(
        functools.partial(_na_kernel, rows=rows, rows_per_step=rb),
        grid=(bsz, rows // rb),
        in_specs=[pl.BlockSpec((1, rb * GRID_W, GROUP_W), lambda b, i: (b, i, 0)),
                  pl.BlockSpec((1, seq, GROUP_W), lambda b, i: (b, 0, 0)),
                  pl.BlockSpec((1, seq, GROUP_W), lambda b, i: (b, 0, 0)),
                  _full_spec(bias)],
        out_specs=pl.BlockSpec((1, rb * GRID_W, GROUP_W), lambda b, i: (b, i, 0)),
        out_shape=jax.ShapeDtypeStruct((bsz, seq, GROUP_W), F32),
        compiler_params=_cparams("parallel", "arbitrary"),
        name="nbr_attn",
    )(qa, ka, va, bias)


def _cs(num, den):
    ang = 2.0 * np.pi * (np.asarray(num, np.float64) % den) / den
    return np.cos(ang), np.sin(ang)


def _hilo(m):
    m32 = jnp.asarray(m, F32)
    hi = m32.astype(BF16)
    return hi, (m32 - hi.astype(F32)).astype(BF16)


def _fnet_tables(seq):
    n1f = seq // LANES
    c, s = _cs(np.outer(np.arange(HEAD_DIM), np.arange(HEAD_DIM)), HEAD_DIM)
    norm = 1.0 / math.sqrt(HEAD_DIM * seq)
    eye = np.eye(GROUP_W // HEAD_DIM)
    fc = np.kron(eye, c) * norm
    fs = np.kron(eye, -s) * norm
    c1, s1 = _cs(np.outer(np.arange(n1f), np.arange(n1f)), n1f)
    m1 = np.block([[c1, -s1], [s1, c1]])
    ct, st = _cs(np.outer(np.arange(LANES), np.arange(n1f)), seq)
    t1 = np.concatenate([ct, ct], axis=1)
    t2 = np.concatenate([st, -st], axis=1)
    c2, s2 = _cs(np.outer(np.arange(LANES), np.arange(LANES)), LANES)
    g2 = np.concatenate([c2, s2], axis=0)
    q = LANES // n1f
    g2 = np.stack([g2[:, a::q] for a in range(q)], axis=0)
    return dict(fc=jnp.asarray(fc, F32).astype(BF16), fs=jnp.asarray(fs, F32).astype(BF16),
                m1=jnp.asarray(m1, F32).astype(BF16), t1=jnp.asarray(t1, F32), t2=jnp.asarray(t2, F32),
                g2=jnp.asarray(g2, F32).astype(BF16))


def _hyena_tables(seq):
    n = 2 * seq
    n1 = n // LANES
    c1, s1 = _cs(np.outer(np.arange(n1), np.arange(n1)), n1)
    f1 = np.concatenate([c1, -s1], axis=1)
    ct, st = _cs(np.outer(np.arange(LANES), np.arange(n1)), n)
    c2, s2 = _cs(np.outer(np.arange(LANES), np.arange(LANES)), LANES)
    g2 = np.block([[c2, -s2], [s2, c2]])
    gi2 = np.block([[c2, s2], [-s2, c2]])
    fold = np.where(np.arange(n1) % (n1 // 2) == 0, 1.0, 2.0) * (np.arange(n1) <= n1 // 2)
    gi1 = np.concatenate([c1 * fold[:, None], -s1 * fold[:, None]], axis=0) / n
    gi1[:, n1 // 2:] = 0.0
    return dict(f1=_hilo(f1), tc=jnp.asarray(ct, F32), ts=jnp.asarray(st, F32),
                tct=jnp.asarray(ct.T, F32), tst=jnp.asarray(st.T, F32),
                g2=_hilo(g2), gi2=_hilo(gi2), gi1=_hilo(gi1))


def _mm(a, tab, precise):
    hi, lo = tab
    if not precise:
        return _dot(a.astype(BF16), hi)
    a_hi, a_lo = _split2(a)
    return _dot(a_hi, hi) + (_dot(a_lo, hi) + _dot(a_hi, lo))


def _fnet_kernel(zr_ref, zi_ref, m1_ref, t1_ref, t2_ref, g2_ref, o_ref):
    cb, n1f, n2 = zr_ref.shape[1:]
    w = 2 * n1f
    z = jnp.swapaxes(jnp.concatenate([zr_ref[0], zi_ref[0]], axis=1), 1, 2)
    a = _dot(z.reshape(cb * n2, w).astype(BF16), m1_ref[...])
    sw = pltpu.roll(a, n1f, 1).reshape(cb, n2, w)
    a = a.reshape(cb, n2, w) * t1_ref[...] + sw * t2_ref[...]
    at = jnp.swapaxes(a, 1, 2)
    op = jnp.concatenate([at[:, :n1f, :], at[:, n1f:, :]], axis=-1)
    op = op.reshape(cb * n1f, 2 * n2).astype(BF16)
    xt = jnp.concatenate([_dot(op, g2_ref[g]).reshape(cb, n1f, n1f) for g in range(g2_ref.shape[0])], axis=1)
    o_ref[0] = jnp.swapaxes(xt, 1, 2).reshape(cb, n1f * n2)


def _fnet(zr, zi, tabs):
    bsz, ch, n1f, _ = zr.shape
    cb = FN_CH_BLOCK
    consts = (tabs["m1"], tabs["t1"], tabs["t2"], tabs["g2"])
    z_spec = pl.BlockSpec((1, cb, n1f, LANES), lambda b, c: (b, c, 0, 0))
    return pl.pallas_call(
        _fnet_kernel,
        grid=(bsz, ch // cb),
        in_specs=[z_spec, z_spec] + [_full_spec(t) for t in consts],
        out_specs=pl.BlockSpec((1, cb, n1f * LANES), lambda b, c: (b, c, 0)),
        out_shape=jax.ShapeDtypeStruct((bsz, ch, n1f * LANES), F32),
        compiler_params=_cparams("parallel", "parallel"),
        name="fourier_mix",
    )(zr, zi, *consts)


def _filter_mlp_kernel(feat_ref, t_ref, w1_ref, b1_ref, f0_ref, w2_ref, b2_ref, f1_ref, w3_ref, dec_ref, o_ref):
    def dot3(a, w):
        return _mm(a, _split2(w), True)

    h = jnp.sin(f0_ref[...] * (dot3(feat_ref[...], w1_ref[...]) + b1_ref[...]))
    h = jnp.sin(f1_ref[...] * (dot3(h, w2_ref[...]) + b2_ref[...]))
    h = dot3(h, w3_ref[0])
    o_ref[...] = h * (jnp.exp(-t_ref[:, 0:1] * dec_ref[0]) * t_ref[:, 1:2])


def _hyena_filter_taps(seq, w1, b1, freq, w2, b2, w3, decay):
    ch = decay.shape[-1]
    r = jnp.arange(2 * seq)
    s = jnp.where(r < seq, r, 2 * seq - r).astype(F32)
    t = s / max(seq - 1, 1)
    ang = (2.0 * math.pi / seq) * s
    bands = jnp.linspace(1e-4, HY_BANDS - 1, HY_BANDS, dtype=F32)
    fb = ang[:, None] * bands[None, :]
    feats = jnp.concatenate([t[:, None], jnp.cos(fb), -jnp.sin(fb)], axis=-1)
    t_keep = jnp.stack([t, (r != seq).astype(F32)], axis=1)
    filt = w1.shape[1]
    pad_c = LANES - filt
    feats = jnp.pad(feats, ((0, 0), (0, LANES - HY_EMB)))
    w1p = jnp.pad(w1.astype(F32), ((0, LANES - HY_EMB), (0, pad_c)))
    w2p = jnp.pad(w2.astype(F32), ((0, pad_c), (0, pad_c)))
    w3s = w3.astype(F32).reshape(filt, HY_ORDER, 2, ch).transpose(2, 0, 1, 3).reshape(2, filt, HY_ORDER * ch)
    w3s = jnp.pad(w3s, ((0, 0), (0, pad_c), (0, 0)))
    decs = decay.astype(F32).transpose(1, 0, 2).reshape(2, 1, HY_ORDER * ch)
    row = lambda v: jnp.pad(v.astype(F32), (0, pad_c)).reshape(1, LANES)
    n_out = HY_ORDER * ch
    tl = min(FILT_TILE, seq)
    half = seq // tl
    side = lambda i: (i // half, 0, 0)
    args = (feats, t_keep, w1p, row(b1), row(freq[0]), w2p, row(b2), row(freq[1]), w3s, decs)
    specs = [pl.BlockSpec((tl, LANES), lambda i: (i, 0)), pl.BlockSpec((tl, 2), lambda i: (i, 0))]
    specs += [_full_spec(a) for a in args[2:8]]
    specs += [pl.BlockSpec((1, LANES, n_out), side), pl.BlockSpec((1, 1, n_out), side)]
    return pl.pallas_call(
        _filter_mlp_kernel,
        grid=(2 * half,),
        in_specs=specs,
        out_specs=pl.BlockSpec((tl, n_out), lambda i: (i, 0)),
        out_shape=jax.ShapeDtypeStruct((2 * seq, n_out), F32),
        compiler_params=_cparams("parallel"),
        name="hyena_filter_mlp",
    )(*args)


def _kept_rows(n1):
    return n1 // 2 + 8


def _fft_fwd(u, f1, tc, ts, g2, precise):
    cb, n2, n1_in = u.shape
    n1 = f1[0].shape[1] // 2
    kept = _kept_rows(n1)
    a = _mm(u.reshape(cb * n2, n1_in), f1, precise)
    ar = a[:, :n1].reshape(cb, n2, n1)
    ai = a[:, n1:].reshape(cb, n2, n1)
    br = ar * tc + ai * ts
    bi = ai * tc - ar * ts
    op = jnp.concatenate([jnp.swapaxes(br, 1, 2)[:, :kept, :], jnp.swapaxes(bi, 1, 2)[:, :kept, :]], axis=-1)
    return _mm(op.reshape(cb * kept, 2 * n2), g2, precise)


def _spectrum_kernel(h_ref, f1h, f1l, tc_ref, ts_ref, g2h, g2l, o_ref):
    x = _fft_fwd(h_ref[...], (f1h[...], f1l[...]), tc_ref[...], ts_ref[...], (g2h[...], g2l[...]), True)
    o_ref[...] = x.reshape(o_ref.shape)


def _hyena_spectra(full, tabs):
    n, nch = full.shape
    n1 = n // LANES
    kept = _kept_rows(n1)
    hcm = full.reshape(n1, LANES, nch).transpose(2, 1, 0)
    cb = HY_CH_BLOCK
    consts = (*tabs["f1"], tabs["tc"], tabs["ts"], *tabs["g2"])
    return pl.pallas_call(
        _spectrum_kernel,
        grid=(nch // cb,),
        in_specs=[pl.BlockSpec((cb, LANES, n1), lambda c: (c, 0, 0))] + [_full_spec(t) for t in consts],
        out_specs=pl.BlockSpec((cb, kept, 2 * LANES), lambda c: (c, 0, 0)),
        out_shape=jax.ShapeDtypeStruct((nch, kept, 2 * LANES), F32),
        compiler_params=_cparams("parallel"),
        name="hyena_filter_spectrum",
    )(hcm, *consts)


def _hyena_kernel(uv_ref, u1_ref, u2_ref, cv_ref, c1_ref, c2_ref, skip_ref, k0_ref, k1_ref,
                  f1_ref, tc_ref, ts_ref, g2_ref, gi2_ref, tct_ref, tst_ref, gi1_ref, o_ref):
    cb, h1, n2 = uv_ref.shape[1:]
    n1 = 2 * h1
    kept = k0_ref.shape[1]
    rows = cb * h1
    lane = lax.broadcasted_iota(jnp.int32, (rows, n2), 1)
    row_n1 = lax.broadcasted_iota(jnp.int32, (rows, n2), 0) % h1
    lane_first, lane_last = lane == 0, lane == n2 - 1
    seq_first = lane_first & (row_n1 == 0)
    seq_last = lane_last & (row_n1 == h1 - 1)

    def short_conv(u_ref, c_ref):
        u = u_ref[0]
        u2 = u.reshape(rows, n2)
        r = pltpu.roll(u2, 1, 1)
        prev = jnp.where(seq_first, 0.0, jnp.where(lane_first, pltpu.roll(r, 1, 0), r))
        r = pltpu.roll(u2, n2 - 1, 1)
        nxt = jnp.where(seq_last, 0.0, jnp.where(lane_last, pltpu.roll(r, rows - 1, 0), r))
        c = c_ref[...]
        return (prev.reshape(cb, h1, n2) * c[:, 0:1, :] + u * c[:, 1:2, :] + nxt.reshape(cb, h1, n2) * c[:, 2:3, :]
                + c[:, 3:4, :])

    tabs_f = ((f1_ref[...], None), tc_ref[...], ts_ref[...], (g2_ref[...], None))
    tct, tst = tct_ref[...], tst_ref[...]
    zero_rows = jnp.zeros((cb, n1 - kept, n2), F32)

    def fftconv(u, k_ref):
        x = _fft_fwd(jnp.swapaxes(u, 1, 2), *tabs_f, False)
        kf = k_ref[...].reshape(cb * kept, 2 * n2)
        xr, xi, kr, ki = x[:, :n2], x[:, n2:], kf[:, :n2], kf[:, n2:]
        y = jnp.concatenate([xr * kr - xi * ki, xr * ki + xi * kr], axis=-1)
        d = _dot(y.astype(BF16), gi2_ref[...])
        dr, di = d[:, :n2].reshape(cb, kept, n2), d[:, n2:].reshape(cb, kept, n2)
        er = jnp.concatenate([dr * tct - di * tst, zero_rows], axis=1)
        ei = jnp.concatenate([dr * tst + di * tct, zero_rows], axis=1)
        e = jnp.concatenate([jnp.swapaxes(er, 1, 2), jnp.swapaxes(ei, 1, 2)], axis=-1)
        out = _dot(e.reshape(cb * n2, 2 * n1).astype(BF16), gi1_ref[...])
        return jnp.swapaxes(out.reshape(cb, n2, h1), 1, 2)

    skip = skip_ref[...]
    v = short_conv(uv_ref, cv_ref)
    conv1 = fftconv(v, k0_ref)
    y1 = short_conv(u1_ref, c1_ref) * (conv1 + v * skip[:, 0:1, :])
    conv2 = fftconv(y1, k1_ref)
    y2 = short_conv(u2_ref, c2_ref) * (conv2 + y1 * skip[:, 1:2, :])
    o_ref[0] = y2.reshape(cb, h1 * n2)


def _hyena(u, conv_w, conv_b, skip, spectra, tabs):
    bsz, _, h1, _ = u.shape
    ch = GROUP_W
    n1 = 2 * h1
    cw =jnp.concatenate([conv_w.astype(F32), conv_b.astype(F32)[None]], axis=0).T
    cw = jnp.broadcast_to(cw[:, :, None], (3 * ch, 4, LANES))
    sk = jnp.broadcast_to(skip.astype(F32).T[:, :, None], (ch, HY_ORDER, LANES))
    cb = HY_CH_BLOCK
    nblk = ch // cb
    kept = _kept_rows(n1)
    consts = (tabs["f1"][0][:h1], tabs["tc"], tabs["ts"], tabs["g2"][0], tabs["gi2"][0], tabs["tct"][:kept],
              tabs["tst"][:kept], tabs["gi1"][0][:, :h1])
    u_spec = lambda g: pl.BlockSpec((1, cb, h1, LANES), lambda c, b, g=g: (b, c + g * nblk, 0, 0))
    c_spec = lambda g: pl.BlockSpec((cb, 4, LANES), lambda c, b, g=g: (c + g * nblk, 0, 0))
    k_spec = lambda o: pl.BlockSpec((cb, kept, 2 * LANES), lambda c, b, o=o: (c + o * nblk, 0, 0))
    y = pl.pallas_call(
        _hyena_kernel,
        grid=(nblk, bsz),
        in_specs=[u_spec(0), u_spec(1), u_spec(2), c_spec(0), c_spec(1), c_spec(2),
                  pl.BlockSpec((cb, HY_ORDER, LANES), lambda c, b: (c, 0, 0)), k_spec(0), k_spec(1)]
                 + [_full_spec(t) for t in consts],
        out_specs=pl.BlockSpec((1, cb, h1 * LANES), lambda c, b: (b, c, 0)),
        out_shape=jax.ShapeDtypeStruct((bsz, ch, h1 * LANES), F32),
        compiler_params=_cparams("parallel", "parallel"),
        name="hyena",
    )(u, u, u, cw, cw, cw, sk, spectra, spectra, *consts)
    return y


def _lane_scan(x, c, reverse, op):
    n = x.shape[-1]
    ax = x.ndim - 1
    pos = lax.broadcasted_iota(jnp.int32, x.shape, ax) % c
    k = 1
    while k < c:
        if reverse:
            x = jnp.where(pos < c - k, op(x, pltpu.roll(x, n - k, ax)), x)
        else:
            x = jnp.where(pos >= k, op(x, pltpu.roll(x, k, ax)), x)
        k *= 2
    return x


ML_STAT_LANES = 16


def _mlstm_direction(d, q, v, kt, gr, s, m0, chunks):
    c = ML_CHUNK
    n = chunks * c
    nh = N_HEADS
    reverse = d == 1
    b, row, cmax = (gr[(3 * d + i) * nh:(3 * d + i + 1) * nh, :] for i in range(3))
    order = list(range(chunks - 1, -1, -1) if reverse else range(chunks))

    m_in, m_top, d_old = {}, {}, {}
    m = m0
    for k in order:
        edge = k * c if reverse else (k + 1) * c - 1
        m_in[k] = m
        m_top[k] = jnp.maximum(m, cmax[:, edge:edge + 1])
        d_old[k] = jnp.exp(m - m_top[k])
        m = b[:, edge:edge + 1] + m_top[k]
    m_in_row = jnp.concatenate([jnp.broadcast_to(m_in[k], (nh, c)) for k in range(chunks)], axis=1)
    m_top_row = jnp.concatenate([jnp.broadcast_to(m_top[k], (nh, c)) for k in range(chunks)], axis=1)
    mx = jnp.maximum(m_in_row, cmax)
    wi = jnp.exp(m_in_row - mx)
    einv = jnp.exp(-b - mx)
    w = jnp.exp(row - m_top_row)
    zeros4 = jnp.zeros_like(mx)
    stat_a = jnp.transpose(jnp.concatenate([mx, wi, zeros4, zeros4], axis=0))
    stat_b = jnp.transpose(jnp.concatenate([zeros4, einv, zeros4, zeros4], axis=0))

    jj = lax.broadcasted_iota(jnp.int32, (c, c), 0)
    ss = lax.broadcasted_iota(jnp.int32, (c, c), 1)
    causal = (ss >= jj) if reverse else (ss <= jj)
    lane_head = lax.broadcasted_iota(jnp.int32, (c, GROUP_W), 1) // HEAD_DIM
    row_head = lax.broadcasted_iota(jnp.int32, (GROUP_W, c), 0) // HEAD_DIM
    stat_lane = lax.broadcasted_iota(jnp.int32, (c, ML_STAT_LANES), 1)
    head_lanes = (stat_lane >= nh) & (stat_lane < 2 * nh)
    ones_bd = (lax.broadcasted_iota(jnp.int32, (nh * c, LANES), 0) // c + nh
               == lax.broadcasted_iota(jnp.int32, (nh * c, LANES), 1)).astype(BF16)
    expand = (lax.broadcasted_iota(jnp.int32, (ML_STAT_LANES, GROUP_W), 0) - nh
              == lax.broadcasted_iota(jnp.int32, (ML_STAT_LANES, GROUP_W), 1) // HEAD_DIM).astype(BF16)
    s_rh = lax.broadcasted_iota(jnp.int32, (GROUP_W, GROUP_W + LANES), 0) // HEAD_DIM
    s_col = lax.broadcasted_iota(jnp.int32, (GROUP_W, GROUP_W + LANES), 1)
    s_mask = jnp.where(s_col < GROUP_W, s_col // HEAD_DIM, s_col - GROUP_W - nh) == s_rh
    ones_cols = jnp.ones((c, LANES), BF16)

    outs = {}
    for k in order:
        tok = slice(k * c, (k + 1) * c)
        qc, vc, ktc = q[tok, :], v[tok, :], kt[:, tok]
        k_bd = jnp.concatenate([jnp.where(row_head == h, ktc, jnp.zeros_like(ktc)) for h in range(nh)], axis=1)
        v_bd = jnp.concatenate([jnp.where(lane_head == h, vc, jnp.zeros_like(vc)) for h in range(nh)], axis=0)
        qk = _dot(qc, k_bd)
        sa = stat_a[tok, :]
        p = jnp.concatenate(
            [jnp.exp(jnp.where(causal, row[h:h + 1, tok] - sa[:, h:h + 1], NEG)) * qk[:, h * c:(h + 1) * c]
             for h in range(nh)], axis=1).astype(BF16)
        pv = _dot(p, v_bd)
        p_sum = _dot(p, ones_bd)[:, :ML_STAT_LANES]
        qs = _dot(qc, s.astype(BF16))
        den = sa * qs[:, GROUP_W:GROUP_W + ML_STAT_LANES] + p_sum
        rden = jnp.where(head_lanes, 1.0 / jnp.maximum(jnp.abs(den), stat_b[tok, :]), 0.0)
        wi_c = jnp.where(head_lanes, sa, 0.0)
        ex = _dot(jnp.concatenate([wi_c, rden], axis=0).astype(BF16), expand)
        outs[k] = (ex[:c] * qs[:, :GROUP_W] + pv) * ex[c:]

        w_full = jnp.concatenate([jnp.broadcast_to(w[h:h + 1, tok], (HEAD_DIM, c)) for h in range(nh)], axis=0)
        d_full = jnp.concatenate([jnp.broadcast_to(d_old[k][h:h + 1, :], (HEAD_DIM, 1)) for h in range(nh)], axis=0)
        ktw = (ktc.astype(F32) * w_full).astype(BF16)
        s_loc = _dot(ktw, jnp.concatenate([vc, ones_cols], axis=1))
        s = d_full * s + jnp.where(s_mask, s_loc, 0.0)
    return outs, s, m


def _mlstm_kernel(qf, vf, ktf, grf, qb, vb, ktb, grb, hf_ref, hb_ref, s_scr, m_scr, *, chunks):
    @pl.when(pl.program_id(1) == 0)
    def _():
        s_scr[...] = jnp.zeros_like(s_scr)
        m_scr[...] = jnp.zeros_like(m_scr)

    c = ML_CHUNK
    results = []
    for d, (q_ref, v_ref, kt_ref, gr_ref) in enumerate(((qf, vf, ktf, grf), (qb, vb, ktb, grb))):
        m0 = m_scr[d * 8:d * 8 + N_HEADS, :][:, :1]
        results.append(_mlstm_direction(d, q_ref[0], v_ref[0], kt_ref[0], gr_ref[0], s_scr[d], m0, chunks))
    for d, out_ref in enumerate((hf_ref, hb_ref)):
        outs, s, m = results[d]
        for k in range(chunks):
            out_ref[0, k * c:(k + 1) * c, :] = outs[k]
        s_scr[d] = s
        m_scr[d * 8:d * 8 + N_HEADS, :] = jnp.broadcast_to(m, (N_HEADS, LANES))


def _mlstm(qd, vd, kdt, grow):
    bsz, seq, _ = qd.shape
    g = min(ML_CHUNKS_PER_STEP, seq // ML_CHUNK)
    blk = g * ML_CHUNK
    nb = seq // blk
    fwd = lambda b, i: (b, i, 0)
    bwd = lambda b, i: (b, nb - 1 - i, 0)
    fwd_t = lambda b, i: (b, 0, i)
    bwd_t = lambda b, i: (b, 0, nb - 1 - i)

    def specs(tok, chan):
        return [pl.BlockSpec((1, blk, GROUP_W), tok), pl.BlockSpec((1, blk, GROUP_W), tok),
                pl.BlockSpec((1, GROUP_W, blk), chan), pl.BlockSpec((1, ML_STAT_ROWS, blk), chan)]

    args = (qd, vd, kdt, grow)
    return pl.pallas_call(
        functools.partial(_mlstm_kernel, chunks=g),
        grid=(bsz, nb),
        in_specs=specs(fwd, fwd_t) + specs(bwd, bwd_t),
        out_specs=[pl.BlockSpec((1, blk, GROUP_W), fwd), pl.BlockSpec((1, blk, GROUP_W), bwd)],
        out_shape=[jax.ShapeDtypeStruct((bsz, seq, GROUP_W), F32)] * 2,
        scratch_shapes=[pltpu.VMEM((2, GROUP_W, GROUP_W + LANES), F32), pltpu.VMEM((16, LANES), F32)],
        compiler_params=_cparams("parallel", "arbitrary"),
        name="mlstm",
    )(*args, *args)


def _post_kernel(x_ref, ya_ref, f_ref, yc_ref, hf_ref, hb_ref, od_ref, p_ref,
                 wfn_ref, onorm_ref, gsum_ref, gbc_ref, wout_ref, nffn_ref, wgate_ref, wup_ref, wdown_ref,
                 pnorm_ref, wpg_ref, wpp_ref, fnorm_ref, o_ref, *, final):
    yb = _dot(jnp.transpose(f_ref[0]).astype(BF16), wfn_ref[...])
    yd = _sigmoid(od_ref[0]) * (hf_ref[0] + hb_ref[0])
    y = jnp.concatenate([ya_ref[0], yb, jnp.transpose(yc_ref[0]), yd], axis=-1)
    ss = _dot((y * y).astype(BF16), gsum_ref[...])
    rb = _dot(lax.rsqrt(ss * (1.0 / HEAD_DIM) + EPS).astype(BF16), gbc_ref[...])
    x = x_ref[0] + _dot((y * rb * onorm_ref[...]).astype(BF16), wout_ref[...])
    hn = _rms(x, nffn_ref[...]).astype(BF16)
    acc = jnp.zeros_like(x)
    for c in range(D_FF // FF_CHUNK):
        sl = slice(c * FF_CHUNK, (c + 1) * FF_CHUNK)
        g = _dot(hn, wgate_ref[:, sl])
        u = _dot(hn, wup_ref[:, sl])
        acc = acc + _dot((g * _sigmoid(g) * u).astype(BF16), wdown_ref[sl, :])
    x = x + acc
    gate = _sigmoid(_dot(_rms(x, pnorm_ref[...]).astype(BF16), wpg_ref[...]))
    x = x + gate * _dot(p_ref[0].astype(BF16), wpp_ref[...])
    if final:
        x = _rms(x, fnorm_ref[...])
    o_ref[0] = x


def _post(x, ya, f, yc, hf, hb, od, p, lw, final_norm, final):
    bsz, seq, _ = x.shape
    tm = min(TOKEN_TILE, seq)
    ind = np.zeros((D_MODEL, LANES), np.float32)
    ind[np.arange(D_MODEL), np.arange(D_MODEL) // HEAD_DIM] = 1.0
    gsum = jnp.asarray(ind, BF16)
    gbc = jnp.asarray(ind.T, BF16)
    fw = lw["fnet_w"].astype(F32)
    wfn = jax.scipy.linalg.block_diag(*[fw[g] for g in range(fw.shape[0])]).astype(BF16)
    row = lambda v: v.astype(F32).reshape(1, D_MODEL)
    weights = (wfn, row(lw["out_norm"]), gsum, gbc, lw["w_out"].astype(BF16), row(lw["norm_ffn"]),
               lw["w_gate"].astype(BF16), lw["w_up"].astype(BF16), lw["w_down"].astype(BF16),
               row(lw["ple_norm"]), lw["w_ple_gate"].astype(BF16), lw["w_ple_proj"].astype(BF16), row(final_norm))
    tok = lambda width: pl.BlockSpec((1, tm, width), lambda b, t: (b, t, 0))
    chan = pl.BlockSpec((1, GROUP_W, tm), lambda b, t: (b, 0, t))
    return pl.pallas_call(
        functools.partial(_post_kernel, final=final),
        grid=(bsz, seq // tm),
        in_specs=[tok(D_MODEL), tok(GROUP_W), chan, chan] + [tok(GROUP_W)] * 3 + [tok(PLE_DIM)]
                 + [_full_spec(w) for w in weights],
        out_specs=tok(D_MODEL),
        out_shape=jax.ShapeDtypeStruct((bsz, seq, D_MODEL), F32),
        compiler_params=_cparams("parallel", "parallel"),
        name="post",
    )(x, ya, f, yc, hf, hb, od, p, *weights)


def _layer_consts(lw, seq, hy_tabs):
    full = _hyena_filter_taps(seq, lw["hy_w1"], lw["hy_b1"], lw["hy_freq"], lw["hy_w2"], lw["hy_b2"], lw["hy_w3"],
                              lw["hy_decay"])
    return dict(spectra=_hyena_spectra(full, hy_tabs), na_bias=_na_bias_table(lw["attn_rpb"]))


def _trunk(x, p, layers, consts, final_norm, fn_tabs, hy_tabs):
    for i, (lw, lc) in enumerate(zip(layers, consts)):
        (qa, ka, va, zr, zi, uc, qd, vd, od, grow, kdt) = _inproj(
            x, lw["norm_mix"], lw["w_in"], lw["ml_gate_b"], (fn_tabs["fc"], fn_tabs["fs"]))
        ya = _na(qa, ka, va, lc["na_bias"])
        f = _fnet(zr, zi, fn_tabs)
        yc = _hyena(uc, lw["hy_conv_w"], lw["hy_conv_b"], lw["hy_skip"], lc["spectra"], hy_tabs)
        hf, hb = _mlstm(qd, vd, kdt, grow)
        x = _post(x, ya, f, yc, hf, hb, od, p[i], lw, final_norm, final=(i == len(layers) - 1))
    return x


_LAYER_KEYS = ("norm_mix", "w_in", "attn_rpb", "fnet_w", "hy_conv_w", "hy_conv_b", "hy_w1", "hy_b1", "hy_freq",
               "hy_w2", "hy_b2", "hy_w3", "hy_decay", "hy_skip", "ml_gate_b", "out_norm", "w_out", "norm_ffn",
               "w_gate", "w_up", "w_down", "ple_norm", "w_ple_gate", "w_ple_proj")


def kernel(x_prompt, x_sample, p_prompt, p_sample, norm_mix, w_in, attn_rpb, fnet_w, hy_conv_w, hy_conv_b, hy_w1,
           hy_b1, hy_freq, hy_w2, hy_b2, hy_w3, hy_decay, hy_skip, ml_gate_b, out_norm, w_out, norm_ffn, w_gate,
           w_up, w_down, ple_norm, w_ple_gate, w_ple_proj, final_norm):
    stacked = dict(zip(_LAYER_KEYS, (norm_mix, w_in, attn_rpb, fnet_w, hy_conv_w, hy_conv_b, hy_w1, hy_b1, hy_freq,
                                     hy_w2, hy_b2, hy_w3, hy_decay, hy_skip, ml_gate_b, out_norm, w_out, norm_ffn,
                                     w_gate, w_up, w_down, ple_norm, w_ple_gate, w_ple_proj)))
    depth = norm_mix.shape[0]
    layers = [{k: v[i] for k, v in stacked.items()} for i in range(depth)]
    outs = []
    cache = {}
    for x, p in ((x_prompt, p_prompt), (x_sample, p_sample)):
        seq = x.shape[1]
        if seq not in cache:
            fn_tabs = _fnet_tables(seq)
            hy_tabs = _hyena_tables(seq)
            cache[seq] = (fn_tabs, hy_tabs, [_layer_consts(lw, seq, hy_tabs) for lw in layers])
        fn_tabs, hy_tabs, consts = cache[seq]
        outs.append(_trunk(x, p, layers, consts, final_norm, fn_tabs, hy_tabs))
    return tuple(outs)
```

```python
import functools
import math

import numpy as np
import jax
import jax.numpy as jnp
from jax import lax
from jax.experimental import pallas as pl
from jax.experimental.pallas import tpu as pltpu

F32, BF16 = jnp.float32, jnp.bfloat16

D_MODEL = 1024
GRID_W = 64
HEAD_DIM = 64
GROUP_W = 256
N_HEADS = 4
KH = 8
KW = 16
HY_ORDER = 2
HY_EMB = 33
HY_BANDS = 16
ML_CHUNK = 128
PLE_DIM = 256
D_FF = 2816
EPS = 1e-6
QK_SCALE = HEAD_DIM ** -0.5
NEG = -1e30
N_GATES = 4 * N_HEADS
ML_STAT_ROWS = 6 * N_HEADS

LANES = 128
SUBLANES = 8
VMEM_LIMIT = 56 * 1024 * 1024
TOKEN_TILE = 512
INPROJ_TILE = 1024
FF_CHUNK = 256
NA_ROWS_PER_STEP = 32
HY_CH_BLOCK = 64
FN_CH_BLOCK = 128
FILT_TILE = 1024
ML_CHUNKS_PER_STEP = 16


def _cparams(*sem):
    return pltpu.CompilerParams(dimension_semantics=sem, vmem_limit_bytes=VMEM_LIMIT)


def _dot(a, b):
    return jnp.dot(a, b, preferred_element_type=F32)


def _dot_nt(a, b):
    return lax.dot_general(a, b, (((1,), (1,)), ((), ())), preferred_element_type=F32)


def _split2(x):
    hi = x.astype(BF16)
    return hi, (x - hi.astype(F32)).astype(BF16)


def _rms(x, g):
    return x * lax.rsqrt(jnp.mean(x * x, axis=-1, keepdims=True) + EPS) * g


def _sigmoid(x):
    return 1.0 / (1.0 + jnp.exp(-x))


def _full_spec(a):
    nd = a.ndim
    return pl.BlockSpec(a.shape, lambda *_: (0,) * nd, pipeline_mode=pl.Buffered(1))


def _inproj_kernel(x_ref, g_ref, wab_ref, wcg_ref, wd_ref, fc_ref, fs_ref, gbr_ref,
                   qa_ref, ka_ref, va_ref, zr_ref, zi_ref, uc_ref, qd_ref, vd_ref, od_ref, grow_ref, kdt_ref):
    xn = _rms(x_ref[0], g_ref[...]).astype(BF16)
    cg = _dot(xn, wcg_ref[...])
    gates = jnp.transpose(cg[:, 3 * GROUP_W:3 * GROUP_W + LANES])[:N_GATES, :] + gbr_ref[...]
    stats = []
    for rev in range(2):
        li = gates[2 * N_HEADS * rev:2 * N_HEADS * rev + N_HEADS, :]
        lf = jax.nn.log_sigmoid(gates[2 * N_HEADS * rev + N_HEADS:2 * N_HEADS * (rev + 1), :])
        b = _lane_scan(lf, ML_CHUNK, rev == 1, jnp.add)
        stats += [b, li - b, _lane_scan(li - b, ML_CHUNK, rev == 1, jnp.maximum)]
    grow_ref[0] = jnp.concatenate(stats, axis=0)
    blocks = [slice(j * LANES, (j + 1) * LANES) for j in range(x_ref.shape[1] // LANES)]
    for j, blk in enumerate(blocks):
        uc_ref[0, :, j, :] = jnp.transpose(cg[blk, :3 * GROUP_W])
    ab = _dot(xn, wab_ref[...])
    ub = ab[:, 3 * GROUP_W:].astype(BF16)
    zr = _dot(ub, fc_ref[...])
    zi = _dot(ub, fs_ref[...])
    for j, blk in enumerate(blocks):
        zr_ref[0, :, j, :] = jnp.transpose(zr[blk, :])
        zi_ref[0, :, j, :] = jnp.transpose(zi[blk, :])
    qa_ref[0] = (ab[:, :GROUP_W] * QK_SCALE).astype(BF16)
    ka_ref[0] = ab[:, GROUP_W:2 * GROUP_W].astype(BF16)
    va_ref[0] = ab[:, 2 * GROUP_W:3 * GROUP_W].astype(BF16)
    d = _dot(xn, wd_ref[...])
    for blk in blocks:
        kdt_ref[0, :, blk] = jnp.transpose(d[blk, GROUP_W:2 * GROUP_W] * QK_SCALE).astype(BF16)
    qd_ref[0] = d[:, :GROUP_W].astype(BF16)
    vd_ref[0] = d[:, 2 * GROUP_W:3 * GROUP_W].astype(BF16)
    od_ref[0] = d[:, 3 * GROUP_W:]


def _inproj(x, g, w_in, gate_b, fcs):
    bsz, seq, _ = x.shape
    tm = min(INPROJ_TILE, seq)
    wb16 = w_in.astype(BF16)
    wab = wb16[:, :4 * GROUP_W]
    wc = wb16[:, 4 * GROUP_W:7 * GROUP_W]
    wd = wb16[:, 7 * GROUP_W:11 * GROUP_W]
    wg = wb16[:, 11 * GROUP_W:11 * GROUP_W + N_GATES]
    wcg = jnp.concatenate([wc, jnp.pad(wg, ((0, 0), (0, GROUP_W - N_GATES)))], axis=1)
    gbr = gate_b.astype(F32).reshape(N_GATES, 1)
    fc, fs = fcs
    weights = (g.astype(F32).reshape(1, D_MODEL), wab, wcg, wd, fc, fs, gbr)

    def tok(width, dtype):
        return jax.ShapeDtypeStruct((bsz, seq, width), dtype), pl.BlockSpec((1, tm, width), lambda b, t: (b, t, 0))

    def chan(height, dtype):
        return jax.ShapeDtypeStruct((bsz, height, seq), dtype), pl.BlockSpec((1, height, tm), lambda b, t: (b, 0, t))

    def chan_tiles(height):
        return (jax.ShapeDtypeStruct((bsz, height, seq // LANES, LANES), F32),
                pl.BlockSpec((1, height, tm // LANES, LANES), lambda b, t: (b, 0, t, 0)))

    outs = [tok(GROUP_W, BF16), tok(GROUP_W, BF16), tok(GROUP_W, BF16),
            chan_tiles(GROUP_W), chan_tiles(GROUP_W), chan_tiles(3 * GROUP_W),
            tok(GROUP_W, BF16), tok(GROUP_W, BF16), tok(GROUP_W, F32),
            chan(ML_STAT_ROWS, F32), chan(GROUP_W, BF16)]
    return pl.pallas_call(
        _inproj_kernel,
        grid=(bsz, seq // tm),
        in_specs=[pl.BlockSpec((1, tm, D_MODEL), lambda b, t: (b, t, 0))] + [_full_spec(w) for w in weights],
        out_specs=[s for _, s in outs],
        out_shape=[s for s, _ in outs],
        compiler_params=_cparams("parallel", "parallel"),
        name="inproj",
    )(x, *weights)


def _na_kernel(q_ref, k_ref, v_ref, bias_ref, o_ref, *, rows, rows_per_step):
    i = pl.program_id(1)
    lane_head = lax.broadcasted_iota(jnp.int32, (GRID_W, GROUP_W), 1) // HEAD_DIM
    masks = [lane_head == h for h in range(N_HEADS)]
    for j in range(rows_per_step):
        r = i * rows_per_step + j
        kr0 = jnp.clip(r - KH // 2, 0, rows - KH)
        case = r - kr0
        ks = pl.multiple_of(kr0 * GRID_W, GRID_W)
        q = q_ref[0, j * GRID_W:(j + 1) * GRID_W, :]
        qs = jnp.concatenate([jnp.where(m, q, jnp.zeros_like(q)) for m in masks], axis=0)
        kw = k_ref[0, pl.ds(ks, KH * GRID_W), :]
        vw = v_ref[0, pl.ds(ks, KH * GRID_W), :]
        s = _dot_nt(qs, kw) + bias_ref[case]
        p = jnp.exp(s - jnp.max(s, axis=-1, keepdims=True))
        den = jnp.sum(p, axis=-1, keepdims=True)
        o = _dot(p.astype(BF16), vw) * (1.0 / den)
        out = jnp.zeros((GRID_W, GROUP_W), F32)
        for h in range(N_HEADS):
            out = out + jnp.where(masks[h], o[h * GRID_W:(h + 1) * GRID_W, :], 0.0)
        o_ref[0, j * GRID_W:(j + 1) * GRID_W, :] = out


def _na_bias_table(rpb):
    c = np.arange(GRID_W)
    kc = np.arange(GRID_W)
    kc0 = np.clip(c - KW // 2, 0, GRID_W - KW)
    valid = (kc[None, :] >= kc0[:, None]) & (kc[None, :] < kc0[:, None] + KW)
    dc = np.clip(kc[None, :] - c[:, None] + KW - 1, 0, 2 * KW - 2)
    case = np.arange(KH)
    j = np.arange(KH)
    dr = j[None, :] - case[:, None] + KH - 1
    row_sel = np.zeros((KH * KH, 2 * KH - 1), np.float32)
    row_sel[np.arange(KH * KH), dr.reshape(-1)] = 1.0
    col_sel = np.zeros((2 * KW - 1, GRID_W * GRID_W), np.float32)
    col_sel[dc.reshape(-1), np.arange(GRID_W * GRID_W)] = 1.0
    hp = lax.Precision.HIGHEST
    tab = jnp.einsum("rd,hde->hre", jnp.asarray(row_sel), rpb.astype(F32), precision=hp)
    tab = jnp.einsum("hre,ex->hrx", tab, jnp.asarray(col_sel), precision=hp)
    tab = tab.reshape(N_HEADS, KH, KH, GRID_W, GRID_W).transpose(1, 0, 3, 2, 4)
    tab = jnp.where(jnp.asarray(valid)[None, None, :, None, :], tab, NEG)
    return tab.reshape(KH, N_HEADS * GRID_W, KH * GRID_W)


def _na(qa, ka, va, bias):
    bsz, seq, _ = qa.shape
    rows = seq // GRID_W
    rb = min(NA_ROWS_PER_STEP, rows)
    assert rows >= KH and rows % rb == 0
    return pl.pallas_call(
        functools.partial(_na_kernel, rows=rows, rows_per_step=rb),
        grid=(bsz, rows // rb),
        in_specs=[pl.BlockSpec((1, rb * GRID_W, GROUP_W), lambda b, i: (b, i, 0)),
                  pl.BlockSpec((1, seq, GROUP_W), lambda b, i: (b, 0, 0)),
                  pl.BlockSpec((1, seq, GROUP_W), lambda b, i: (b, 0, 0)),
                  _full_spec(bias)],
        out_specs=pl.BlockSpec((1, rb * GRID_W, GROUP_W), lambda b, i: (b, i, 0)),
        out_shape=jax.ShapeDtypeStruct((bsz, seq, GROUP_W), F32),
        compiler_params=_cparams("parallel", "arbitrary"),
        name="nbr_attn",
    )(qa, ka, va, bias)


def _cs(num, den):
    ang = 2.0 * np.pi * (np.asarray(num, np.float64) % den) / den
    return np.cos(ang), np.sin(ang)


def _hilo(m):
    m32 = jnp.asarray(m, F32)
    hi = m32.astype(BF16)
    return hi, (m32 - hi.astype(F32)).astype(BF16)


def _fnet_tables(seq):
    n1f = seq // LANES
    c, s = _cs(np.outer(np.arange(HEAD_DIM), np.arange(HEAD_DIM)), HEAD_DIM)
    norm = 1.0 / math.sqrt(HEAD_DIM * seq)
    eye = np.eye(GROUP_W // HEAD_DIM)
    fc = np.kron(eye, c) * norm
    fs = np.kron(eye, -s) * norm
    c1, s1 = _cs(np.outer(np.arange(n1f), np.arange(n1f)), n1f)
    m1 = np.block([[c1, -s1, -s1, c1], [s1, c1, c1, s1]])
    ct, st = _cs(np.outer(np.arange(LANES), np.arange(n1f)), seq)
    t1 = np.concatenate([ct, ct], axis=1)
    t2 = np.concatenate([st, -st], axis=1)
    c2, s2 = _cs(np.outer(np.arange(LANES), np.arange(LANES)), LANES)
    g2 = np.concatenate([c2, s2], axis=0)
    q = LANES // n1f
    g2 = np.stack([g2[:, a::q] for a in range(q)], axis=0)
    return dict(fc=jnp.asarray(fc, F32).astype(BF16), fs=jnp.asarray(fs, F32).astype(BF16),
                m1=jnp.asarray(m1, F32).astype(BF16), t1=jnp.asarray(t1, F32), t2=jnp.asarray(t2, F32),
                g2=jnp.asarray(g2, F32).astype(BF16))


def _hyena_tables(seq):
    n = 2 * seq
    n1 = n // LANES
    c1, s1 = _cs(np.outer(np.arange(n1), np.arange(n1)), n1)
    f1 = np.concatenate([c1, -s1], axis=1)
    ct, st = _cs(np.outer(np.arange(LANES), np.arange(n1)), n)
    c2, s2 = _cs(np.outer(np.arange(LANES), np.arange(LANES)), LANES)
    g2 = np.block([[c2, -s2], [s2, c2]])
    gi2 = np.block([[c2, s2], [-s2, c2]])
    fold = np.where(np.arange(n1) % (n1 // 2) == 0, 1.0, 2.0) * (np.arange(n1) <= n1 // 2)
    gi1 = np.concatenate([c1 * fold[:, None], -s1 * fold[:, None]], axis=0) / n
    gi1[:, n1 // 2:] = 0.0
    return dict(f1=_hilo(f1), tc=jnp.asarray(ct, F32), ts=jnp.asarray(st, F32),
                tct=jnp.asarray(ct.T, F32), tst=jnp.asarray(st.T, F32),
                g2=_hilo(g2), gi2=_hilo(gi2), gi1=_hilo(gi1))


def _mm(a, tab, precise):
    hi, lo = tab
    if not precise:
        return _dot(a.astype(BF16), hi)
    a_hi, a_lo = _split2(a)
    return _dot(a_hi, hi) + (_dot(a_lo, hi) + _dot(a_hi, lo))


def _fnet_kernel(zr_ref, zi_ref, m1_ref, t1_ref, t2_ref, g2_ref, o_ref):
    cb, n1f, n2 = zr_ref.shape[1:]
    w = 2 * n1f
    z = jnp.swapaxes(jnp.concatenate([zr_ref[0], zi_ref[0]], axis=1), 1, 2)
    a = _dot(z.reshape(cb * n2, w).astype(BF16), m1_ref[...])
    a = a[:, :w].reshape(cb, n2, w) * t1_ref[...] + a[:, w:].reshape(cb, n2, w) * t2_ref[...]
    at = jnp.swapaxes(a, 1, 2)
    op = jnp.concatenate([at[:, :n1f, :], at[:, n1f:, :]], axis=-1)
    op = op.reshape(cb * n1f, 2 * n2).astype(BF16)
    xt = jnp.concatenate([_dot(op, g2_ref[g]).reshape(cb, n1f, n1f) for g in range(g2_ref.shape[0])], axis=1)
    o_ref[0] = jnp.swapaxes(xt, 1, 2).reshape(cb, n1f * n2)


def _fnet(zr, zi, tabs):
    bsz, ch, n1f, _ = zr.shape
    cb = FN_CH_BLOCK
    consts = (tabs["m1"], tabs["t1"], tabs["t2"], tabs["g2"])
    z_spec = pl.BlockSpec((1, cb, n1f, LANES), lambda b, c: (b, c, 0, 0))
    return pl.pallas_call(
        _fnet_kernel,
        grid=(bsz, ch // cb),
        in_specs=[z_spec, z_spec] + [_full_spec(t) for t in consts],
        out_specs=pl.BlockSpec((1, cb, n1f * LANES), lambda b, c: (b, c, 0)),
        out_shape=jax.ShapeDtypeStruct((bsz, ch, n1f * LANES), F32),
        compiler_params=_cparams("parallel", "parallel"),
        name="fourier_mix",
    )(zr, zi, *consts)


def _filter_mlp_kernel(feat_ref, t_ref, w1_ref, b1_ref, f0_ref, w2_ref, b2_ref, f1_ref, w3_ref, dec_ref, o_ref):
    def dot3(a, w):
        return _mm(a, _split2(w), True)

    h = jnp.sin(f0_ref[...] * (dot3(feat_ref[...], w1_ref[...]) + b1_ref[...]))
    h = jnp.sin(f1_ref[...] * (dot3(h, w2_ref[...]) + b2_ref[...]))
    h = dot3(h, w3_ref[0])
    h = h * (jnp.exp(-t_ref[:, 0:1] * dec_ref[0]) * t_ref[:, 1:2])
    for j in range(h.shape[0] // LANES):
        o_ref[:, j, :] = jnp.transpose(h[j * LANES:(j + 1) * LANES, :])


def _hyena_filter_taps(seq, w1, b1, freq, w2, b2, w3, decay):
    ch = decay.shape[-1]
    r = jnp.arange(2 * seq)
    s = jnp.where(r < seq, r, 2 * seq - r).astype(F32)
    t = s / max(seq - 1, 1)
    ang = (2.0 * math.pi / seq) * s
    bands = jnp.linspace(1e-4, HY_BANDS - 1, HY_BANDS, dtype=F32)
    fb = ang[:, None] * bands[None, :]
    feats = jnp.concatenate([t[:, None], jnp.cos(fb), -jnp.sin(fb)], axis=-1)
    t_keep = jnp.stack([t, (r != seq).astype(F32)], axis=1)
    filt = w1.shape[1]
    pad_c = LANES - filt
    feats = jnp.pad(feats, ((0, 0), (0, LANES - HY_EMB)))
    w1p = jnp.pad(w1.astype(F32), ((0, LANES - HY_EMB), (0, pad_c)))
    w2p = jnp.pad(w2.astype(F32), ((0, pad_c), (0, pad_c)))
    w3s = w3.astype(F32).reshape(filt, HY_ORDER, 2, ch).transpose(2, 0, 1, 3).reshape(2, filt, HY_ORDER * ch)
    w3s = jnp.pad(w3s, ((0, 0), (0, pad_c), (0, 0)))
    decs = decay.astype(F32).transpose(1, 0, 2).reshape(2, 1, HY_ORDER * ch)
    row = lambda v: jnp.pad(v.astype(F32), (0, pad_c)).reshape(1, LANES)
    n_out = HY_ORDER * ch
    tl = min(FILT_TILE, seq)
    half = seq // tl
    side = lambda i: (i // half, 0, 0)
    args = (feats, t_keep, w1p, row(b1), row(freq[0]), w2p, row(b2), row(freq[1]), w3s, decs)
    specs = [pl.BlockSpec((tl, LANES), lambda i: (i, 0)), pl.BlockSpec((tl, 2), lambda i: (i, 0))]
    specs += [_full_spec(a) for a in args[2:8]]
    specs += [pl.BlockSpec((1, LANES, n_out), side), pl.BlockSpec((1, 1, n_out), side)]
    return pl.pallas_call(
        _filter_mlp_kernel,
        grid=(2 * half,),
        in_specs=specs,
        out_specs=pl.BlockSpec((n_out, tl // LANES, LANES), lambda i: (0, i, 0)),
        out_shape=jax.ShapeDtypeStruct((n_out, 2 * seq // LANES, LANES), F32),
        compiler_params=_cparams("parallel"),
        name="hyena_filter_mlp",
    )(*args)


def _kept_rows(n1):
    return n1 // 2 + SUBLANES


def _fft_fwd(u, f1, tc, ts, g2, precise):
    cb, n2, n1_in = u.shape
    n1 = f1[0].shape[1] // 2
    kept = _kept_rows(n1)
    a = _mm(u.reshape(cb * n2, n1_in), f1, precise)
    ar = a[:, :n1].reshape(cb, n2, n1)
    ai = a[:, n1:].reshape(cb, n2, n1)
    br = ar * tc + ai * ts
    bi = ai * tc - ar * ts
    op = jnp.concatenate([jnp.swapaxes(br, 1, 2)[:, :kept, :], jnp.swapaxes(bi, 1, 2)[:, :kept, :]], axis=-1)
    return _mm(op.reshape(cb * kept, 2 * n2), g2, precise)


def _spectrum_kernel(h_ref, f1h, f1l, tc_ref, ts_ref, g2h, g2l, o_ref):
    x = _fft_fwd(jnp.swapaxes(h_ref[...], 1, 2), (f1h[...], f1l[...]), tc_ref[...], ts_ref[...],
                 (g2h[...], g2l[...]), True)
    o_ref[...] = x.reshape(o_ref.shape)


def _hyena_spectra(full, tabs):
    nch, n1, _ = full.shape
    kept = _kept_rows(n1)
    cb = HY_CH_BLOCK
    consts = (*tabs["f1"], tabs["tc"], tabs["ts"], *tabs["g2"])
    return pl.pallas_call(
        _spectrum_kernel,
        grid=(nch // cb,),
        in_specs=[pl.BlockSpec((cb, n1, LANES), lambda c: (c, 0, 0))] + [_full_spec(t) for t in consts],
        out_specs=pl.BlockSpec((cb, kept, 2 * LANES), lambda c: (c, 0, 0)),
        out_shape=jax.ShapeDtypeStruct((nch, kept, 2 * LANES), F32),
        compiler_params=_cparams("parallel"),
        name="hyena_filter_spectrum",
    )(full, *consts)


def _hyena_kernel(uv_ref, u1_ref, u2_ref, cv_ref, c1_ref, c2_ref, skip_ref, k0_ref, k1_ref,
                  f1_ref, tc_ref, ts_ref, g2_ref, gi2_ref, tct_ref, tst_ref, gi1_ref, o_ref):
    cb, h1, n2 = uv_ref.shape[1:]
    n1 = 2 * h1
    kept = k0_ref.shape[1]
    rows = cb * h1
    lane = lax.broadcasted_iota(jnp.int32, (rows, n2), 1)
    row_n1 = lax.broadcasted_iota(jnp.int32, (rows, n2), 0) % h1
    lane_first, lane_last = lane == 0, lane == n2 - 1
    seq_first = lane_first & (row_n1 == 0)
    seq_last = lane_last & (row_n1 == h1 - 1)

    def short_conv(u_ref, c_ref):
        u = u_ref[0]
        u2 = u.reshape(rows, n2)
        r = pltpu.roll(u2, 1, 1)
        prev = jnp.where(seq_first, 0.0, jnp.where(lane_first, pltpu.roll(r, 1, 0), r))
        r = pltpu.roll(u2, n2 - 1, 1)
        nxt = jnp.where(seq_last, 0.0, jnp.where(lane_last, pltpu.roll(r, rows - 1, 0), r))
        c = c_ref[...]
        return (prev.reshape(cb, h1, n2) * c[:, 0:1, :] + u * c[:, 1:2, :] + nxt.reshape(cb, h1, n2) * c[:, 2:3, :]
                + c[:, 3:4, :])

    tabs_f = ((f1_ref[...], None), tc_ref[...], ts_ref[...], (g2_ref[...], None))
    tct, tst = tct_ref[...], tst_ref[...]
    zero_rows = jnp.zeros((cb, n1 - kept, n2), F32)

    def fftconv(u, k_ref):
        x = _fft_fwd(jnp.swapaxes(u, 1, 2), *tabs_f, False)
        kf = k_ref[...].reshape(cb * kept, 2 * n2)
        xr, xi, kr, ki = x[:, :n2], x[:, n2:], kf[:, :n2], kf[:, n2:]
        y = jnp.concatenate([xr * kr - xi * ki, xr * ki + xi * kr], axis=-1)
        d = _dot(y.astype(BF16), gi2_ref[...])
        dr, di = d[:, :n2].reshape(cb, kept, n2), d[:, n2:].reshape(cb, kept, n2)
        er = jnp.concatenate([dr * tct - di * tst, zero_rows], axis=1)
        ei = jnp.concatenate([dr * tst + di * tct, zero_rows], axis=1)
        e = jnp.concatenate([jnp.swapaxes(er, 1, 2), jnp.swapaxes(ei, 1, 2)], axis=-1)
        out = _dot(e.reshape(cb * n2, 2 * n1).astype(BF16), gi1_ref[...])
        return jnp.swapaxes(out.reshape(cb, n2, h1), 1, 2)

    skip = skip_ref[...]
    v = short_conv(uv_ref, cv_ref)
    conv1 = fftconv(v, k0_ref)
    y1 = short_conv(u1_ref, c1_ref) * (conv1 + v * skip[:, 0:1, :])
    conv2 = fftconv(y1, k1_ref)
    y2 = short_conv(u2_ref, c2_ref) * (conv2 + y1 * skip[:, 1:2, :])
    o_ref[0] = y2.reshape(cb, h1 * n2)


def _hyena(u, conv_w, conv_b, skip, spectra, tabs):
    bsz, _, h1, _ = u.shape
    ch = GROUP_W
    n1 = 2 * h1
    cw =jnp.concatenate([conv_w.astype(F32), conv_b.astype(F32)[None]], axis=0).T
    cw = jnp.broadcast_to(cw[:, :, None], (3 * ch, 4, LANES))
    sk = jnp.broadcast_to(skip.astype(F32).T[:, :, None], (ch, HY_ORDER, LANES))
    cb = HY_CH_BLOCK
    nblk = ch // cb
    kept = _kept_rows(n1)
    consts = (tabs["f1"][0][:h1], tabs["tc"], tabs["ts"], tabs["g2"][0], tabs["gi2"][0], tabs["tct"][:kept],
              tabs["tst"][:kept], tabs["gi1"][0][:, :h1])
    u_spec = lambda g: pl.BlockSpec((1, cb, h1, LANES), lambda c, b, g=g: (b, c + g * nblk, 0, 0))
    c_spec = lambda g: pl.BlockSpec((cb, 4, LANES), lambda c, b, g=g: (c + g * nblk, 0, 0))
    k_spec = lambda o: pl.BlockSpec((cb, kept, 2 * LANES), lambda c, b, o=o: (c + o * nblk, 0, 0))
    y = pl.pallas_call(
        _hyena_kernel,
        grid=(nblk, bsz),
        in_specs=[u_spec(0), u_spec(1), u_spec(2), c_spec(0), c_spec(1), c_spec(2),
                  pl.BlockSpec((cb, HY_ORDER, LANES), lambda c, b: (c, 0, 0)), k_spec(0), k_spec(1)]
                 + [_full_spec(t) for t in consts],
        out_specs=pl.BlockSpec((1, cb, h1 * LANES), lambda c, b: (b, c, 0)),
        out_shape=jax.ShapeDtypeStruct((bsz, ch, h1 * LANES), F32),
        compiler_params=_cparams("parallel", "parallel"),
        name="hyena",
    )(u, u, u, cw, cw, cw, sk, spectra, spectra, *consts)
    return y


def _lane_scan(x, c, reverse, op):
    n = x.shape[-1]
    ax = x.ndim - 1
    pos = lax.broadcasted_iota(jnp.int32, x.shape, ax) % c
    k = 1
    while k < c:
        if reverse:
            x = jnp.where(pos < c - k, op(x, pltpu.roll(x, n - k, ax)), x)
        else:
            x = jnp.where(pos >= k, op(x, pltpu.roll(x, k, ax)), x)
        k *= 2
    return x


ML_STAT_LANES = 16


def _mlstm_direction(d, q_ref, v_ref, kt_ref, gr, out_ref, s, m0, chunks):
    c = ML_CHUNK
    n = chunks * c
    nh = N_HEADS
    reverse = d == 1
    b, row, cmax = (gr[(3 * d + i) * nh:(3 * d + i + 1) * nh, :] for i in range(3))
    order = list(range(chunks - 1, -1, -1) if reverse else range(chunks))

    m_in, m_top, d_old = {}, {}, {}
    m = m0
    for k in order:
        edge = k * c if reverse else (k + 1) * c - 1
        m_in[k] = m
        m_top[k] = jnp.maximum(m, cmax[:, edge:edge + 1])
        d_old[k] = jnp.exp(m - m_top[k])
        m = b[:, edge:edge + 1] + m_top[k]
    m_in_row = jnp.concatenate([jnp.broadcast_to(m_in[k], (nh, c)) for k in range(chunks)], axis=1)
    m_top_row = jnp.concatenate([jnp.broadcast_to(m_top[k], (nh, c)) for k in range(chunks)], axis=1)
    mx = jnp.maximum(m_in_row, cmax)
    wi = jnp.exp(m_in_row - mx)
    einv = jnp.exp(-b - mx)
    w = jnp.exp(row - m_top_row)
    zeros4 = jnp.zeros_like(mx)
    stat_a = jnp.transpose(jnp.concatenate([mx, wi, zeros4, zeros4], axis=0))
    stat_b = jnp.transpose(jnp.concatenate([zeros4, einv, zeros4, zeros4], axis=0))

    jj = lax.broadcasted_iota(jnp.int32, (c, c), 0)
    ss = lax.broadcasted_iota(jnp.int32, (c, c), 1)
    causal = (ss >= jj) if reverse else (ss <= jj)
    lane_head = lax.broadcasted_iota(jnp.int32, (c, GROUP_W), 1) // HEAD_DIM
    row_head = lax.broadcasted_iota(jnp.int32, (GROUP_W, c), 0) // HEAD_DIM
    stat_lane = lax.broadcasted_iota(jnp.int32, (c, ML_STAT_LANES), 1)
    head_lanes = (stat_lane >= nh) & (stat_lane < 2 * nh)
    ones_bd = (lax.broadcasted_iota(jnp.int32, (nh * c, LANES), 0) // c + nh
               == lax.broadcasted_iota(jnp.int32, (nh * c, LANES), 1)).astype(BF16)
    expand = (lax.broadcasted_iota(jnp.int32, (ML_STAT_LANES, GROUP_W), 0) - nh
              == lax.broadcasted_iota(jnp.int32, (ML_STAT_LANES, GROUP_W), 1) // HEAD_DIM).astype(BF16)
    s_rh = lax.broadcasted_iota(jnp.int32, (GROUP_W, GROUP_W + LANES), 0) // HEAD_DIM
    s_col = lax.broadcasted_iota(jnp.int32, (GROUP_W, GROUP_W + LANES), 1)
    s_mask = jnp.where(s_col < GROUP_W, s_col // HEAD_DIM, s_col - GROUP_W - nh) == s_rh
    ones_cols = jnp.ones((c, LANES), BF16)

    for k in order:
        tok = slice(k * c, (k + 1) * c)
        qc, vc, ktc = q_ref[0, tok, :], v_ref[0, tok, :], kt_ref[0, :, tok]
        k_bd = jnp.concatenate([jnp.where(row_head == h, ktc, jnp.zeros_like(ktc)) for h in range(nh)], axis=1)
        v_bd = jnp.concatenate([jnp.where(lane_head == h, vc, jnp.zeros_like(vc)) for h in range(nh)], axis=0)
        qk = _dot(qc, k_bd)
        sa = stat_a[tok, :]
        p = jnp.concatenate(
            [jnp.exp(jnp.where(causal, row[h:h + 1, tok] - sa[:, h:h + 1], NEG)) * qk[:, h * c:(h + 1) * c]
             for h in range(nh)], axis=1).astype(BF16)
        pv = _dot(p, v_bd)
        p_sum = _dot(p, ones_bd)[:, :ML_STAT_LANES]
        qs = _dot(qc, s.astype(BF16))
        den = sa * qs[:, GROUP_W:GROUP_W + ML_STAT_LANES] + p_sum
        rden = jnp.where(head_lanes, 1.0 / jnp.maximum(jnp.abs(den), stat_b[tok, :]), 0.0)
        wi_c = jnp.where(head_lanes, sa, 0.0)
        ex = _dot(jnp.concatenate([wi_c, rden], axis=0).astype(BF16), expand)
        out_ref[0, tok, :] = (ex[:c] * qs[:, :GROUP_W] + pv) * ex[c:]

        w_full = jnp.concatenate([jnp.broadcast_to(w[h:h + 1, tok], (HEAD_DIM, c)) for h in range(nh)], axis=0)
        d_full = jnp.concatenate([jnp.broadcast_to(d_old[k][h:h + 1, :], (HEAD_DIM, 1)) for h in range(nh)], axis=0)
        ktw = (ktc.astype(F32) * w_full).astype(BF16)
        s_loc = _dot(ktw, jnp.concatenate([vc, ones_cols], axis=1))
        s = d_full * s + jnp.where(s_mask, s_loc, 0.0)
    return s, m


def _mlstm_kernel(qf, vf, ktf, grf, qb, vb, ktb, grb, hf_ref, hb_ref, s_scr, m_scr, *, chunks):
    @pl.when(pl.program_id(1) == 0)
    def _():
        s_scr[...] = jnp.zeros_like(s_scr)
        m_scr[...] = jnp.zeros_like(m_scr)

    for d, (q_ref, v_ref, kt_ref, gr_ref, out_ref) in enumerate(((qf, vf, ktf, grf, hf_ref),
                                                                  (qb, vb, ktb, grb, hb_ref))):
        m0 = m_scr[d * SUBLANES:d * SUBLANES + N_HEADS, :][:, :1]
        s, m = _mlstm_direction(d, q_ref, v_ref, kt_ref, gr_ref[0], out_ref, s_scr[d], m0, chunks)
        s_scr[d] = s
        m_scr[d * SUBLANES:d * SUBLANES + N_HEADS, :] = jnp.broadcast_to(m, (N_HEADS, LANES))


def _mlstm(qd, vd, kdt, grow):
    bsz, seq, _ = qd.shape
    g = min(ML_CHUNKS_PER_STEP, seq // ML_CHUNK)
    blk = g * ML_CHUNK
    nb = seq // blk
    fwd = lambda b, i: (b, i, 0)
    bwd = lambda b, i: (b, nb - 1 - i, 0)
    fwd_t = lambda b, i: (b, 0, i)
    bwd_t = lambda b, i: (b, 0, nb - 1 - i)

    def specs(tok, chan):
        return [pl.BlockSpec((1, blk, GROUP_W), tok), pl.BlockSpec((1, blk, GROUP_W), tok),
                pl.BlockSpec((1, GROUP_W, blk), chan), pl.BlockSpec((1, ML_STAT_ROWS, blk), chan)]

    args = (qd, vd, kdt, grow)
    return pl.pallas_call(
        functools.partial(_mlstm_kernel, chunks=g),
        grid=(bsz, nb),
        in_specs=specs(fwd, fwd_t) + specs(bwd, bwd_t),
        out_specs=[pl.BlockSpec((1, blk, GROUP_W), fwd), pl.BlockSpec((1, blk, GROUP_W), bwd)],
        out_shape=[jax.ShapeDtypeStruct((bsz, seq, GROUP_W), F32)] * 2,
        scratch_shapes=[pltpu.VMEM((2, GROUP_W, GROUP_W + LANES), F32), pltpu.VMEM((2 * SUBLANES, LANES), F32)],
        compiler_params=_cparams("parallel", "arbitrary"),
        name="mlstm",
    )(*args, *args)


def _post_kernel(x_ref, ya_ref, f_ref, yc_ref, hf_ref, hb_ref, od_ref, p_ref,
                 wfn_ref, onorm_ref, gsum_ref, gbc_ref, wout_ref, nffn_ref, wgate_ref, wup_ref, wdown_ref,
                 pnorm_ref, wpg_ref, wpp_ref, fnorm_ref, o_ref, *, final):
    yb = _dot(jnp.transpose(f_ref[0]).astype(BF16), wfn_ref[...])
    yd = _sigmoid(od_ref[0]) * (hf_ref[0] + hb_ref[0])
    y = jnp.concatenate([ya_ref[0], yb, jnp.transpose(yc_ref[0]), yd], axis=-1)
    ss = _dot((y * y).astype(BF16), gsum_ref[...])
    rb = _dot(lax.rsqrt(ss * (1.0 / HEAD_DIM) + EPS).astype(BF16), gbc_ref[...])
    x = x_ref[0] + _dot((y * rb * onorm_ref[...]).astype(BF16), wout_ref[...])
    hn = _rms(x, nffn_ref[...]).astype(BF16)
    act = []
    for c in range(D_FF // FF_CHUNK):
        sl = slice(c * FF_CHUNK, (c + 1) * FF_CHUNK)
        g = _dot(hn, wgate_ref[:, sl])
        u = _dot(hn, wup_ref[:, sl])
        act.append((g * _sigmoid(g) * u).astype(BF16))
    x = x + _dot(jnp.concatenate(act, axis=1), wdown_ref[...])
    gate = _sigmoid(_dot(_rms(x, pnorm_ref[...]).astype(BF16), wpg_ref[...]))
    x = x + gate * _dot(p_ref[0].astype(BF16), wpp_ref[...])
    if final:
        x = _rms(x, fnorm_ref[...])
    o_ref[0] = x


def _post(x, ya, f, yc, hf, hb, od, p, lw, final_norm, final):
    bsz, seq, _ = x.shape
    tm = min(TOKEN_TILE, seq)
    ind = np.zeros((D_MODEL, LANES), np.float32)
    ind[np.arange(D_MODEL), np.arange(D_MODEL) // HEAD_DIM] = 1.0
    gsum = jnp.asarray(ind, BF16)
    gbc = jnp.asarray(ind.T, BF16)
    fw = lw["fnet_w"].astype(F32)
    wfn = jax.scipy.linalg.block_diag(*[fw[g] for g in range(fw.shape[0])]).astype(BF16)
    row = lambda v: v.astype(F32).reshape(1, D_MODEL)
    weights = (wfn, row(lw["out_norm"]), gsum, gbc, lw["w_out"].astype(BF16), row(lw["norm_ffn"]),
               lw["w_gate"].astype(BF16), lw["w_up"].astype(BF16), lw["w_down"].astype(BF16),
               row(lw["ple_norm"]), lw["w_ple_gate"].astype(BF16), lw["w_ple_proj"].astype(BF16), row(final_norm))
    tok = lambda width: pl.BlockSpec((1, tm, width), lambda b, t: (b, t, 0))
    chan = pl.BlockSpec((1, GROUP_W, tm), lambda b, t: (b, 0, t))
    return pl.pallas_call(
        functools.partial(_post_kernel, final=final),
        grid=(bsz, seq // tm),
        in_specs=[tok(D_MODEL), tok(GROUP_W), chan, chan] + [tok(GROUP_W)] * 3 + [tok(PLE_DIM)]
                 + [_full_spec(w) for w in weights],
        out_specs=tok(D_MODEL),
        out_shape=jax.ShapeDtypeStruct((bsz, seq, D_MODEL), F32),
        compiler_params=_cparams("parallel", "parallel"),
        name="post",
    )(x, ya, f, yc, hf, hb, od, p, *weights)


def _layer_consts(lw, seq, hy_tabs):
    full = _hyena_filter_taps(seq, lw["hy_w1"], lw["hy_b1"], lw["hy_freq"], lw["hy_w2"], lw["hy_b2"], lw["hy_w3"],
                              lw["hy_decay"])
    return dict(spectra=_hyena_spectra(full, hy_tabs), na_bias=_na_bias_table(lw["attn_rpb"]))


def _trunk(x, p, layers, consts, final_norm, fn_tabs, hy_tabs):
    for i, (lw, lc) in enumerate(zip(layers, consts)):
        (qa, ka, va, zr, zi, uc, qd, vd, od, grow, kdt) = _inproj(
            x, lw["norm_mix"], lw["w_in"], lw["ml_gate_b"], (fn_tabs["fc"], fn_tabs["fs"]))
        ya = _na(qa, ka, va, lc["na_bias"])
        f = _fnet(zr, zi, fn_tabs)
        yc = _hyena(uc, lw["hy_conv_w"], lw["hy_conv_b"], lw["hy_skip"], lc["spectra"], hy_tabs)
        hf, hb = _mlstm(qd, vd, kdt, grow)
        x = _post(x, ya, f, yc, hf, hb, od, p[i], lw, final_norm, final=(i == len(layers) - 1))
    return x


_LAYER_KEYS = ("norm_mix", "w_in", "attn_rpb", "fnet_w", "hy_conv_w", "hy_conv_b", "hy_w1", "hy_b1", "hy_freq",
               "hy_w2", "hy_b2", "hy_w3", "hy_decay", "hy_skip", "ml_gate_b", "out_norm", "w_out", "norm_ffn",
               "w_gate", "w_up", "w_down", "ple_norm", "w_ple_gate", "w_ple_proj")


def kernel(x_prompt, x_sample, p_prompt, p_sample, norm_mix, w_in, attn_rpb, fnet_w, hy_conv_w, hy_conv_b, hy_w1,
           hy_b1, hy_freq, hy_w2, hy_b2, hy_w3, hy_decay, hy_skip, ml_gate_b, out_norm, w_out, norm_ffn, w_gate,
           w_up, w_down, ple_norm, w_ple_gate, w_ple_proj, final_norm):
    stacked = dict(zip(_LAYER_KEYS, (norm_mix, w_in, attn_rpb, fnet_w, hy_conv_w, hy_conv_b, hy_w1, hy_b1, hy_freq,
                                     hy_w2, hy_b2, hy_w3, hy_decay, hy_skip, ml_gate_b, out_norm, w_out, norm_ffn,
                                     w_gate, w_up, w_down, ple_norm, w_ple_gate, w_ple_proj)))
    depth = norm_mix.shape[0]
    layers = [{k: v[i] for k, v in stacked.items()} for i in range(depth)]
    outs = []
    cache = {}
    for x, p in ((x_prompt, p_prompt), (x_sample, p_sample)):
        seq = x.shape[1]
        if seq not in cache:
            fn_tabs = _fnet_tables(seq)
            hy_tabs = _hyena_tables(seq)
            cache[seq] = (fn_tabs, hy_tabs, [_layer_consts(lw, seq, hy_tabs) for lw in layers])
        fn_tabs, hy_tabs, consts = cache[seq]
        outs.append(_trunk(x, p, layers, consts, final_norm, fn_tabs, hy_tabs))
    return tuple(outs)
```

```python
import functools
import math

import numpy as np
import jax
import jax.numpy as jnp
from jax import lax
from jax.experimental import pallas as pl
from jax.experimental.pallas import tpu as pltpu

F32, BF16 = jnp.float32, jnp.bfloat16

D_MODEL = 1024
GRID_W = 64
HEAD_DIM = 64
GROUP_W = 256
N_HEADS = 4
KH = 8
KW = 16
HY_ORDER = 2
HY_EMB = 33
HY_BANDS = 16
ML_CHUNK = 128
PLE_DIM = 256
D_FF = 2816
EPS = 1e-6
QK_SCALE = HEAD_DIM ** -0.5
NEG = -1e30
N_GATES = 4 * N_HEADS
ML_STAT_ROWS = 6 * N_HEADS

LANES = 128
SUBLANES = 8
VMEM_LIMIT = 56 * 1024 * 1024
TOKEN_TILE = 512
INPROJ_TILE = 1024
FF_CHUNK = 256
NA_ROWS_PER_STEP = 32
HY_CH_BLOCK = 64
FN_CH_BLOCK = 128
FILT_TILE = 1024
ML_CHUNKS_PER_STEP = 16


def _cparams(*sem):
    return pltpu.CompilerParams(dimension_semantics=sem, vmem_limit_bytes=VMEM_LIMIT)


def _dot(a, b):
    return jnp.dot(a, b, preferred_element_type=F32)


def _dot_nt(a, b):
    return lax.dot_general(a, b, (((1,), (1,)), ((), ())), preferred_element_type=F32)


def _split2(x):
    hi = x.astype(BF16)
    return hi, (x - hi.astype(F32)).astype(BF16)


def _rms(x, g):
    return x * lax.rsqrt(jnp.mean(x * x, axis=-1, keepdims=True) + EPS) * g


def _sigmoid(x):
    return 1.0 / (1.0 + jnp.exp(-x))


def _full_spec(a):
    nd = a.ndim
    return pl.BlockSpec(a.shape, lambda *_: (0,) * nd, pipeline_mode=pl.Buffered(1))


def _inproj_kernel(x_ref, g_ref, wab_ref, wcg_ref, wd_ref, fc_ref, fs_ref, gbr_ref,
                   qa_ref, ka_ref, va_ref, zr_ref, zi_ref, uc_ref, qd_ref, vd_ref, od_ref, grow_ref, kdt_ref):
    xn = _rms(x_ref[0], g_ref[...]).astype(BF16)
    cg = _dot(xn, wcg_ref[...])
    gates = jnp.transpose(cg[:, 3 * GROUP_W:3 * GROUP_W + LANES])[:N_GATES, :] + gbr_ref[...]
    stats = []
    for rev in range(2):
        li = gates[2 * N_HEADS * rev:2 * N_HEADS * rev + N_HEADS, :]
        lf = jax.nn.log_sigmoid(gates[2 * N_HEADS * rev + N_HEADS:2 * N_HEADS * (rev + 1), :])
        b = _lane_scan(lf, ML_CHUNK, rev == 1, jnp.add)
        stats += [b, li - b, _lane_scan(li - b, ML_CHUNK, rev == 1, jnp.maximum)]
    grow_ref[0] = jnp.concatenate(stats, axis=0)
    blocks = [slice(j * LANES, (j + 1) * LANES) for j in range(x_ref.shape[1] // LANES)]
    for j, blk in enumerate(blocks):
        uc_ref[0, :, j, :] = jnp.transpose(cg[blk, :3 * GROUP_W])
    ab = _dot(xn, wab_ref[...])
    ub = ab[:, 3 * GROUP_W:].astype(BF16)
    zr = _dot(ub, fc_ref[...])
    zi = _dot(ub, fs_ref[...])
    for j, blk in enumerate(blocks):
        zr_ref[0, :, j, :] = jnp.transpose(zr[blk, :])
        zi_ref[0, :, j, :] = jnp.transpose(zi[blk, :])
    qa_ref[0] = (ab[:, :GROUP_W] * QK_SCALE).astype(BF16)
    ka_ref[0] = ab[:, GROUP_W:2 * GROUP_W].astype(BF16)
    va_ref[0] = ab[:, 2 * GROUP_W:3 * GROUP_W].astype(BF16)
    d = _dot(xn, wd_ref[...])
    for blk in blocks:
        kdt_ref[0, :, blk] = jnp.transpose(d[blk, GROUP_W:2 * GROUP_W] * QK_SCALE).astype(BF16)
    qd_ref[0] = d[:, :GROUP_W].astype(BF16)
    vd_ref[0] = d[:, 2 * GROUP_W:3 * GROUP_W].astype(BF16)
    od_ref[0] = d[:, 3 * GROUP_W:]


def _inproj(x, g, w_in, gate_b, fcs):
    bsz, seq, _ = x.shape
    tm = min(INPROJ_TILE, seq)
    wb16 = w_in.astype(BF16)
    wab = wb16[:, :4 * GROUP_W]
    wc = wb16[:, 4 * GROUP_W:7 * GROUP_W]
    wd = wb16[:, 7 * GROUP_W:11 * GROUP_W]
    wg = wb16[:, 11 * GROUP_W:11 * GROUP_W + N_GATES]
    wcg = jnp.concatenate([wc, jnp.pad(wg, ((0, 0), (0, GROUP_W - N_GATES)))], axis=1)
    gbr = gate_b.astype(F32).reshape(N_GATES, 1)
    fc, fs = fcs
    weights = (g.astype(F32).reshape(1, D_MODEL), wab, wcg, wd, fc, fs, gbr)

    def tok(width, dtype):
        return jax.ShapeDtypeStruct((bsz, seq, width), dtype), pl.BlockSpec((1, tm, width), lambda b, t: (b, t, 0))

    def chan(height, dtype):
        return jax.ShapeDtypeStruct((bsz, height, seq), dtype), pl.BlockSpec((1, height, tm), lambda b, t: (b, 0, t))

    def chan_tiles(height):
        return (jax.ShapeDtypeStruct((bsz, height, seq // LANES, LANES), F32),
                pl.BlockSpec((1, height, tm // LANES, LANES), lambda b, t: (b, 0, t, 0)))

    outs = [tok(GROUP_W, BF16), tok(GROUP_W, BF16), tok(GROUP_W, BF16),
            chan_tiles(GROUP_W), chan_tiles(GROUP_W), chan_tiles(3 * GROUP_W),
            tok(GROUP_W, BF16), tok(GROUP_W, BF16), tok(GROUP_W, F32),
            chan(ML_STAT_ROWS, F32), chan(GROUP_W, BF16)]
    return pl.pallas_call(
        _inproj_kernel,
        grid=(bsz, seq // tm),
        in_specs=[pl.BlockSpec((1, tm, D_MODEL), lambda b, t: (b, t, 0))] + [_full_spec(w) for w in weights],
        out_specs=[s for _, s in outs],
        out_shape=[s for s, _ in outs],
        compiler_params=_cparams("parallel", "parallel"),
        name="inproj",
    )(x, *weights)


def _na_kernel(q_ref, k_ref, v_ref, bias_ref, o_ref, *, rows, rows_per_step):
    i = pl.program_id(1)
    lane_head = lax.broadcasted_iota(jnp.int32, (GRID_W, GROUP_W), 1) // HEAD_DIM
    masks = [lane_head == h for h in range(N_HEADS)]
    for j in range(rows_per_step):
        r = i * rows_per_step + j
        kr0 = jnp.clip(r - KH // 2, 0, rows - KH)
        case = r - kr0
        ks = pl.multiple_of(kr0 * GRID_W, GRID_W)
        q = q_ref[0, j * GRID_W:(j + 1) * GRID_W, :]
        qs = jnp.concatenate([jnp.where(m, q, jnp.zeros_like(q)) for m in masks], axis=0)
        kw = k_ref[0, pl.ds(ks, KH * GRID_W), :]
        vw = v_ref[0, pl.ds(ks, KH * GRID_W), :]
        s = _dot_nt(qs, kw) + bias_ref[case]
        p = jnp.exp(s - jnp.max(s, axis=-1, keepdims=True))
        den = jnp.sum(p, axis=-1, keepdims=True)
        o = _dot(p.astype(BF16), vw) * (1.0 / den)
        out = jnp.zeros((GRID_W, GROUP_W), F32)
        for h in range(N_HEADS):
            out = out + jnp.where(masks[h], o[h * GRID_W:(h + 1) * GRID_W, :], 0.0)
        o_ref[0, j * GRID_W:(j + 1) * GRID_W, :] = out


def _na_bias_table(rpb):
    c = np.arange(GRID_W)
    kc = np.arange(GRID_W)
    kc0 = np.clip(c - KW // 2, 0, GRID_W - KW)
    valid = (kc[None, :] >= kc0[:, None]) & (kc[None, :] < kc0[:, None] + KW)
    dc = np.clip(kc[None, :] - c[:, None] + KW - 1, 0, 2 * KW - 2)
    case = np.arange(KH)
    j = np.arange(KH)
    dr = j[None, :] - case[:, None] + KH - 1
    row_sel = np.zeros((KH * KH, 2 * KH - 1), np.float32)
    row_sel[np.arange(KH * KH), dr.reshape(-1)] = 1.0
    col_sel = np.zeros((2 * KW - 1, GRID_W * GRID_W), np.float32)
    col_sel[dc.reshape(-1), np.arange(GRID_W * GRID_W)] = 1.0
    hp = lax.Precision.HIGHEST
    tab = jnp.einsum("rd,hde->hre", jnp.asarray(row_sel), rpb.astype(F32), precision=hp)
    tab = jnp.einsum("hre,ex->hrx", tab, jnp.asarray(col_sel), precision=hp)
    tab = tab.reshape(N_HEADS, KH, KH, GRID_W, GRID_W).transpose(1, 0, 3, 2, 4)
    tab = jnp.where(jnp.asarray(valid)[None, None, :, None, :], tab, NEG)
    return tab.reshape(KH, N_HEADS * GRID_W, KH * GRID_W)


def _na(qa, ka, va, bias):
    bsz, seq, _ = qa.shape
    rows = seq // GRID_W
    rb = min(NA_ROWS_PER_STEP, rows)
    assert rows >= KH and rows % rb == 0
    return pl.pallas_call(
        functools.partial(_na_kernel, rows=rows, rows_per_step=rb),
        grid=(bsz, rows // rb),
        in_specs=[pl.BlockSpec((1, rb * GRID_W, GROUP_W), lambda b, i: (b, i, 0)),
                  pl.BlockSpec((1, seq, GROUP_W), lambda b, i: (b, 0, 0)),
                  pl.BlockSpec((1, seq, GROUP_W), lambda b, i: (b, 0, 0)),
                  _full_spec(bias)],
        out_specs=pl.BlockSpec((1, rb * GRID_W, GROUP_W), lambda b, i: (b, i, 0)),
        out_shape=jax.ShapeDtypeStruct((bsz, seq, GROUP_W), F32),
        compiler_params=_cparams("parallel", "arbitrary"),
        name="nbr_attn",
    )(qa, ka, va, bias)


def _cs(num, den):
    ang = 2.0 * np.pi * (np.asarray(num, np.float64) % den) / den
    return np.cos(ang), np.sin(ang)


def _hilo(m):
    m32 = jnp.asarray(m, F32)
    hi = m32.astype(BF16)
    return hi, (m32 - hi.astype(F32)).astype(BF16)


def _fnet_tables(seq):
    n1f = seq // LANES
    c, s = _cs(np.outer(np.arange(HEAD_DIM), np.arange(HEAD_DIM)), HEAD_DIM)
    norm = 1.0 / math.sqrt(HEAD_DIM * seq)
    eye = np.eye(GROUP_W // HEAD_DIM)
    fc = np.kron(eye, c) * norm
    fs = np.kron(eye, -s) * norm
    c1, s1 = _cs(np.outer(np.arange(n1f), np.arange(n1f)), n1f)
    m1 = np.block([[c1, -s1], [s1, c1]])
    ct, st = _cs(np.outer(np.arange(LANES), np.arange(n1f)), seq)
    t1 = np.concatenate([ct, ct], axis=1)
    t2 = np.concatenate([st, -st], axis=1)
    c2, s2 = _cs(np.outer(np.arange(LANES), np.arange(LANES)), LANES)
    g2 = np.concatenate([c2, s2], axis=0)
    q = LANES // n1f
    g2 = np.stack([g2[:, a::q] for a in range(q)], axis=0)
    return dict(fc=jnp.asarray(fc, F32).astype(BF16), fs=jnp.asarray(fs, F32).astype(BF16),
                m1=jnp.asarray(m1, F32).astype(BF16), t1=jnp.asarray(t1, F32), t2=jnp.asarray(t2, F32),
                g2=jnp.asarray(g2, F32).astype(BF16))


def _hyena_tables(seq):
    n = 2 * seq
    n1 = n // LANES
    c1, s1 = _cs(np.outer(np.arange(n1), np.arange(n1)), n1)
    f1 = np.concatenate([c1, -s1], axis=1)
    ct, st = _cs(np.outer(np.arange(LANES), np.arange(n1)), n)
    c2, s2 = _cs(np.outer(np.arange(LANES), np.arange(LANES)), LANES)
    g2 = np.block([[c2, -s2], [s2, c2]])
    gi2 = np.block([[c2, s2], [-s2, c2]])
    fold = np.where(np.arange(n1) % (n1 // 2) == 0, 1.0, 2.0) * (np.arange(n1) <= n1 // 2)
    gi1 = np.concatenate([c1 * fold[:, None], -s1 * fold[:, None]], axis=0) / n
    gi1[:, n1 // 2:] = 0.0
    return dict(f1=_hilo(f1), tc=jnp.asarray(ct, F32), ts=jnp.asarray(st, F32),
                tct=jnp.asarray(ct.T, F32), tst=jnp.asarray(st.T, F32),
                g2=_hilo(g2), gi2=_hilo(gi2), gi1=_hilo(gi1))


def _mm(a, tab, precise):
    hi, lo = tab
    if not precise:
        return _dot(a.astype(BF16), hi)
    a_hi, a_lo = _split2(a)
    return _dot(a_hi, hi) + (_dot(a_lo, hi) + _dot(a_hi, lo))


def _fnet_kernel(zr_ref, zi_ref, m1_ref, t1_ref, t2_ref, g2_ref, o_ref):
    cb, n1f, n2 = zr_ref.shape[1:]
    w = 2 * n1f
    z = jnp.swapaxes(jnp.concatenate([zr_ref[0], zi_ref[0]], axis=1), 1, 2)
    a = _dot(z.reshape(cb * n2, w).astype(BF16), m1_ref[...])
    sw = pltpu.roll(a, n1f, 1).reshape(cb, n2, w)
    a = a.reshape(cb, n2, w) * t1_ref[...] + sw * t2_ref[...]
    at = jnp.swapaxes(a, 1, 2)
    op = jnp.concatenate([at[:, :n1f, :], at[:, n1f:, :]], axis=-1)
    op = op.reshape(cb * n1f, 2 * n2).astype(BF16)
    xt = jnp.concatenate([_dot(op, g2_ref[g]).reshape(cb, n1f, n1f) for g in range(g2_ref.shape[0])], axis=1)
    o_ref[0] = jnp.swapaxes(xt, 1, 2).reshape(cb, n1f * n2)


def _fnet(zr, zi, tabs):
    bsz, ch, n1f, _ = zr.shape
    cb = FN_CH_BLOCK
    consts = (tabs["m1"], tabs["t1"], tabs["t2"], tabs["g2"])
    z_spec = pl.BlockSpec((1, cb, n1f, LANES), lambda b, c: (b, c, 0, 0))
    return pl.pallas_call(
        _fnet_kernel,
        grid=(bsz, ch // cb),
        in_specs=[z_spec, z_spec] + [_full_spec(t) for t in consts],
        out_specs=pl.BlockSpec((1, cb, n1f * LANES), lambda b, c: (b, c, 0)),
        out_shape=jax.ShapeDtypeStruct((bsz, ch, n1f * LANES), F32),
        compiler_params=_cparams("parallel", "parallel"),
        name="fourier_mix",
    )(zr, zi, *consts)


def _filter_mlp_kernel(feat_ref, t_ref, w1_ref, b1_ref, f0_ref, w2_ref, b2_ref, f1_ref, w3_ref, dec_ref, o_ref):
    def dot3(a, w):
        return _mm(a, _split2(w), True)

    h = jnp.sin(f0_ref[...] * (dot3(feat_ref[...], w1_ref[...]) + b1_ref[...]))
    h = jnp.sin(f1_ref[...] * (dot3(h, w2_ref[...]) + b2_ref[...]))
    h = dot3(h, w3_ref[0])
    h = h * (jnp.exp(-t_ref[:, 0:1] * dec_ref[0]) * t_ref[:, 1:2])
    for j in range(h.shape[0] // LANES):
        o_ref[:, j, :] = jnp.transpose(h[j * LANES:(j + 1) * LANES, :])


def _hyena_filter_taps(seq, w1, b1, freq, w2, b2, w3, decay):
    ch = decay.shape[-1]
    r = jnp.arange(2 * seq)
    s = jnp.where(r < seq, r, 2 * seq - r).astype(F32)
    t = s / max(seq - 1, 1)
    ang = (2.0 * math.pi / seq) * s
    bands = jnp.linspace(1e-4, HY_BANDS - 1, HY_BANDS, dtype=F32)
    fb = ang[:, None] * bands[None, :]
    feats = jnp.concatenate([t[:, None], jnp.cos(fb), -jnp.sin(fb)], axis=-1)
    t_keep = jnp.stack([t, (r != seq).astype(F32)], axis=1)
    filt = w1.shape[1]
    pad_c = LANES - filt
    feats = jnp.pad(feats, ((0, 0), (0, LANES - HY_EMB)))
    w1p = jnp.pad(w1.astype(F32), ((0, LANES - HY_EMB), (0, pad_c)))
    w2p = jnp.pad(w2.astype(F32), ((0, pad_c), (0, pad_c)))
    w3s = w3.astype(F32).reshape(filt, HY_ORDER, 2, ch).transpose(2, 0, 1, 3).reshape(2, filt, HY_ORDER * ch)
    w3s = jnp.pad(w3s, ((0, 0), (0, pad_c), (0, 0)))
    decs = decay.astype(F32).transpose(1, 0, 2).reshape(2, 1, HY_ORDER * ch)
    row = lambda v: jnp.pad(v.astype(F32), (0, pad_c)).reshape(1, LANES)
    n_out = HY_ORDER * ch
    tl = min(FILT_TILE, seq)
    half = seq // tl
    side = lambda i: (i // half, 0, 0)
    args = (feats, t_keep, w1p, row(b1), row(freq[0]), w2p, row(b2), row(freq[1]), w3s, decs)
    specs = [pl.BlockSpec((tl, LANES), lambda i: (i, 0)), pl.BlockSpec((tl, 2), lambda i: (i, 0))]
    specs += [_full_spec(a) for a in args[2:8]]
    specs += [pl.BlockSpec((1, LANES, n_out), side), pl.BlockSpec((1, 1, n_out), side)]
    return pl.pallas_call(
        _filter_mlp_kernel,
        grid=(2 * half,),
        in_specs=specs,
        out_specs=pl.BlockSpec((n_out, tl // LANES, LANES), lambda i: (0, i, 0)),
        out_shape=jax.ShapeDtypeStruct((n_out, 2 * seq // LANES, LANES), F32),
        compiler_params=_cparams("parallel"),
        name="hyena_filter_mlp",
    )(*args)


def _kept_rows(n1):
    return n1 // 2 + SUBLANES


def _fft_fwd(u, f1, tc, ts, g2, precise):
    cb, n2, n1_in = u.shape
    n1 = f1[0].shape[1] // 2
    kept = _kept_rows(n1)
    a = _mm(u.reshape(cb * n2, n1_in), f1, precise)
    ar = a[:, :n1].reshape(cb, n2, n1)
    ai = a[:, n1:].reshape(cb, n2, n1)
    br = ar * tc + ai * ts
    bi = ai * tc - ar * ts
    op = jnp.concatenate([jnp.swapaxes(br, 1, 2)[:, :kept, :], jnp.swapaxes(bi, 1, 2)[:, :kept, :]], axis=-1)
    return _mm(op.reshape(cb * kept, 2 * n2), g2, precise)


def _spectrum_kernel(h_ref, f1h, f1l, tc_ref, ts_ref, g2h, g2l, o_ref):
    x = _fft_fwd(jnp.swapaxes(h_ref[...], 1, 2), (f1h[...], f1l[...]), tc_ref[...], ts_ref[...],
                 (g2h[...], g2l[...]), True)
    o_ref[...] = x.reshape(o_ref.shape)


def _hyena_spectra(full, tabs):
    nch, n1, _ = full.shape
    kept = _kept_rows(n1)
    cb = HY_CH_BLOCK
    consts = (*tabs["f1"], tabs["tc"], tabs["ts"], *tabs["g2"])
    return pl.pallas_call(
        _spectrum_kernel,
        grid=(nch // cb,),
        in_specs=[pl.BlockSpec((cb, n1, LANES), lambda c: (c, 0, 0))] + [_full_spec(t) for t in consts],
        out_specs=pl.BlockSpec((cb, kept, 2 * LANES), lambda c: (c, 0, 0)),
        out_shape=jax.ShapeDtypeStruct((nch, kept, 2 * LANES), F32),
        compiler_params=_cparams("parallel"),
        name="hyena_filter_spectrum",
    )(full, *consts)


def _hyena_kernel(uv_ref, u1_ref, u2_ref, cv_ref, c1_ref, c2_ref, skip_ref, k0_ref, k1_ref,
                  f1_ref, tc_ref, ts_ref, g2_ref, gi2_ref, tct_ref, tst_ref, gi1_ref, o_ref):
    cb, h1, n2 = uv_ref.shape[1:]
    n1 = 2 * h1
    kept = k0_ref.shape[1]
    rows = cb * h1
    lane = lax.broadcasted_iota(jnp.int32, (rows, n2), 1)
    row_n1 = lax.broadcasted_iota(jnp.int32, (rows, n2), 0) % h1
    lane_first, lane_last = lane == 0, lane == n2 - 1
    seq_first = lane_first & (row_n1 == 0)
    seq_last = lane_last & (row_n1 == h1 - 1)

    def short_conv(u_ref, c_ref):
        u = u_ref[0]
        u2 = u.reshape(rows, n2)
        r = pltpu.roll(u2, 1, 1)
        prev = jnp.where(seq_first, 0.0, jnp.where(lane_first, pltpu.roll(r, 1, 0), r))
        r = pltpu.roll(u2, n2 - 1, 1)
        nxt = jnp.where(seq_last, 0.0, jnp.where(lane_last, pltpu.roll(r, rows - 1, 0), r))
        c = c_ref[...]
        return (prev.reshape(cb, h1, n2) * c[:, 0:1, :] + u * c[:, 1:2, :] + nxt.reshape(cb, h1, n2) * c[:, 2:3, :]
                + c[:, 3:4, :])

    tabs_f = ((f1_ref[...], None), tc_ref[...], ts_ref[...], (g2_ref[...], None))
    tct, tst = tct_ref[...], tst_ref[...]
    zero_rows = jnp.zeros((cb, n1 - kept, n2), F32)

    def fftconv(u, k_ref):
        x = _fft_fwd(jnp.swapaxes(u, 1, 2), *tabs_f, False)
        kf = k_ref[...].reshape(cb * kept, 2 * n2)
        xr, xi, kr, ki = x[:, :n2], x[:, n2:], kf[:, :n2], kf[:, n2:]
        y = jnp.concatenate([xr * kr - xi * ki, xr * ki + xi * kr], axis=-1)
        d = _dot(y.astype(BF16), gi2_ref[...])
        dr, di = d[:, :n2].reshape(cb, kept, n2), d[:, n2:].reshape(cb, kept, n2)
        er = jnp.concatenate([dr * tct - di * tst, zero_rows], axis=1)
        ei = jnp.concatenate([dr * tst + di * tct, zero_rows], axis=1)
        e = jnp.concatenate([jnp.swapaxes(er, 1, 2), jnp.swapaxes(ei, 1, 2)], axis=-1)
        out = _dot(e.reshape(cb * n2, 2 * n1).astype(BF16), gi1_ref[...])
        return jnp.swapaxes(out.reshape(cb, n2, h1), 1, 2)

    skip = skip_ref[...]
    v = short_conv(uv_ref, cv_ref)
    conv1 = fftconv(v, k0_ref)
    y1 = short_conv(u1_ref, c1_ref) * (conv1 + v * skip[:, 0:1, :])
    conv2 = fftconv(y1, k1_ref)
    y2 = short_conv(u2_ref, c2_ref) * (conv2 + y1 * skip[:, 1:2, :])
    o_ref[0] = y2.reshape(cb, h1 * n2)


def _hyena(u, conv_w, conv_b, skip, spectra, tabs):
    bsz, _, h1, _ = u.shape
    ch = GROUP_W
    n1 = 2 * h1
    cw =jnp.concatenate([conv_w.astype(F32), conv_b.astype(F32)[None]], axis=0).T
    cw = jnp.broadcast_to(cw[:, :, None], (3 * ch, 4, LANES))
    sk = jnp.broadcast_to(skip.astype(F32).T[:, :, None], (ch, HY_ORDER, LANES))
    cb = HY_CH_BLOCK
    nblk = ch // cb
    kept = _kept_rows(n1)
    consts = (tabs["f1"][0][:h1], tabs["tc"], tabs["ts"], tabs["g2"][0], tabs["gi2"][0], tabs["tct"][:kept],
              tabs["tst"][:kept], tabs["gi1"][0][:, :h1])
    u_spec = lambda g: pl.BlockSpec((1, cb, h1, LANES), lambda c, b, g=g: (b, c + g * nblk, 0, 0))
    c_spec = lambda g: pl.BlockSpec((cb, 4, LANES), lambda c, b, g=g: (c + g * nblk, 0, 0))
    k_spec = lambda o: pl.BlockSpec((cb, kept, 2 * LANES), lambda c, b, o=o: (c + o * nblk, 0, 0))
    y = pl.pallas_call(
        _hyena_kernel,
        grid=(nblk, bsz),
        in_specs=[u_spec(0), u_spec(1), u_spec(2), c_spec(0), c_spec(1), c_spec(2),
                  pl.BlockSpec((cb, HY_ORDER, LANES), lambda c, b: (c, 0, 0)), k_spec(0), k_spec(1)]
                 + [_full_spec(t) for t in consts],
        out_specs=pl.BlockSpec((1, cb, h1 * LANES), lambda c, b: (b, c, 0)),
        out_shape=jax.ShapeDtypeStruct((bsz, ch, h1 * LANES), F32),
        compiler_params=_cparams("parallel", "parallel"),
        name="hyena",
    )(u, u, u, cw, cw, cw, sk, spectra, spectra, *consts)
    return y


def _lane_scan(x, c, reverse, op):
    n = x.shape[-1]
    ax = x.ndim - 1
    pos = lax.broadcasted_iota(jnp.int32, x.shape, ax) % c
    k = 1
    while k < c:
        if reverse:
            x = jnp.where(pos < c - k, op(x, pltpu.roll(x, n - k, ax)), x)
        else:
            x = jnp.where(pos >= k, op(x, pltpu.roll(x, k, ax)), x)
        k *= 2
    return x


ML_STAT_LANES = 16


def _mlstm_direction(d, q_ref, v_ref, kt_ref, gr, out_ref, s, m0, chunks):
    c = ML_CHUNK
    n = chunks * c
    nh = N_HEADS
    reverse = d == 1
    b, row, cmax = (gr[(3 * d + i) * nh:(3 * d + i + 1) * nh, :] for i in range(3))
    order = list(range(chunks - 1, -1, -1) if reverse else range(chunks))

    m_in, m_top, d_old = {}, {}, {}
    m = m0
    for k in order:
        edge = k * c if reverse else (k + 1) * c - 1
        m_in[k] = m
        m_top[k] = jnp.maximum(m, cmax[:, edge:edge + 1])
        d_old[k] = jnp.exp(m - m_top[k])
        m = b[:, edge:edge + 1] + m_top[k]
    m_in_row = jnp.concatenate([jnp.broadcast_to(m_in[k], (nh, c)) for k in range(chunks)], axis=1)
    m_top_row = jnp.concatenate([jnp.broadcast_to(m_top[k], (nh, c)) for k in range(chunks)], axis=1)
    mx = jnp.maximum(m_in_row, cmax)
    wi = jnp.exp(m_in_row - mx)
    einv = jnp.exp(-b - mx)
    w = jnp.exp(row - m_top_row)
    zeros4 = jnp.zeros_like(mx)
    stat_a = jnp.transpose(jnp.concatenate([mx, wi, zeros4, zeros4], axis=0))
    stat_b = jnp.transpose(jnp.concatenate([zeros4, einv, zeros4, zeros4], axis=0))

    jj = lax.broadcasted_iota(jnp.int32, (c, c), 0)
    ss = lax.broadcasted_iota(jnp.int32, (c, c), 1)
    causal = (ss >= jj) if reverse else (ss <= jj)
    lane_head = lax.broadcasted_iota(jnp.int32, (c, GROUP_W), 1) // HEAD_DIM
    row_head = lax.broadcasted_iota(jnp.int32, (GROUP_W, c), 0) // HEAD_DIM
    stat_lane = lax.broadcasted_iota(jnp.int32, (c, ML_STAT_LANES), 1)
    head_lanes = (stat_lane >= nh) & (stat_lane < 2 * nh)
    ones_bd = (lax.broadcasted_iota(jnp.int32, (nh * c, LANES), 0) // c + nh
               == lax.broadcasted_iota(jnp.int32, (nh * c, LANES), 1)).astype(BF16)
    expand = (lax.broadcasted_iota(jnp.int32, (ML_STAT_LANES, GROUP_W), 0) - nh
              == lax.broadcasted_iota(jnp.int32, (ML_STAT_LANES, GROUP_W), 1) // HEAD_DIM).astype(BF16)
    s_rh = lax.broadcasted_iota(jnp.int32, (GROUP_W, GROUP_W + LANES), 0) // HEAD_DIM
    s_col = lax.broadcasted_iota(jnp.int32, (GROUP_W, GROUP_W + LANES), 1)
    s_mask = jnp.where(s_col < GROUP_W, s_col // HEAD_DIM, s_col - GROUP_W - nh) == s_rh
    ones_cols = jnp.ones((c, LANES), BF16)

    for k in order:
        tok = slice(k * c, (k + 1) * c)
        qc, vc, ktc = q_ref[0, tok, :], v_ref[0, tok, :], kt_ref[0, :, tok]
        k_bd = jnp.concatenate([jnp.where(row_head == h, ktc, jnp.zeros_like(ktc)) for h in range(nh)], axis=1)
        v_bd = jnp.concatenate([jnp.where(lane_head == h, vc, jnp.zeros_like(vc)) for h in range(nh)], axis=0)
        qk = _dot(qc, k_bd)
        sa = stat_a[tok, :]
        p = jnp.concatenate(
            [jnp.exp(jnp.where(causal, row[h:h + 1, tok] - sa[:, h:h + 1], NEG)) * qk[:, h * c:(h + 1) * c]
             for h in range(nh)], axis=1).astype(BF16)
        pv = _dot(p, v_bd)
        p_sum = _dot(p, ones_bd)[:, :ML_STAT_LANES]
        qs = _dot(qc, s.astype(BF16))
        den = sa * qs[:, GROUP_W:GROUP_W + ML_STAT_LANES] + p_sum
        rden = jnp.where(head_lanes, 1.0 / jnp.maximum(jnp.abs(den), stat_b[tok, :]), 0.0)
        wi_c = jnp.where(head_lanes, sa, 0.0)
        ex = _dot(jnp.concatenate([wi_c, rden], axis=0).astype(BF16), expand)
        out_ref[0, tok, :] = (ex[:c] * qs[:, :GROUP_W] + pv) * ex[c:]

        w_full = jnp.concatenate([jnp.broadcast_to(w[h:h + 1, tok], (HEAD_DIM, c)) for h in range(nh)], axis=0)
        d_full = jnp.concatenate([jnp.broadcast_to(d_old[k][h:h + 1, :], (HEAD_DIM, 1)) for h in range(nh)], axis=0)
        ktw = (ktc.astype(F32) * w_full).astype(BF16)
        s_loc = _dot(ktw, jnp.concatenate([vc, ones_cols], axis=1))
        s = d_full * s + jnp.where(s_mask, s_loc, 0.0)
    return s, m


def _mlstm_kernel(qf, vf, ktf, grf, qb, vb, ktb, grb, hf_ref, hb_ref, s_scr, m_scr, *, chunks):
    @pl.when(pl.program_id(1) == 0)
    def _():
        s_scr[...] = jnp.zeros_like(s_scr)
        m_scr[...] = jnp.zeros_like(m_scr)

    for d, (q_ref, v_ref, kt_ref, gr_ref, out_ref) in enumerate(((qf, vf, ktf, grf, hf_ref),
                                                                  (qb, vb, ktb, grb, hb_ref))):
        m0 = m_scr[d * SUBLANES:d * SUBLANES + N_HEADS, :][:, :1]
        s, m = _mlstm_direction(d, q_ref, v_ref, kt_ref, gr_ref[0], out_ref, s_scr[d], m0, chunks)
        s_scr[d] = s
        m_scr[d * SUBLANES:d * SUBLANES + N_HEADS, :] = jnp.broadcast_to(m, (N_HEADS, LANES))


def _mlstm(qd, vd, kdt, grow):
    bsz, seq, _ = qd.shape
    g = min(ML_CHUNKS_PER_STEP, seq // ML_CHUNK)
    blk = g * ML_CHUNK
    nb = seq // blk
    fwd = lambda b, i: (b, i, 0)
    bwd = lambda b, i: (b, nb - 1 - i, 0)
    fwd_t = lambda b, i: (b, 0, i)
    bwd_t = lambda b, i: (b, 0, nb - 1 - i)

    def specs(tok, chan):
        return [pl.BlockSpec((1, blk, GROUP_W), tok), pl.BlockSpec((1, blk, GROUP_W), tok),
                pl.BlockSpec((1, GROUP_W, blk), chan), pl.BlockSpec((1, ML_STAT_ROWS, blk), chan)]

    args = (qd, vd, kdt, grow)
    return pl.pallas_call(
        functools.partial(_mlstm_kernel, chunks=g),
        grid=(bsz, nb),
        in_specs=specs(fwd, fwd_t) + specs(bwd, bwd_t),
        out_specs=[pl.BlockSpec((1, blk, GROUP_W), fwd), pl.BlockSpec((1, blk, GROUP_W), bwd)],
        out_shape=[jax.ShapeDtypeStruct((bsz, seq, GROUP_W), F32)] * 2,
        scratch_shapes=[pltpu.VMEM((2, GROUP_W, GROUP_W + LANES), F32), pltpu.VMEM((2 * SUBLANES, LANES), F32)],
        compiler_params=_cparams("parallel", "arbitrary"),
        name="mlstm",
    )(*args, *args)


def _post_kernel(x_ref, ya_ref, f_ref, yc_ref, hf_ref, hb_ref, od_ref, p_ref,
                 wfn_ref, onorm_ref, gsum_ref, gbc_ref, wout_ref, nffn_ref, wgate_ref, wup_ref, wdown_ref,
                 pnorm_ref, wpg_ref, wpp_ref, fnorm_ref, o_ref, *, final):
    yb = _dot(jnp.transpose(f_ref[0]).astype(BF16), wfn_ref[...])
    yd = _sigmoid(od_ref[0]) * (hf_ref[0] + hb_ref[0])
    y = jnp.concatenate([ya_ref[0], yb, jnp.transpose(yc_ref[0]), yd], axis=-1)
    ss = _dot((y * y).astype(BF16), gsum_ref[...])
    rb = _dot(lax.rsqrt(ss * (1.0 / HEAD_DIM) + EPS).astype(BF16), gbc_ref[...])
    x = x_ref[0] + _dot((y * rb * onorm_ref[...]).astype(BF16), wout_ref[...])
    hn = _rms(x, nffn_ref[...]).astype(BF16)
    act = []
    for c in range(D_FF // FF_CHUNK):
        sl = slice(c * FF_CHUNK, (c + 1) * FF_CHUNK)
        g = _dot(hn, wgate_ref[:, sl])
        u = _dot(hn, wup_ref[:, sl])
        act.append((g * _sigmoid(g) * u).astype(BF16))
    x = x + _dot(jnp.concatenate(act, axis=1), wdown_ref[...])
    gate = _sigmoid(_dot(_rms(x, pnorm_ref[...]).astype(BF16), wpg_ref[...]))
    x = x + gate * _dot(p_ref[0, 0].astype(BF16), wpp_ref[...])
    if final:
        x = _rms(x, fnorm_ref[...])
    o_ref[0] = x


def _post(x, ya, f, yc, hf, hb, od, p, layer, lw, final_norm, final):
    bsz, seq, _ = x.shape
    tm = min(TOKEN_TILE, seq)
    ind = np.zeros((D_MODEL, LANES), np.float32)
    ind[np.arange(D_MODEL), np.arange(D_MODEL) // HEAD_DIM] = 1.0
    gsum = jnp.asarray(ind, BF16)
    gbc = jnp.asarray(ind.T, BF16)
    fw = lw["fnet_w"].astype(F32)
    wfn = jax.scipy.linalg.block_diag(*[fw[g] for g in range(fw.shape[0])]).astype(BF16)
    row = lambda v: v.astype(F32).reshape(1, D_MODEL)
    weights = (wfn, row(lw["out_norm"]), gsum, gbc, lw["w_out"].astype(BF16), row(lw["norm_ffn"]),
               lw["w_gate"].astype(BF16), lw["w_up"].astype(BF16), lw["w_down"].astype(BF16),
               row(lw["ple_norm"]), lw["w_ple_gate"].astype(BF16), lw["w_ple_proj"].astype(BF16), row(final_norm))
    tok = lambda width: pl.BlockSpec((1, tm, width), lambda b, t: (b, t, 0))
    chan = pl.BlockSpec((1, GROUP_W, tm), lambda b, t: (b, 0, t))
    return pl.pallas_call(
        functools.partial(_post_kernel, final=final),
        grid=(bsz, seq // tm),
        in_specs=[tok(D_MODEL), tok(GROUP_W), chan, chan] + [tok(GROUP_W)] * 3
                 + [pl.BlockSpec((1, 1, tm, PLE_DIM), lambda b, t: (layer, b, t, 0))]
                 + [_full_spec(w) for w in weights],
        out_specs=tok(D_MODEL),
        out_shape=jax.ShapeDtypeStruct((bsz, seq, D_MODEL), F32),
        compiler_params=_cparams("parallel", "parallel"),
        name="post",
    )(x, ya, f, yc, hf, hb, od, p, *weights)


def _layer_consts(lw, seq, hy_tabs):
    full = _hyena_filter_taps(seq, lw["hy_w1"], lw["hy_b1"], lw["hy_freq"], lw["hy_w2"], lw["hy_b2"], lw["hy_w3"],
                              lw["hy_decay"])
    return dict(spectra=_hyena_spectra(full, hy_tabs), na_bias=_na_bias_table(lw["attn_rpb"]))


def _trunk(x, p, layers, consts, final_norm, fn_tabs, hy_tabs):
    for i, (lw, lc) in enumerate(zip(layers, consts)):
        (qa, ka, va, zr, zi, uc, qd, vd, od, grow, kdt) = _inproj(
            x, lw["norm_mix"], lw["w_in"], lw["ml_gate_b"], (fn_tabs["fc"], fn_tabs["fs"]))
        ya = _na(qa, ka, va, lc["na_bias"])
        f = _fnet(zr, zi, fn_tabs)
        yc = _hyena(uc, lw["hy_conv_w"], lw["hy_conv_b"], lw["hy_skip"], lc["spectra"], hy_tabs)
        hf, hb = _mlstm(qd, vd, kdt, grow)
        x = _post(x, ya, f, yc, hf, hb, od, p, i, lw, final_norm, final=(i == len(layers) - 1))
    return x


_LAYER_KEYS = ("norm_mix", "w_in", "attn_rpb", "fnet_w", "hy_conv_w", "hy_conv_b", "hy_w1", "hy_b1", "hy_freq",
               "hy_w2", "hy_b2", "hy_w3", "hy_decay", "hy_skip", "ml_gate_b", "out_norm", "w_out", "norm_ffn",
               "w_gate", "w_up", "w_down", "ple_norm", "w_ple_gate", "w_ple_proj")


def kernel(x_prompt, x_sample, p_prompt, p_sample, norm_mix, w_in, attn_rpb, fnet_w, hy_conv_w, hy_conv_b, hy_w1,
           hy_b1, hy_freq, hy_w2, hy_b2, hy_w3, hy_decay, hy_skip, ml_gate_b, out_norm, w_out, norm_ffn, w_gate,
           w_up, w_down, ple_norm, w_ple_gate, w_ple_proj, final_norm):
    stacked = dict(zip(_LAYER_KEYS, (norm_mix, w_in, attn_rpb, fnet_w, hy_conv_w, hy_conv_b, hy_w1, hy_b1, hy_freq,
                                     hy_w2, hy_b2, hy_w3, hy_decay, hy_skip, ml_gate_b, out_norm, w_out, norm_ffn,
                                     w_gate, w_up, w_down, ple_norm, w_ple_gate, w_ple_proj)))
    depth = norm_mix.shape[0]
    layers = [{k: v[i] for k, v in stacked.items()} for i in range(depth)]
    outs = []
    cache = {}
    for x, p in ((x_prompt, p_prompt), (x_sample, p_sample)):
        seq = x.shape[1]
        if seq not in cache:
            fn_tabs = _fnet_tables(seq)
            hy_tabs = _hyena_tables(seq)
            cache[seq] = (fn_tabs, hy_tabs, [_layer_consts(lw, seq, hy_tabs) for lw in layers])
        fn_tabs, hy_tabs, consts = cache[seq]
        outs.append(_trunk(x, p, layers, consts, final_norm, fn_tabs, hy_tabs))
    return tuple(outs)
```

```python
import functools
import math

import numpy as np
import jax
import jax.numpy as jnp
from jax import lax
from jax.experimental import pallas as pl
from jax.experimental.pallas import tpu as pltpu

F32, BF16 = jnp.float32, jnp.bfloat16

D_MODEL = 1024
GRID_W = 64
HEAD_DIM = 64
GROUP_W = 256
N_HEADS = 4
KH = 8
KW = 16
HY_ORDER = 2
HY_EMB = 33
HY_BANDS = 16
ML_CHUNK = 128
PLE_DIM = 256
D_FF = 2816
EPS = 1e-6
QK_SCALE = HEAD_DIM ** -0.5
NEG = -1e30
N_GATES = 4 * N_HEADS
ML_STAT_ROWS = 6 * N_HEADS

LANES = 128
SUBLANES = 8
VMEM_LIMIT = 56 * 1024 * 1024
TOKEN_TILE = 512
INPROJ_TILE = 1024
FF_CHUNK = 256
NA_ROWS_PER_STEP = 32
HY_CH_BLOCK = 64
FN_CH_BLOCK = 128
FILT_TILE = 1024
ML_CHUNKS_PER_STEP = 16


def _cparams(*sem):
    return pltpu.CompilerParams(dimension_semantics=sem, vmem_limit_bytes=VMEM_LIMIT)


def _dot(a, b):
    return jnp.dot(a, b, preferred_element_type=F32)


def _dot_nt(a, b):
    return lax.dot_general(a, b, (((1,), (1,)), ((), ())), preferred_element_type=F32)


def _split2(x):
    hi = x.astype(BF16)
    return hi, (x - hi.astype(F32)).astype(BF16)


def _rms(x, g):
    return x * lax.rsqrt(jnp.mean(x * x, axis=-1, keepdims=True) + EPS) * g


def _sigmoid(x):
    return 1.0 / (1.0 + jnp.exp(-x))


def _full_spec(a):
    nd = a.ndim
    return pl.BlockSpec(a.shape, lambda *_: (0,) * nd, pipeline_mode=pl.Buffered(1))


def _inproj_kernel(x_ref, g_ref, wab_ref, wcg_ref, wd_ref, fc_ref, fs_ref, gbr_ref,
                   qa_ref, ka_ref, va_ref, zr_ref, zi_ref, uc_ref, qd_ref, vd_ref, od_ref, grow_ref, kdt_ref):
    xn = _rms(x_ref[0], g_ref[...]).astype(BF16)
    cg = _dot(xn, wcg_ref[...])
    gates = jnp.transpose(cg[:, 3 * GROUP_W:3 * GROUP_W + LANES])[:N_GATES, :] + gbr_ref[...]
    stats = []
    for rev in range(2):
        li = gates[2 * N_HEADS * rev:2 * N_HEADS * rev + N_HEADS, :]
        lf = jax.nn.log_sigmoid(gates[2 * N_HEADS * rev + N_HEADS:2 * N_HEADS * (rev + 1), :])
        b = _lane_scan(lf, ML_CHUNK, rev == 1, jnp.add)
        stats += [b, li - b, _lane_scan(li - b, ML_CHUNK, rev == 1, jnp.maximum)]
    grow_ref[0] = jnp.concatenate(stats, axis=0)
    blocks = [slice(j * LANES, (j + 1) * LANES) for j in range(x_ref.shape[1] // LANES)]
    for j, blk in enumerate(blocks):
        uc_ref[0, :, j, :] = jnp.transpose(cg[blk, :3 * GROUP_W])
    ab = _dot(xn, wab_ref[...])
    ub = ab[:, 3 * GROUP_W:].astype(BF16)
    zr = _dot(ub, fc_ref[...])
    zi = _dot(ub, fs_ref[...])
    for j, blk in enumerate(blocks):
        zr_ref[0, :, j, :] = jnp.transpose(zr[blk, :])
        zi_ref[0, :, j, :] = jnp.transpose(zi[blk, :])
    qa_ref[0] = (ab[:, :GROUP_W] * QK_SCALE).astype(BF16)
    ka_ref[0] = ab[:, GROUP_W:2 * GROUP_W].astype(BF16)
    va_ref[0] = ab[:, 2 * GROUP_W:3 * GROUP_W].astype(BF16)
    d = _dot(xn, wd_ref[...])
    for blk in blocks:
        kdt_ref[0, :, blk] = jnp.transpose(d[blk, GROUP_W:2 * GROUP_W] * QK_SCALE).astype(BF16)
    qd_ref[0] = d[:, :GROUP_W].astype(BF16)
    vd_ref[0] = d[:, 2 * GROUP_W:3 * GROUP_W].astype(BF16)
    od_ref[0] = d[:, 3 * GROUP_W:]


def _inproj(x, g, w_in, gate_b, fcs):
    bsz, seq, _ = x.shape
    tm = min(INPROJ_TILE, seq)
    wb16 = w_in.astype(BF16)
    wab = wb16[:, :4 * GROUP_W]
    wc = wb16[:, 4 * GROUP_W:7 * GROUP_W]
    wd = wb16[:, 7 * GROUP_W:11 * GROUP_W]
    wg = wb16[:, 11 * GROUP_W:11 * GROUP_W + N_GATES]
    wcg = jnp.concatenate([wc, jnp.pad(wg, ((0, 0), (0, GROUP_W - N_GATES)))], axis=1)
    gbr = gate_b.astype(F32).reshape(N_GATES, 1)
    fc, fs = fcs
    weights = (g.astype(F32).reshape(1, D_MODEL), wab, wcg, wd, fc, fs, gbr)

    def tok(width, dtype):
        return jax.ShapeDtypeStruct((bsz, seq, width), dtype), pl.BlockSpec((1, tm, width), lambda b, t: (b, t, 0))

    def chan(height, dtype):
        return jax.ShapeDtypeStruct((bsz, height, seq), dtype), pl.BlockSpec((1, height, tm), lambda b, t: (b, 0, t))

    def chan_tiles(height):
        return (jax.ShapeDtypeStruct((bsz, height, seq // LANES, LANES), F32),
                pl.BlockSpec((1, height, tm // LANES, LANES), lambda b, t: (b, 0, t, 0)))

    outs = [tok(GROUP_W, BF16), tok(GROUP_W, BF16), tok(GROUP_W, BF16),
            chan_tiles(GROUP_W), chan_tiles(GROUP_W), chan_tiles(3 * GROUP_W),
            tok(GROUP_W, BF16), tok(GROUP_W, BF16), tok(GROUP_W, F32),
            chan(ML_STAT_ROWS, F32), chan(GROUP_W, BF16)]
    return pl.pallas_call(
        _inproj_kernel,
        grid=(bsz, seq // tm),
        in_specs=[pl.BlockSpec((1, tm, D_MODEL), lambda b, t: (b, t, 0))] + [_full_spec(w) for w in weights],
        out_specs=[s for _, s in outs],
        out_shape=[s for s, _ in outs],
        compiler_params=_cparams("parallel", "parallel"),
        name="inproj",
    )(x, *weights)


def _na_kernel(q_ref, k_ref, v_ref, bias_ref, o_ref, *, rows, rows_per_step):
    i = pl.program_id(1)
    lane_head = lax.broadcasted_iota(jnp.int32, (GRID_W, GROUP_W), 1) // HEAD_DIM
    masks = [lane_head == h for h in range(N_HEADS)]
    for j in range(rows_per_step):
        r = i * rows_per_step + j
        kr0 = jnp.clip(r - KH // 2, 0, rows - KH)
        case = r - kr0
        ks = pl.multiple_of(kr0 * GRID_W, GRID_W)
        q = q_ref[0, j * GRID_W:(j + 1) * GRID_W, :]
        qs = jnp.concatenate([jnp.where(m, q, jnp.zeros_like(q)) for m in masks], axis=0)
        kw = k_ref[0, pl.ds(ks, KH * GRID_W), :]
        vw = v_ref[0, pl.ds(ks, KH * GRID_W), :]
        s = _dot_nt(qs, kw) + bias_ref[case]
        p = jnp.exp(s - jnp.max(s, axis=-1, keepdims=True))
        den = jnp.sum(p, axis=-1, keepdims=True)
        o = _dot(p.astype(BF16), vw) * (1.0 / den)
        out = jnp.zeros((GRID_W, GROUP_W), F32)
        for h in range(N_HEADS):
            out = out + jnp.where(masks[h], o[h * GRID_W:(h + 1) * GRID_W, :], 0.0)
        o_ref[0, j * GRID_W:(j + 1) * GRID_W, :] = out


def _na_bias_table(rpb):
    c = np.arange(GRID_W)
    kc = np.arange(GRID_W)
    kc0 = np.clip(c - KW // 2, 0, GRID_W - KW)
    valid = (kc[None, :] >= kc0[:, None]) & (kc[None, :] < kc0[:, None] + KW)
    dc = np.clip(kc[None, :] - c[:, None] + KW - 1, 0, 2 * KW - 2)
    case = np.arange(KH)
    j = np.arange(KH)
    dr = j[None, :] - case[:, None] + KH - 1
    row_sel = np.zeros((KH * KH, 2 * KH - 1), np.float32)
    row_sel[np.arange(KH * KH), dr.reshape(-1)] = 1.0
    col_sel = np.zeros((2 * KW - 1, GRID_W * GRID_W), np.float32)
    col_sel[dc.reshape(-1), np.arange(GRID_W * GRID_W)] = 1.0
    hp = lax.Precision.HIGHEST
    tab = jnp.einsum("rd,hde->hre", jnp.asarray(row_sel), rpb.astype(F32), precision=hp)
    tab = jnp.einsum("hre,ex->hrx", tab, jnp.asarray(col_sel), precision=hp)
    tab = tab.reshape(N_HEADS, KH, KH, GRID_W, GRID_W).transpose(1, 0, 3, 2, 4)
    tab = jnp.where(jnp.asarray(valid)[None, None, :, None, :], tab, NEG)
    return tab.reshape(KH, N_HEADS * GRID_W, KH * GRID_W)


def _na(qa, ka, va, bias):
    bsz, seq, _ = qa.shape
    rows = seq // GRID_W
    rb = min(NA_ROWS_PER_STEP, rows)
    assert rows >= KH and rows % rb == 0
    return pl.pallas_call(
        functools.partial(_na_kernel, rows=rows, rows_per_step=rb),
        grid=(bsz, rows // rb),
        in_specs=[pl.BlockSpec((1, rb * GRID_W, GROUP_W), lambda b, i: (b, i, 0)),
                  pl.BlockSpec((1, seq, GROUP_W), lambda b, i: (b, 0, 0)),
                  pl.BlockSpec((1, seq, GROUP_W), lambda b, i: (b, 0, 0)),
                  _full_spec(bias)],
        out_specs=pl.BlockSpec((1, rb * GRID_W, GROUP_W), lambda b, i: (b, i, 0)),
        out_shape=jax.ShapeDtypeStruct((bsz, seq, GROUP_W), F32),
        compiler_params=_cparams("parallel", "arbitrary"),
        name="nbr_attn",
    )(qa, ka, va, bias)


def _cs(num, den):
    ang = 2.0 * np.pi * (np.asarray(num, np.float64) % den) / den
    return np.cos(ang), np.sin(ang)


def _hilo(m):
    m32 = jnp.asarray(m, F32)
    hi = m32.astype(BF16)
    return hi, (m32 - hi.astype(F32)).astype(BF16)


def _fnet_tables(seq):
    n1f = seq // LANES
    c, s = _cs(np.outer(np.arange(HEAD_DIM), np.arange(HEAD_DIM)), HEAD_DIM)
    norm = 1.0 / math.sqrt(HEAD_DIM * seq)
    eye = np.eye(GROUP_W // HEAD_DIM)
    fc = np.kron(eye, c) * norm
    fs = np.kron(eye, -s) * norm
    c1, s1 = _cs(np.outer(np.arange(n1f), np.arange(n1f)), n1f)
    m1 = np.block([[c1, -s1], [s1, c1]])
    ct, st = _cs(np.outer(np.arange(LANES), np.arange(n1f)), seq)
    t1 = np.concatenate([ct, ct], axis=1)
    t2 = np.concatenate([st, -st], axis=1)
    c2, s2 = _cs(np.outer(np.arange(LANES), np.arange(LANES)), LANES)
    g2 = np.concatenate([c2, s2], axis=0)
    q = LANES // n1f
    g2 = np.stack([g2[:, a::q] for a in range(q)], axis=0)
    return dict(fc=jnp.asarray(fc, F32).astype(BF16), fs=jnp.asarray(fs, F32).astype(BF16),
                m1=jnp.asarray(m1, F32).astype(BF16), t1=jnp.asarray(t1, F32), t2=jnp.asarray(t2, F32),
                g2=jnp.asarray(g2, F32).astype(BF16))


def _hyena_tables(seq):
    n = 2 * seq
    n1 = n // LANES
    c1, s1 = _cs(np.outer(np.arange(n1), np.arange(n1)), n1)
    f1 = np.concatenate([c1, -s1], axis=1)
    ct, st = _cs(np.outer(np.arange(LANES), np.arange(n1)), n)
    c2, s2 = _cs(np.outer(np.arange(LANES), np.arange(LANES)), LANES)
    g2 = np.block([[c2, -s2], [s2, c2]])
    gi2 = np.block([[c2, s2], [-s2, c2]])
    fold = np.where(np.arange(n1) % (n1 // 2) == 0, 1.0, 2.0) * (np.arange(n1) <= n1 // 2)
    gi1 = np.concatenate([c1 * fold[:, None], -s1 * fold[:, None]], axis=0) / n
    gi1[:, n1 // 2:] = 0.0
    return dict(f1=_hilo(f1), tc=jnp.asarray(ct, F32), ts=jnp.asarray(st, F32),
                tct=jnp.asarray(ct.T, F32), tst=jnp.asarray(st.T, F32),
                g2=_hilo(g2), gi2=_hilo(gi2), gi1=_hilo(gi1))


def _mm(a, tab, precise):
    hi, lo = tab
    if not precise:
        return _dot(a.astype(BF16), hi)
    a_hi, a_lo = _split2(a)
    return _dot(a_hi, hi) + (_dot(a_lo, hi) + _dot(a_hi, lo))


def _fnet_kernel(zr_ref, zi_ref, m1_ref, t1_ref, t2_ref, g2_ref, o_ref):
    cb, n1f, n2 = zr_ref.shape[1:]
    w = 2 * n1f
    z = jnp.swapaxes(jnp.concatenate([zr_ref[0], zi_ref[0]], axis=1), 1, 2)
    a = _dot(z.reshape(cb * n2, w).astype(BF16), m1_ref[...])
    sw = pltpu.roll(a, n1f, 1).reshape(cb, n2, w)
    a = a.reshape(cb, n2, w) * t1_ref[...] + sw * t2_ref[...]
    at = jnp.swapaxes(a, 1, 2)
    op = jnp.concatenate([at[:, :n1f, :], at[:, n1f:, :]], axis=-1)
    op = op.reshape(cb * n1f, 2 * n2).astype(BF16)
    xt = jnp.concatenate([_dot(op, g2_ref[g]).reshape(cb, n1f, n1f) for g in range(g2_ref.shape[0])], axis=1)
    o_ref[0] = jnp.swapaxes(xt, 1, 2).reshape(cb, n1f * n2)


def _fnet(zr, zi, tabs):
    bsz, ch, n1f, _ = zr.shape
    cb = FN_CH_BLOCK
    consts = (tabs["m1"], tabs["t1"], tabs["t2"], tabs["g2"])
    z_spec = pl.BlockSpec((1, cb, n1f, LANES), lambda b, c: (b, c, 0, 0))
    return pl.pallas_call(
        _fnet_kernel,
        grid=(bsz, ch // cb),
        in_specs=[z_spec, z_spec] + [_full_spec(t) for t in consts],
        out_specs=pl.BlockSpec((1, cb, n1f * LANES), lambda b, c: (b, c, 0)),
        out_shape=jax.ShapeDtypeStruct((bsz, ch, n1f * LANES), F32),
        compiler_params=_cparams("parallel", "parallel"),
        name="fourier_mix",
    )(zr, zi, *consts)


def _filter_mlp_kernel(feat_ref, t_ref, w1_ref, b1_ref, f0_ref, w2_ref, b2_ref, f1_ref, w3_ref, dec_ref, o_ref):
    def dot3(a, w):
        return _mm(a, _split2(w), True)

    h = jnp.sin(f0_ref[...] * (dot3(feat_ref[...], w1_ref[...]) + b1_ref[...]))
    h = jnp.sin(f1_ref[...] * (dot3(h, w2_ref[...]) + b2_ref[...]))
    h = dot3(h, w3_ref[0])
    h = h * (jnp.exp(-t_ref[:, 0:1] * dec_ref[0]) * t_ref[:, 1:2])
    for j in range(h.shape[0] // LANES):
        o_ref[:, j, :] = jnp.transpose(h[j * LANES:(j + 1) * LANES, :])


def _hyena_filter_taps(seq, w1, b1, freq, w2, b2, w3, decay):
    ch = decay.shape[-1]
    r = jnp.arange(2 * seq)
    s = jnp.where(r < seq, r, 2 * seq - r).astype(F32)
    t = s / max(seq - 1, 1)
    ang = (2.0 * math.pi / seq) * s
    bands = jnp.linspace(1e-4, HY_BANDS - 1, HY_BANDS, dtype=F32)
    fb = ang[:, None] * bands[None, :]
    feats = jnp.concatenate([t[:, None], jnp.cos(fb), -jnp.sin(fb)], axis=-1)
    t_keep = jnp.stack([t, (r != seq).astype(F32)], axis=1)
    filt = w1.shape[1]
    pad_c = LANES - filt
    feats = jnp.pad(feats, ((0, 0), (0, LANES - HY_EMB)))
    w1p = jnp.pad(w1.astype(F32), ((0, LANES - HY_EMB), (0, pad_c)))
    w2p = jnp.pad(w2.astype(F32), ((0, pad_c), (0, pad_c)))
    w3s = w3.astype(F32).reshape(filt, HY_ORDER, 2, ch).transpose(2, 0, 1, 3).reshape(2, filt, HY_ORDER * ch)
    w3s = jnp.pad(w3s, ((0, 0), (0, pad_c), (0, 0)))
    decs = decay.astype(F32).transpose(1, 0, 2).reshape(2, 1, HY_ORDER * ch)
    row = lambda v: jnp.pad(v.astype(F32), (0, pad_c)).reshape(1, LANES)
    n_out = HY_ORDER * ch
    tl = min(FILT_TILE, seq)
    half = seq // tl
    side = lambda i: (i // half, 0, 0)
    args = (feats, t_keep, w1p, row(b1), row(freq[0]), w2p, row(b2), row(freq[1]), w3s, decs)
    specs = [pl.BlockSpec((tl, LANES), lambda i: (i, 0)), pl.BlockSpec((tl, 2), lambda i: (i, 0))]
    specs += [_full_spec(a) for a in args[2:8]]
    specs += [pl.BlockSpec((1, LANES, n_out), side), pl.BlockSpec((1, 1, n_out), side)]
    return pl.pallas_call(
        _filter_mlp_kernel,
        grid=(2 * half,),
        in_specs=specs,
        out_specs=pl.BlockSpec((n_out, tl // LANES, LANES), lambda i: (0, i, 0)),
        out_shape=jax.ShapeDtypeStruct((n_out, 2 * seq // LANES, LANES), F32),
        compiler_params=_cparams("parallel"),
        name="hyena_filter_mlp",
    )(*args)


def _kept_rows(n1):
    return n1 // 2 + SUBLANES


def _fft_fwd(u, f1, tc, ts, g2, precise):
    cb, n2, n1_in = u.shape
    n1 = f1[0].shape[1] // 2
    kept = _kept_rows(n1)
    a = _mm(u.reshape(cb * n2, n1_in), f1, precise)
    ar = a[:, :n1].reshape(cb, n2, n1)
    ai = a[:, n1:].reshape(cb, n2, n1)
    br = ar * tc + ai * ts
    bi = ai * tc - ar * ts
    op = jnp.concatenate([jnp.swapaxes(br, 1, 2)[:, :kept, :], jnp.swapaxes(bi, 1, 2)[:, :kept, :]], axis=-1)
    return _mm(op.reshape(cb * kept, 2 * n2), g2, precise)


def _spectrum_kernel(h_ref, f1h, f1l, tc_ref, ts_ref, g2h, g2l, o_ref):
    x = _fft_fwd(jnp.swapaxes(h_ref[...], 1, 2), (f1h[...], f1l[...]), tc_ref[...], ts_ref[...],
                 (g2h[...], g2l[...]), True)
    o_ref[...] = x.reshape(o_ref.shape)


def _hyena_spectra(full, tabs):
    nch, n1, _ = full.shape
    kept = _kept_rows(n1)
    cb = HY_CH_BLOCK
    consts = (*tabs["f1"], tabs["tc"], tabs["ts"], *tabs["g2"])
    return pl.pallas_call(
        _spectrum_kernel,
        grid=(nch // cb,),
        in_specs=[pl.BlockSpec((cb, n1, LANES), lambda c: (c, 0, 0))] + [_full_spec(t) for t in consts],
        out_specs=pl.BlockSpec((cb, kept, 2 * LANES), lambda c: (c, 0, 0)),
        out_shape=jax.ShapeDtypeStruct((nch, kept, 2 * LANES), F32),
        compiler_params=_cparams("parallel"),
        name="hyena_filter_spectrum",
    )(full, *consts)


def _hyena_kernel(uv_ref, u1_ref, u2_ref, cv_ref, c1_ref, c2_ref, skip_ref, k0_ref, k1_ref,
                  f1_ref, tc_ref, ts_ref, g2_ref, gi2_ref, tct_ref, tst_ref, gi1_ref, o_ref):
    cb, h1, n2 = uv_ref.shape[1:]
    n1 = 2 * h1
    kept = k0_ref.shape[1]
    rows = cb * h1
    lane = lax.broadcasted_iota(jnp.int32, (rows, n2), 1)
    row_n1 = lax.broadcasted_iota(jnp.int32, (rows, n2), 0) % h1
    lane_first, lane_last = lane == 0, lane == n2 - 1
    seq_first = lane_first & (row_n1 == 0)
    seq_last = lane_last & (row_n1 == h1 - 1)

    def short_conv(u_ref, c_ref):
        u = u_ref[0]
        u2 = u.reshape(rows, n2)
        r = pltpu.roll(u2, 1, 1)
        prev = jnp.where(seq_first, 0.0, jnp.where(lane_first, pltpu.roll(r, 1, 0), r))
        r = pltpu.roll(u2, n2 - 1, 1)
        nxt = jnp.where(seq_last, 0.0, jnp.where(lane_last, pltpu.roll(r, rows - 1, 0), r))
        c = c_ref[...]
        return (prev.reshape(cb, h1, n2) * c[:, 0:1, :] + u * c[:, 1:2, :] + nxt.reshape(cb, h1, n2) * c[:, 2:3, :]
                + c[:, 3:4, :])

    tabs_f = ((f1_ref[...], None), tc_ref[...], ts_ref[...], (g2_ref[...], None))
    tct, tst = tct_ref[...], tst_ref[...]
    zero_rows = jnp.zeros((cb, n1 - kept, n2), F32)

    def fftconv(u, k_ref):
        x = _fft_fwd(jnp.swapaxes(u, 1, 2), *tabs_f, False)
        kf = k_ref[...].reshape(cb * kept, 2 * n2)
        xr, xi, kr, ki = x[:, :n2], x[:, n2:], kf[:, :n2], kf[:, n2:]
        y = jnp.concatenate([xr * kr - xi * ki, xr * ki + xi * kr], axis=-1)
        d = _dot(y.astype(BF16), gi2_ref[...])
        dr, di = d[:, :n2].reshape(cb, kept, n2), d[:, n2:].reshape(cb, kept, n2)
        er = jnp.concatenate([dr * tct - di * tst, zero_rows], axis=1)
        ei = jnp.concatenate([dr * tst + di * tct, zero_rows], axis=1)
        e = jnp.concatenate([jnp.swapaxes(er, 1, 2), jnp.swapaxes(ei, 1, 2)], axis=-1)
        out = _dot(e.reshape(cb * n2, 2 * n1).astype(BF16), gi1_ref[...])
        return jnp.swapaxes(out.reshape(cb, n2, h1), 1, 2)

    skip = skip_ref[...]
    v = short_conv(uv_ref, cv_ref)
    conv1 = fftconv(v, k0_ref)
    y1 = short_conv(u1_ref, c1_ref) * (conv1 + v * skip[:, 0:1, :])
    conv2 = fftconv(y1, k1_ref)
    y2 = short_conv(u2_ref, c2_ref) * (conv2 + y1 * skip[:, 1:2, :])
    o_ref[0] = y2.reshape(cb, h1 * n2)


def _hyena(u, conv_w, conv_b, skip, spectra, tabs):
    bsz, _, h1, _ = u.shape
    ch = GROUP_W
    n1 = 2 * h1
    cw =jnp.concatenate([conv_w.astype(F32), conv_b.astype(F32)[None]], axis=0).T
    cw = jnp.broadcast_to(cw[:, :, None], (3 * ch, 4, LANES))
    sk = jnp.broadcast_to(skip.astype(F32).T[:, :, None], (ch, HY_ORDER, LANES))
    cb = HY_CH_BLOCK
    nblk = ch // cb
    kept = _kept_rows(n1)
    consts = (tabs["f1"][0][:h1], tabs["tc"], tabs["ts"], tabs["g2"][0], tabs["gi2"][0], tabs["tct"][:kept],
              tabs["tst"][:kept], tabs["gi1"][0][:, :h1])
    u_spec = lambda g: pl.BlockSpec((1, cb, h1, LANES), lambda c, b, g=g: (b, c + g * nblk, 0, 0))
    c_spec = lambda g: pl.BlockSpec((cb, 4, LANES), lambda c, b, g=g: (c + g * nblk, 0, 0))
    k_spec = lambda o: pl.BlockSpec((cb, kept, 2 * LANES), lambda c, b, o=o: (c + o * nblk, 0, 0))
    y = pl.pallas_call(
        _hyena_kernel,
        grid=(nblk, bsz),
        in_specs=[u_spec(0), u_spec(1), u_spec(2), c_spec(0), c_spec(1), c_spec(2),
                  pl.BlockSpec((cb, HY_ORDER, LANES), lambda c, b: (c, 0, 0)), k_spec(0), k_spec(1)]
                 + [_full_spec(t) for t in consts],
        out_specs=pl.BlockSpec((1, cb, h1 * LANES), lambda c, b: (b, c, 0)),
        out_shape=jax.ShapeDtypeStruct((bsz, ch, h1 * LANES), F32),
        compiler_params=_cparams("parallel", "parallel"),
        name="hyena",
    )(u, u, u, cw, cw, cw, sk, spectra, spectra, *consts)
    return y


def _lane_scan(x, c, reverse, op):
    n = x.shape[-1]
    ax = x.ndim - 1
    pos = lax.broadcasted_iota(jnp.int32, x.shape, ax) % c
    k = 1
    while k < c:
        if reverse:
            x = jnp.where(pos < c - k, op(x, pltpu.roll(x, n - k, ax)), x)
        else:
            x = jnp.where(pos >= k, op(x, pltpu.roll(x, k, ax)), x)
        k *= 2
    return x


ML_STAT_LANES = 16


def _mlstm_direction(d, q_ref, v_ref, kt_ref, gr, out_ref, s, m0, chunks):
    c = ML_CHUNK
    n = chunks * c
    nh = N_HEADS
    reverse = d == 1
    b, row, cmax = (gr[(3 * d + i) * nh:(3 * d + i + 1) * nh, :] for i in range(3))
    order = list(range(chunks - 1, -1, -1) if reverse else range(chunks))

    m_in, m_top, d_old = {}, {}, {}
    m = m0
    for k in order:
        edge = k * c if reverse else (k + 1) * c - 1
        m_in[k] = m
        m_top[k] = jnp.maximum(m, cmax[:, edge:edge + 1])
        d_old[k] = jnp.exp(m - m_top[k])
        m = b[:, edge:edge + 1] + m_top[k]
    m_in_row = jnp.concatenate([jnp.broadcast_to(m_in[k], (nh, c)) for k in range(chunks)], axis=1)
    m_top_row = jnp.concatenate([jnp.broadcast_to(m_top[k], (nh, c)) for k in range(chunks)], axis=1)
    mx = jnp.maximum(m_in_row, cmax)
    wi = jnp.exp(m_in_row - mx)
    einv = jnp.exp(-b - mx)
    w = jnp.exp(row - m_top_row)
    zeros4 = jnp.zeros_like(mx)
    stat_a = jnp.transpose(jnp.concatenate([mx, wi, zeros4, zeros4], axis=0))
    stat_b = jnp.transpose(jnp.concatenate([zeros4, einv, zeros4, zeros4], axis=0))

    jj = lax.broadcasted_iota(jnp.int32, (c, c), 0)
    ss = lax.broadcasted_iota(jnp.int32, (c, c), 1)
    causal = (ss >= jj) if reverse else (ss <= jj)
    lane_head = lax.broadcasted_iota(jnp.int32, (c, GROUP_W), 1) // HEAD_DIM
    row_head = lax.broadcasted_iota(jnp.int32, (GROUP_W, c), 0) // HEAD_DIM
    stat_lane = lax.broadcasted_iota(jnp.int32, (c, ML_STAT_LANES), 1)
    head_lanes = (stat_lane >= nh) & (stat_lane < 2 * nh)
    expand = (lax.broadcasted_iota(jnp.int32, (ML_STAT_LANES, GROUP_W), 0) - nh
              == lax.broadcasted_iota(jnp.int32, (ML_STAT_LANES, GROUP_W), 1) // HEAD_DIM).astype(BF16)
    s_rh = lax.broadcasted_iota(jnp.int32, (GROUP_W, GROUP_W + LANES), 0) // HEAD_DIM
    s_col = lax.broadcasted_iota(jnp.int32, (GROUP_W, GROUP_W + LANES), 1)
    s_mask = jnp.where(s_col < GROUP_W, s_col // HEAD_DIM, s_col - GROUP_W - nh) == s_rh
    ones_cols = jnp.ones((c, LANES), BF16)

    for k in order:
        tok = slice(k * c, (k + 1) * c)
        qc, vc, ktc = q_ref[0, tok, :], v_ref[0, tok, :], kt_ref[0, :, tok]
        k_bd = jnp.concatenate([jnp.where(row_head == h, ktc, jnp.zeros_like(ktc)) for h in range(nh)], axis=1)
        v_bd = jnp.concatenate([jnp.where(lane_head == h, vc, jnp.zeros_like(vc)) for h in range(nh)], axis=0)
        qk = _dot(qc, k_bd)
        sa = stat_a[tok, :]
        ph = [jnp.exp(jnp.where(causal, row[h:h + 1, tok] - sa[:, h:h + 1], NEG)) * qk[:, h * c:(h + 1) * c]
              for h in range(nh)]
        pv = _dot(jnp.concatenate(ph, axis=1).astype(BF16), v_bd)
        p_sum = jnp.zeros((c, ML_STAT_LANES), F32)
        for h in range(nh):
            p_sum = jnp.where(stat_lane == nh + h, jnp.sum(ph[h], axis=-1, keepdims=True), p_sum)
        qs = _dot(qc, s.astype(BF16))
        den = sa * qs[:, GROUP_W:GROUP_W + ML_STAT_LANES] + p_sum
        rden = jnp.where(head_lanes, 1.0 / jnp.maximum(jnp.abs(den), stat_b[tok, :]), 0.0)
        wi_c = jnp.where(head_lanes, sa, 0.0)
        ex = _dot(jnp.concatenate([wi_c, rden], axis=0).astype(BF16), expand)
        out_ref[0, tok, :] = (ex[:c] * qs[:, :GROUP_W] + pv) * ex[c:]

        w_full = jnp.concatenate([jnp.broadcast_to(w[h:h + 1, tok], (HEAD_DIM, c)) for h in range(nh)], axis=0)
        d_full = jnp.concatenate([jnp.broadcast_to(d_old[k][h:h + 1, :], (HEAD_DIM, 1)) for h in range(nh)], axis=0)
        ktw = (ktc.astype(F32) * w_full).astype(BF16)
        s_loc = _dot(ktw, jnp.concatenate([vc, ones_cols], axis=1))
        s = d_full * s + jnp.where(s_mask, s_loc, 0.0)
    return s, m


def _mlstm_kernel(qf, vf, ktf, grf, qb, vb, ktb, grb, hf_ref, hb_ref, s_scr, m_scr, *, chunks):
    @pl.when(pl.program_id(1) == 0)
    def _():
        s_scr[...] = jnp.zeros_like(s_scr)
        m_scr[...] = jnp.zeros_like(m_scr)

    for d, (q_ref, v_ref, kt_ref, gr_ref, out_ref) in enumerate(((qf, vf, ktf, grf, hf_ref),
                                                                  (qb, vb, ktb, grb, hb_ref))):
        m0 = m_scr[d * SUBLANES:d * SUBLANES + N_HEADS, :][:, :1]
        s, m = _mlstm_direction(d, q_ref, v_ref, kt_ref, gr_ref[0], out_ref, s_scr[d], m0, chunks)
        s_scr[d] = s
        m_scr[d * SUBLANES:d * SUBLANES + N_HEADS, :] = jnp.broadcast_to(m, (N_HEADS, LANES))


def _mlstm(qd, vd, kdt, grow):
    bsz, seq, _ = qd.shape
    g = min(ML_CHUNKS_PER_STEP, seq // ML_CHUNK)
    blk = g * ML_CHUNK
    nb = seq // blk
    fwd = lambda b, i: (b, i, 0)
    bwd = lambda b, i: (b, nb - 1 - i, 0)
    fwd_t = lambda b, i: (b, 0, i)
    bwd_t = lambda b, i: (b, 0, nb - 1 - i)

    def specs(tok, chan):
        return [pl.BlockSpec((1, blk, GROUP_W), tok), pl.BlockSpec((1, blk, GROUP_W), tok),
                pl.BlockSpec((1, GROUP_W, blk), chan), pl.BlockSpec((1, ML_STAT_ROWS, blk), chan)]

    args = (qd, vd, kdt, grow)
    return pl.pallas_call(
        functools.partial(_mlstm_kernel, chunks=g),
        grid=(bsz, nb),
        in_specs=specs(fwd, fwd_t) + specs(bwd, bwd_t),
        out_specs=[pl.BlockSpec((1, blk, GROUP_W), fwd), pl.BlockSpec((1, blk, GROUP_W), bwd)],
        out_shape=[jax.ShapeDtypeStruct((bsz, seq, GROUP_W), F32)] * 2,
        scratch_shapes=[pltpu.VMEM((2, GROUP_W, GROUP_W + LANES), F32), pltpu.VMEM((2 * SUBLANES, LANES), F32)],
        compiler_params=_cparams("parallel", "arbitrary"),
        name="mlstm",
    )(*args, *args)


def _post_kernel(x_ref, ya_ref, f_ref, yc_ref, hf_ref, hb_ref, od_ref, p_ref,
                 wfn_ref, onorm_ref, gsum_ref, gbc_ref, wout_ref, nffn_ref, wgate_ref, wup_ref, wdown_ref,
                 pnorm_ref, wpg_ref, wpp_ref, fnorm_ref, o_ref, *, final):
    yb = _dot(jnp.transpose(f_ref[0]).astype(BF16), wfn_ref[...])
    yd = _sigmoid(od_ref[0]) * (hf_ref[0] + hb_ref[0])
    y = jnp.concatenate([ya_ref[0], yb, jnp.transpose(yc_ref[0]), yd], axis=-1)
    ss = _dot((y * y).astype(BF16), gsum_ref[...])
    rb = _dot(lax.rsqrt(ss * (1.0 / HEAD_DIM) + EPS).astype(BF16), gbc_ref[...])
    x = x_ref[0] + _dot((y * rb * onorm_ref[...]).astype(BF16), wout_ref[...])
    hn = _rms(x, nffn_ref[...]).astype(BF16)
    act = []
    for c in range(D_FF // FF_CHUNK):
        sl = slice(c * FF_CHUNK, (c + 1) * FF_CHUNK)
        g = _dot(hn, wgate_ref[:, sl])
        u = _dot(hn, wup_ref[:, sl])
        act.append((g * _sigmoid(g) * u).astype(BF16))
    x = x + _dot(jnp.concatenate(act, axis=1), wdown_ref[...])
    gate = _sigmoid(_dot(_rms(x, pnorm_ref[...]).astype(BF16), wpg_ref[...]))
    x = x + gate * _dot(p_ref[0, 0].astype(BF16), wpp_ref[...])
    if final:
        x = _rms(x, fnorm_ref[...])
    o_ref[0] = x


def _post(x, ya, f, yc, hf, hb, od, p, layer, lw, final_norm, final):
    bsz, seq, _ = x.shape
    tm = min(TOKEN_TILE, seq)
    ind = np.zeros((D_MODEL, LANES), np.float32)
    ind[np.arange(D_MODEL), np.arange(D_MODEL) // HEAD_DIM] = 1.0
    gsum = jnp.asarray(ind, BF16)
    gbc = jnp.asarray(ind.T, BF16)
    fw = lw["fnet_w"].astype(F32)
    wfn = jax.scipy.linalg.block_diag(*[fw[g] for g in range(fw.shape[0])]).astype(BF16)
    row = lambda v: v.astype(F32).reshape(1, D_MODEL)
    weights = (wfn, row(lw["out_norm"]), gsum, gbc, lw["w_out"].astype(BF16), row(lw["norm_ffn"]),
               lw["w_gate"].astype(BF16), lw["w_up"].astype(BF16), lw["w_down"].astype(BF16),
               row(lw["ple_norm"]), lw["w_ple_gate"].astype(BF16), lw["w_ple_proj"].astype(BF16), row(final_norm))
    tok = lambda width: pl.BlockSpec((1, tm, width), lambda b, t: (b, t, 0))
    chan = pl.BlockSpec((1, GROUP_W, tm), lambda b, t: (b, 0, t))
    return pl.pallas_call(
        functools.partial(_post_kernel, final=final),
        grid=(bsz, seq // tm),
        in_specs=[tok(D_MODEL), tok(GROUP_W), chan, chan] + [tok(GROUP_W)] * 3
                 + [pl.BlockSpec((1, 1, tm, PLE_DIM), lambda b, t: (layer, b, t, 0))]
                 + [_full_spec(w) for w in weights],
        out_specs=tok(D_MODEL),
        out_shape=jax.ShapeDtypeStruct((bsz, seq, D_MODEL), F32),
        compiler_params=_cparams("parallel", "parallel"),
        name="post",
    )(x, ya, f, yc, hf, hb, od, p, *weights)


def _layer_consts(lw, seq, hy_tabs):
    full = _hyena_filter_taps(seq, lw["hy_w1"], lw["hy_b1"], lw["hy_freq"], lw["hy_w2"], lw["hy_b2"], lw["hy_w3"],
                              lw["hy_decay"])
    return dict(spectra=_hyena_spectra(full, hy_tabs), na_bias=_na_bias_table(lw["attn_rpb"]))


def _trunk(x, p, layers, consts, final_norm, fn_tabs, hy_tabs):
    for i, (lw, lc) in enumerate(zip(layers, consts)):
        (qa, ka, va, zr, zi, uc, qd, vd, od, grow, kdt) = _inproj(
            x, lw["norm_mix"], lw["w_in"], lw["ml_gate_b"], (fn_tabs["fc"], fn_tabs["fs"]))
        ya = _na(qa, ka, va, lc["na_bias"])
        f = _fnet(zr, zi, fn_tabs)
        yc = _hyena(uc, lw["hy_conv_w"], lw["hy_conv_b"], lw["hy_skip"], lc["spectra"], hy_tabs)
        hf, hb = _mlstm(qd, vd, kdt, grow)
        x = _post(x, ya, f, yc, hf, hb, od, p, i, lw, final_norm, final=(i == len(layers) - 1))
    return x


_LAYER_KEYS = ("norm_mix", "w_in", "attn_rpb", "fnet_w", "hy_conv_w", "hy_conv_b", "hy_w1", "hy_b1", "hy_freq",
               "hy_w2", "hy_b2", "hy_w3", "hy_decay", "hy_skip", "ml_gate_b", "out_norm", "w_out", "norm_ffn",
               "w_gate", "w_up", "w_down", "ple_norm", "w_ple_gate", "w_ple_proj")


def kernel(x_prompt, x_sample, p_prompt, p_sample, norm_mix, w_in, attn_rpb, fnet_w, hy_conv_w, hy_conv_b, hy_w1,
           hy_b1, hy_freq, hy_w2, hy_b2, hy_w3, hy_decay, hy_skip, ml_gate_b, out_norm, w_out, norm_ffn, w_gate,
           w_up, w_down, ple_norm, w_ple_gate, w_ple_proj, final_norm):
    stacked = dict(zip(_LAYER_KEYS, (norm_mix, w_in, attn_rpb, fnet_w, hy_conv_w, hy_conv_b, hy_w1, hy_b1, hy_freq,
                                     hy_w2, hy_b2, hy_w3, hy_decay, hy_skip, ml_gate_b, out_norm, w_out, norm_ffn,
                                     w_gate, w_up, w_down, ple_norm, w_ple_gate, w_ple_proj)))
    depth = norm_mix.shape[0]
    layers = [{k: v[i] for k, v in stacked.items()} for i in range(depth)]
    outs = []
    cache = {}
    for x, p in ((x_prompt, p_prompt), (x_sample, p_sample)):
        seq = x.shape[1]
        if seq not in cache:
            fn_tabs = _fnet_tables(seq)
            hy_tabs = _hyena_tables(seq)
            cache[seq] = (fn_tabs, hy_tabs, [_layer_consts(lw, seq, hy_tabs) for lw in layers])
        fn_tabs, hy_tabs, consts = cache[seq]
        outs.append(_trunk(x, p, layers, consts, final_norm, fn_tabs, hy_tabs))
    return tuple(outs)
```
